```python
import jax
import jax.numpy as jnp
from jax import lax
import numpy as np

D_MODEL = 1024
BATCH = 8
SEQ = 4096
DEPTH = 4

GRID_W = 64
CTX_LEN = 256
EPS = 1e-6
N_BRANCH = 4
BRANCH_W = D_MODEL // 2

CONV_W = BRANCH_W
CONV_K = 3
LRU_W = BRANCH_W
LRU_HEADS = 8
LRU_HD = LRU_W // LRU_HEADS
LRU_CONV_K = 4
LRU_C = 8.0
CMLP_W = BRANCH_W
CMLP_GROUPS = 4
CMLP_GD = CMLP_W // CMLP_GROUPS
CHUNK = 128
MLA_HEADS = 8
QK_NOPE = 64
QK_ROPE = 32
V_HD = BRANCH_W // MLA_HEADS
Q_LORA = 384
KV_LORA = 256
MLA_SCALE = (QK_NOPE + QK_ROPE) ** -0.5
ROPE_THETA = 10000.0
Q_BLOCK = 128
D_FF = -(-8 * D_MODEL // (3 * 256)) * 256

SPLIT_SIZES = (LRU_W, KV_LORA, QK_ROPE, LRU_W, Q_LORA, CONV_W, CONV_W, CONV_W, CMLP_W, CMLP_W, N_BRANCH * D_MODEL)
IN_W = sum(SPLIT_SIZES)
CTX_STATE_COLS = LRU_W + KV_LORA + QK_ROPE

kernel_name = 'hybrid_flow_backbone'


def rms_norm(x, g):
    x32 = x.astype(jnp.float32)
    y = x32 * lax.rsqrt(jnp.mean(x32 * x32, axis=-1, keepdims=True) + EPS)
    return (y * g.astype(jnp.float32)).astype(x.dtype)


def layer_norm(x, g, b):
    x32 = x.astype(jnp.float32)
    mu = jnp.mean(x32, axis=-1, keepdims=True)
    var = jnp.mean(jnp.square(x32 - mu), axis=-1, keepdims=True)
    y = (x32 - mu) * lax.rsqrt(var + EPS) * g.astype(jnp.float32) + b.astype(jnp.float32)
    return y.astype(x.dtype)


def modulate(h, shift, scale):
    return h * (1 + scale) + shift


def split_cols(z, sizes):
    idx = tuple(int(i) for i in np.cumsum(sizes)[:-1])
    return jnp.split(z, idx, axis=-1)


def dwconv(x, w, pad_left):
    k, ch = w.shape
    return lax.conv_general_dilated(
        x, w[:, None, :].astype(x.dtype), window_strides=(1,),
        padding=((pad_left, k - 1 - pad_left),),
        dimension_numbers=('NWC', 'WIO', 'NWC'), feature_group_count=ch)


def axial_rope_tables(row, col):
    n_freq = QK_ROPE // 4
    inv = ROPE_THETA ** (-jnp.arange(n_freq, dtype=jnp.float32) / n_freq)
    ang_r = row.astype(jnp.float32)[:, None] * inv
    ang_c = col.astype(jnp.float32)[:, None] * inv
    return jnp.cos(ang_r), jnp.sin(ang_r), jnp.cos(ang_c), jnp.sin(ang_c)


def _rotate(x, cos, sin):
    m = x.shape[-1] // 2
    x1, x2 = x[..., :m], x[..., m:]
    cos = cos.astype(x.dtype)
    sin = sin.astype(x.dtype)
    return jnp.concatenate([x1 * cos - x2 * sin, x1 * sin + x2 * cos], axis=-1)


def axial_rope(x, cos_r, sin_r, cos_c, sin_c):
    half = QK_ROPE // 2
    return jnp.concatenate([_rotate(x[..., :half], cos_r, sin_r), _rotate(x[..., half:], cos_c, sin_c)], axis=-1)


def short_conv_mixer(a_b, a_c, a_x, w_conv):
    return a_b * dwconv(a_c * a_x, w_conv, CONV_K // 2)


def rglru_coeffs(xc, w_a, b_a, w_x, b_x, lam):
    bn, n, _ = xc.shape
    x32 = xc.astype(jnp.float32)
    xh = x32.reshape(bn, n, LRU_HEADS, LRU_HD)
    r = jax.nn.sigmoid(jnp.einsum('blhi,hij->blhj', xh, w_a.astype(jnp.float32)).reshape(bn, n, LRU_W) + b_a.astype(jnp.float32))
    i = jax.nn.sigmoid(jnp.einsum('blhi,hij->blhj', xh, w_x.astype(jnp.float32)).reshape(bn, n, LRU_W) + b_x.astype(jnp.float32))
    log_a = -LRU_C * r * jax.nn.softplus(-lam.astype(jnp.float32))
    a = jnp.exp(log_a)
    b = jnp.sqrt(-jnp.expm1(2.0 * log_a)) * (i * x32)
    return a, b


def _scan_combine(left, right):
    a_l, b_l = left
    a_r, b_r = right
    return a_l * a_r, a_r * b_l + b_r


def linear_scan(a, b, h0, reverse):
    if h0 is not None:
        edge = -1 if reverse else 0
        b = b.at[:, edge].add(a[:, edge] * h0)
    return lax.associative_scan(_scan_combine, (a, b), reverse=reverse, axis=1)[1]


def rglru_bidirectional(x_lat, x_ctx, conv_w, conv_b, w_a, b_a, w_x, b_x, lam, need_ctx):
    xl = dwconv(x_lat, conv_w, LRU_CONV_K // 2) + conv_b
    xc = dwconv(x_ctx, conv_w, LRU_CONV_K // 2) + conv_b
    lat_sum = None
    ctx_sum = None
    for d, rev in enumerate((False, True)):
        a_c, b_c = rglru_coeffs(xc, w_a[d], b_a[d], w_x[d], b_x[d], lam[d])
        h_c = linear_scan(a_c, b_c, None, rev)
        h0 = h_c[:, 0] if rev else h_c[:, -1]
        a_l, b_l = rglru_coeffs(xl, w_a[d], b_a[d], w_x[d], b_x[d], lam[d])
        h_l = linear_scan(a_l, b_l, h0, rev)
        lat_sum = h_l if lat_sum is None else lat_sum + h_l
        if need_ctx:
            ctx_sum = h_c if ctx_sum is None else ctx_sum + h_c
    ctx_out = ctx_sum.astype(x_ctx.dtype) if need_ctx else None
    return lat_sum.astype(x_lat.dtype), ctx_out


def chunk_mlp_mixer(u_pre, v_pre, ln_g, ln_b, w_s, b_s):
    bn, n, _ = u_pre.shape
    u = jax.nn.gelu(u_pre)
    v = layer_norm(jax.nn.gelu(v_pre), ln_g, ln_b)
    vc = v.reshape(bn, n // CHUNK, CHUNK, CMLP_GROUPS, CMLP_GD)
    mixed = jnp.einsum('gpq,bnqgd->bnpgd', w_s, vc) + b_s.T[None, None, :, :, None]
    return u * mixed.reshape(bn, n, CMLP_W)


def mla_keys_values(kv_lat, k_rope, kv_norm_g, w_kv_up, rope):
    bn, n, _ = kv_lat.shape
    kv = (rms_norm(kv_lat, kv_norm_g) @ w_kv_up).reshape(bn, n, MLA_HEADS, QK_NOPE + V_HD)
    k_nope, v = kv[..., :QK_NOPE], kv[..., QK_NOPE:]
    if rope is not None:
        k_rope = axial_rope(k_rope, *rope)
    k_rope = jnp.broadcast_to(k_rope[:, :, None, :], (bn, n, MLA_HEADS, QK_ROPE))
    return jnp.concatenate([k_nope, k_rope], axis=-1), v


def mla_queries(q_lat, q_norm_g, w_q_up, rope):
    bn, n, _ = q_lat.shape
    q = (rms_norm(q_lat, q_norm_g) @ w_q_up).reshape(bn, n, MLA_HEADS, QK_NOPE + QK_ROPE)
    if rope is None:
        return q
    return jnp.concatenate([q[..., :QK_NOPE], axial_rope(q[..., QK_NOPE:], *rope)], axis=-1)


def softmax_attention(q, k, v):
    s = jnp.einsum('bqhd,bkhd->bhqk', q, k).astype(jnp.float32) * MLA_SCALE
    p = jax.nn.softmax(s, axis=-1).astype(v.dtype)
    return jnp.einsum('bhqk,bkhd->bqhd', p, v)


def latent_attention(q, k_all, v_all):
    bn, n, h, dk = q.shape
    qb = jnp.moveaxis(q.reshape(bn, n // Q_BLOCK, Q_BLOCK, h, dk), 1, 0)
    ob = lax.map(lambda qq: softmax_attention(qq, k_all, v_all), qb)
    return jnp.moveaxis(ob, 0, 1).reshape(bn, n, h * V_HD)


def merge_branches(ys, gate_pre, w_branch, w_out):
    d = w_out.shape[0]
    g = jax.nn.sigmoid(gate_pre)
    merged = g[..., :d] * (ys[0] @ w_branch[0])
    for n in range(1, N_BRANCH):
        merged = merged + g[..., n * d:(n + 1) * d] * (ys[n] @ w_branch[n])
    return merged @ w_out


def swiglu(h, w1, w3, w2):
    return (jax.nn.silu(h @ w1) * (h @ w3)) @ w2


def setup_inputs(seed: int = 0) -> dict:
    key = jax.random.key(seed)
    k = jax.random.split(key, 32)
    f32 = jnp.float32

    def nrm(i, shape, scale):
        return scale * jax.random.normal(k[i], shape, f32)

    L = DEPTH
    a_target = jax.random.uniform(k[16], (L, 2, LRU_W), f32, 0.9, 0.999)
    a_base = a_target ** (1.0 / LRU_C)
    lam = jnp.log(a_base) - jnp.log1p(-a_base)
    return {
        'x': nrm(0, (BATCH, SEQ, D_MODEL), 1.0),
        'c': nrm(1, (BATCH, D_MODEL), 1.0),
        'ctx': nrm(2, (BATCH, CTX_LEN, D_MODEL), 1.0),
        'c_ctx': nrm(3, (D_MODEL,), 1.0),
        'w_mod': nrm(4, (L, D_MODEL, 6 * D_MODEL), 0.5 * D_MODEL ** -0.5),
        'b_mod': nrm(5, (L, 6 * D_MODEL), 0.01),
        'norm1_g': 1.0 + nrm(6, (L, D_MODEL), 0.05),
        'norm2_g': 1.0 + nrm(7, (L, D_MODEL), 0.05),
        'w_in': nrm(8, (L, D_MODEL, IN_W), D_MODEL ** -0.5),
        'conv_a_w': nrm(9, (L, CONV_K, CONV_W), CONV_K ** -0.5),
        'lru_conv_w': nrm(10, (L, LRU_CONV_K, LRU_W), LRU_CONV_K ** -0.5),
        'lru_conv_b': nrm(11, (L, LRU_W), 0.01),
        'lru_w_a': nrm(12, (L, 2, LRU_HEADS, LRU_HD, LRU_HD), LRU_HD ** -0.5),
        'lru_b_a': nrm(13, (L, 2, LRU_W), 0.01),
        'lru_w_x': nrm(14, (L, 2, LRU_HEADS, LRU_HD, LRU_HD), LRU_HD ** -0.5),
        'lru_b_x': nrm(15, (L, 2, LRU_W), 0.01),
        'lru_lam': lam,
        'cmlp_ln_g': 1.0 + nrm(17, (L, CMLP_W), 0.05),
        'cmlp_ln_b': nrm(18, (L, CMLP_W), 0.01),
        'cmlp_w_s': nrm(19, (L, CMLP_GROUPS, CHUNK, CHUNK), CHUNK ** -0.5),
        'cmlp_b_s': 1.0 + nrm(20, (L, CMLP_GROUPS, CHUNK), 0.05),
        'mla_q_norm_g': 1.0 + nrm(21, (L, Q_LORA), 0.05),
        'mla_kv_norm_g': 1.0 + nrm(22, (L, KV_LORA), 0.05),
        'mla_w_q_up': nrm(23, (L, Q_LORA, MLA_HEADS * (QK_NOPE + QK_ROPE)), Q_LORA ** -0.5),
        'mla_w_kv_up': nrm(24, (L, KV_LORA, MLA_HEADS * (QK_NOPE + V_HD)), KV_LORA ** -0.5),
        'w_branch': nrm(25, (L, N_BRANCH, BRANCH_W, D_MODEL), BRANCH_W ** -0.5),
        'w_out': nrm(26, (L, D_MODEL, D_MODEL), D_MODEL ** -0.5),
        'w_ff1': nrm(27, (L, D_MODEL, D_FF), D_MODEL ** -0.5),
        'w_ff3': nrm(28, (L, D_MODEL, D_FF), D_MODEL ** -0.5),
        'w_ff2': nrm(29, (L, D_FF, D_MODEL), D_FF ** -0.5),
        'final_norm_g': 1.0 + nrm(30, (D_MODEL,), 0.05),
    }


def reference(x, c, ctx, c_ctx, w_mod, b_mod, norm1_g, norm2_g, w_in, conv_a_w,
              lru_conv_w, lru_conv_b, lru_w_a, lru_b_a, lru_w_x, lru_b_x, lru_lam,
              cmlp_ln_g, cmlp_ln_b, cmlp_w_s, cmlp_b_s, mla_q_norm_g, mla_kv_norm_g,
              mla_w_q_up, mla_w_kv_up, w_branch, w_out, w_ff1, w_ff3, w_ff2, final_norm_g):
    d = x.shape[-1]
    n_lat = x.shape[1]
    rows = n_lat // GRID_W
    row = jnp.repeat(jnp.arange(rows), GRID_W)
    col = jnp.tile(jnp.arange(GRID_W), rows)
    rope_k = axial_rope_tables(row, col)
    rope_q = tuple(t[:, None, :] for t in rope_k)
    s_lat = jax.nn.silu(c)
    s_ctx = jax.nn.silu(c_ctx)
    xc = ctx
    for l in range(DEPTH):
        last = l == DEPTH - 1
        sh1, sc1, g1, sh2, sc2, g2 = (m[:, None, :] for m in jnp.split(s_lat @ w_mod[l] + b_mod[l], 6, axis=-1))
        if last:
            sh1c, sc1c = jnp.split(s_ctx @ w_mod[l][:, :2 * d] + b_mod[l][:2 * d], 2, axis=-1)
        else:
            sh1c, sc1c, g1c, sh2c, sc2c, g2c = jnp.split(s_ctx @ w_mod[l] + b_mod[l], 6, axis=-1)

        h = modulate(rms_norm(x, norm1_g[l]), sh1, sc1)
        hc = modulate(rms_norm(xc, norm1_g[l]), sh1c, sc1c)
        (lru_x, kv_lat, k_rope, lru_g, q_lat, a_b, a_c, a_x, c_u, c_v, gate_pre) = split_cols(h @ w_in[l], SPLIT_SIZES)
        if last:
            lru_xc, kv_latc, k_ropec = split_cols(hc @ w_in[l][:, :CTX_STATE_COLS], SPLIT_SIZES[:3])
        else:
            (lru_xc, kv_latc, k_ropec, lru_gc, q_latc, a_bc, a_cc, a_xc, c_uc, c_vc, gate_prec) = split_cols(hc @ w_in[l], SPLIT_SIZES)

        h_lru, hc_lru = rglru_bidirectional(lru_x, lru_xc, lru_conv_w[l], lru_conv_b[l], lru_w_a[l], lru_b_a[l],
                                            lru_w_x[l], lru_b_x[l], lru_lam[l], not last)
        y_b = jax.nn.gelu(lru_g) * h_lru
        k_lat, v_lat = mla_keys_values(kv_lat, k_rope, mla_kv_norm_g[l], mla_w_kv_up[l], rope_k)
        k_ctx, v_ctx = mla_keys_values(kv_latc, k_ropec, mla_kv_norm_g[l], mla_w_kv_up[l], None)
        q = mla_queries(q_lat, mla_q_norm_g[l], mla_w_q_up[l], rope_q)
        y_d = latent_attention(q, jnp.concatenate([k_lat, k_ctx], axis=1), jnp.concatenate([v_lat, v_ctx], axis=1))
        y_a = short_conv_mixer(a_b, a_c, a_x, conv_a_w[l])
        y_c = chunk_mlp_mixer(c_u, c_v, cmlp_ln_g[l], cmlp_ln_b[l], cmlp_w_s[l], cmlp_b_s[l])

        x = x + g1 * merge_branches((y_a, y_b, y_c, y_d), gate_pre, w_branch[l], w_out[l])
        h2 = modulate(rms_norm(x, norm2_g[l]), sh2, sc2)
        x = x + g2 * swiglu(h2, w_ff1[l], w_ff3[l], w_ff2[l])

        if not last:
            yc_b = jax.nn.gelu(lru_gc) * hc_lru
            qc = mla_queries(q_latc, mla_q_norm_g[l], mla_w_q_up[l], None)
            oc = softmax_attention(qc, k_ctx, v_ctx)
            yc_d = oc.reshape(oc.shape[0], oc.shape[1], MLA_HEADS * V_HD)
            yc_a = short_conv_mixer(a_bc, a_cc, a_xc, conv_a_w[l])
            yc_c = chunk_mlp_mixer(c_uc, c_vc, cmlp_ln_g[l], cmlp_ln_b[l], cmlp_w_s[l], cmlp_b_s[l])
            xc = xc + g1c * merge_branches((yc_a, yc_b, yc_c, yc_d), gate_prec, w_branch[l], w_out[l])
            h2c = modulate(rms_norm(xc, norm2_g[l]), sh2c, sc2c)
            xc = xc + g2c * swiglu(h2c, w_ff1[l], w_ff3[l], w_ff2[l])
    return rms_norm(x, final_norm_g)
```

```python
import functools

import jax
import jax.numpy as jnp
from jax import lax
from jax.experimental import pallas as pl
from jax.experimental.pallas import tpu as pltpu

F32 = jnp.float32
BF16 = jnp.bfloat16

EPS = 1e-6
GRID_W = 64
N_BRANCH = 4
LRU_HEADS = 8
LRU_C = 8.0
CMLP_GROUPS = 4
CHUNK = 128
MLA_HEADS = 8
QK_NOPE = 64
QK_ROPE = 32
Q_LORA = 384
KV_LORA = 256
ROPE_THETA = 10000.0
HEAD_PAD = 128
SUBLANES = 8
VMEM_LIMIT = 56 * 1024 * 1024


def _params(sem, vmem=VMEM_LIMIT):
    return pltpu.CompilerParams(dimension_semantics=sem, vmem_limit_bytes=vmem)


def _const_spec(shape):
    zeros = (0,) * len(shape)
    return pl.BlockSpec(shape, lambda *_: zeros)


def _gelu(x):
    return jax.nn.gelu(x)


def _norm_mod(x, g, shift, scale):
    y = x * lax.rsqrt(jnp.mean(x * x, axis=-1, keepdims=True) + EPS) * g
    return y * (1.0 + scale) + shift


def _dot(a, b):
    return jnp.dot(a, b, preferred_element_type=F32)


def _mod_kernel(s_ref, w_ref, b_ref, o_ref):
    s = s_ref[...]
    s = s * jax.nn.sigmoid(s)
    o_ref[0] = _dot(s.astype(BF16), w_ref[0].astype(BF16)) + b_ref[0]


def _modulation(s_rows, w_mod, b_mod):
    n_layer, d, d6 = w_mod.shape
    r = s_rows.shape[0]
    return pl.pallas_call(
        _mod_kernel,
        grid=(n_layer, d6 // d),
        in_specs=[
            pl.BlockSpec((r, d), lambda l, j: (0, 0)),
            pl.BlockSpec((1, d, d), lambda l, j: (l, 0, j)),
            pl.BlockSpec((1, 1, d), lambda l, j: (l, 0, j)),
        ],
        out_specs=pl.BlockSpec((1, r, d), lambda l, j: (l, 0, j)),
        out_shape=jax.ShapeDtypeStruct((n_layer, r, d6), F32),
        compiler_params=_params(("parallel", "parallel")),
        name="modulation",
    )(s_rows, w_mod, b_mod.reshape(n_layer, 1, d6))


def _inproj_kernel(x_ref, sh_ref, sc_ref, g_ref, *refs):
    n = len(refs) // 2
    h = _norm_mod(x_ref[0], g_ref[...], sh_ref[0], sc_ref[0]).astype(BF16)
    for w_ref, o_ref in zip(refs[:n], refs[n:]):
        o_ref[0] = _dot(h, w_ref[...]).astype(o_ref.dtype)


def _inproj(x, mod3, row_of, g, weights, tm):
    b, l, d = x.shape
    tm = min(tm, l)
    in_specs = [
        pl.BlockSpec((1, tm, d), lambda bi, i: (bi, i, 0)),
        pl.BlockSpec((1, 1, d), lambda bi, i: (row_of(bi), 0, 0)),
        pl.BlockSpec((1, 1, d), lambda bi, i: (row_of(bi), 0, 1)),
        _const_spec((1, d)),
    ] + [_const_spec(w.shape) for w in weights]
    out_specs = [pl.BlockSpec((1, tm, w.shape[1]), lambda bi, i: (bi, i, 0)) for w in weights]
    out_shape = [jax.ShapeDtypeStruct((b, l, w.shape[1]), F32) for w in weights]
    return pl.pallas_call(
        _inproj_kernel,
        grid=(b, l // tm),
        in_specs=in_specs,
        out_specs=out_specs,
        out_shape=out_shape,
        compiler_params=_params(("parallel", "parallel")),
        name="inproj",
    )(x, mod3, mod3, g.reshape(1, d), *weights)


def _halo_specs(tm, l, c, lane_blk):
    nb = l // SUBLANES
    per = tm // SUBLANES
    prev = pl.BlockSpec((1, SUBLANES, c), lambda bi, i: (bi, jnp.maximum(i * per - 1, 0), lane_blk))
    nxt = pl.BlockSpec((1, SUBLANES, c), lambda bi, i: (bi, jnp.minimum((i + 1) * per, nb - 1), lane_blk))
    return prev, nxt


def _shift_down(p, first_row, k):
    tm = p.shape[0]
    row = lax.broadcasted_iota(jnp.int32, p.shape, 0)
    out = pltpu.roll(p, k, 0)
    for j in range(k):
        out = jnp.where(row == j, first_row[j:j + 1, :], out)
    del tm
    return out


def _shift_up(p, last_row):
    tm = p.shape[0]
    row = lax.broadcasted_iota(jnp.int32, p.shape, 0)
    return jnp.where(row == tm - 1, last_row, pltpu.roll(p, tm - 1, 0))


def _conv_kernel(ab_ref, ac_ref, ax_ref, acp_ref, axp_ref, acn_ref, axn_ref, w_ref, o_ref, *, n_tiles):
    i = pl.program_id(1)
    p = ac_ref[0] * ax_ref[0]
    p_prev = jnp.where(i > 0, acp_ref[0, SUBLANES - 1:, :] * axp_ref[0, SUBLANES - 1:, :], 0.0)
    p_next = jnp.where(i < n_tiles - 1, acn_ref[0, 0:1, :] * axn_ref[0, 0:1, :], 0.0)
    w = w_ref[...]
    conv = w[0:1] * _shift_down(p, p_prev, 1) + w[1:2] * p + w[2:3] * _shift_up(p, p_next)
    o_ref[0] = (ab_ref[0] * conv).astype(o_ref.dtype)


def _conv_mixer(z_conv, w_conv, tm):
    b, l, c3 = z_conv.shape
    c = c3 // 3
    tm = min(tm, l)
    main = lambda blk: pl.BlockSpec((1, tm, c), lambda bi, i: (bi, i, blk))
    acp, acn = _halo_specs(tm, l, c, 1)
    axp, axn = _halo_specs(tm, l, c, 2)
    return pl.pallas_call(
        functools.partial(_conv_kernel, n_tiles=l // tm),
        grid=(b, l // tm),
        in_specs=[main(0), main(1), main(2), acp, axp, acn, axn, _const_spec(w_conv.shape)],
        out_specs=pl.BlockSpec((1, tm, c), lambda bi, i: (bi, i, 0)),
        out_shape=jax.ShapeDtypeStruct((b, l, c), BF16),
        compiler_params=_params(("parallel", "parallel")),
        name="conv_mixer",
    )(z_conv, z_conv, z_conv, z_conv, z_conv, z_conv, z_conv, w_conv)


def _lru_coeffs(x_ref, xp_ref, xn_ref, cw_ref, cb_ref, wg_ref, bg_ref, lam_ref, i, n_tiles):
    x = x_ref[0]
    c = x.shape[1]
    prev = jnp.where(i > 0, xp_ref[0, SUBLANES - 2:, :], 0.0)
    nxt = jnp.where(i < n_tiles - 1, xn_ref[0, 0:1, :], 0.0)
    cw = cw_ref[...]
    x_m1 = _shift_down(x, prev[1:2], 1)
    x_m2 = _shift_down(x, prev, 2)
    xl = cw[0:1] * x_m2 + cw[1:2] * x_m1 + cw[2:3] * x + cw[3:4] * _shift_up(x, nxt) + cb_ref[...]
    gates = _dot(xl.astype(BF16), wg_ref[0]) + bg_ref[0]
    r = jax.nn.sigmoid(gates[:, :c])
    gi = jax.nn.sigmoid(gates[:, c:])
    lam = lam_ref[0]
    softplus_neg = jnp.maximum(-lam, 0.0) + jnp.log1p(jnp.exp(-jnp.abs(lam)))
    log_a = (-LRU_C * softplus_neg) * r
    a = jnp.exp(log_a)
    bb = jnp.sqrt(1.0 - a * a) * (gi * xl)
    return a, bb


def _scan_tile(a, bb, a_scr, b_scr, h_scr, carry_ref, reverse):
    tm = a.shape[0]
    r8 = lax.broadcasted_iota(jnp.int32, a.shape, 0) & (SUBLANES - 1)
    acc_a, acc_b = a, bb
    for k in (1, 2, 4):
        if reverse:
            keep = r8 < SUBLANES - k
            shift = tm - k
        else:
            keep = r8 >= k
            shift = k
        a_sh = jnp.where(keep, pltpu.roll(acc_a, shift, 0), 1.0)
        b_sh = jnp.where(keep, pltpu.roll(acc_b, shift, 0), 0.0)
        acc_b = acc_a * b_sh + acc_b
        acc_a = acc_a * a_sh
    a_scr[...] = acc_a
    b_scr[...] = acc_b
    groups = tm // SUBLANES

    def body(gi, h_in):
        g = groups - 1 - gi if reverse else gi
        rows = pl.ds(pl.multiple_of(g * SUBLANES, SUBLANES), SUBLANES)
        h8 = b_scr[rows, :] + a_scr[rows, :] * h_in
        h_scr[rows, :] = h8
        return h8[0:1, :] if reverse else h8[SUBLANES - 1:, :]

    carry_ref[...] = lax.fori_loop(0, groups, body, carry_ref[...], unroll=4)


def _lru_fwd_kernel(x_ref, xp_ref, xn_ref, h0_ref, cw_ref, cb_ref, wg_ref, bg_ref, lam_ref,
                    h_ref, hlast_ref, a_scr, b_scr, carry, *, n_tiles):
    i = pl.program_id(1)

    @pl.when(i == 0)
    def _():
        carry[...] = h0_ref[0]

    a, bb = _lru_coeffs(x_ref, xp_ref, xn_ref, cw_ref, cb_ref, wg_ref, bg_ref, lam_ref, i, n_tiles)
    _scan_tile(a, bb, a_scr, b_scr, h_ref.at[0], carry, reverse=False)
    hlast_ref[0] = carry[...]


def _lru_bwd_kernel(x_ref, xp_ref, xn_ref, h0_ref, hf_ref, g_ref, cw_ref, cb_ref, wg_ref, bg_ref, lam_ref,
                    y_ref, hfirst_ref, a_scr, b_scr, h_scr, carry, *, n_tiles):
    i = pl.program_id(1)

    @pl.when(i == 0)
    def _():
        carry[...] = h0_ref[0]

    tile = n_tiles - 1 - i
    a, bb = _lru_coeffs(x_ref, xp_ref, xn_ref, cw_ref, cb_ref, wg_ref, bg_ref, lam_ref, tile, n_tiles)
    _scan_tile(a, bb, a_scr, b_scr, h_scr, carry, reverse=True)
    hfirst_ref[0] = carry[...]
    y_ref[0] = (_gelu(g_ref[0]) * (hf_ref[0] + h_scr[...])).astype(y_ref.dtype)


def _lru_mixer(z_lru, h0_f, h0_b, conv_w, conv_b, wg, bg, lam, tm):
    b, l, c2 = z_lru.shape
    c = c2 // 2
    tm = min(tm, l)
    nt = l // tm
    per = tm // SUBLANES
    nb = l // SUBLANES
    state = jax.ShapeDtypeStruct((b, 1, c), F32)
    state_spec = pl.BlockSpec((1, 1, c), lambda bi, i: (bi, 0, 0))
    cp = _params(("parallel", "arbitrary"))
    par = lambda d: [_const_spec(conv_w.shape), _const_spec((1, c)),
                     pl.BlockSpec((1, c, c2), lambda bi, i: (d, 0, 0)),
                     pl.BlockSpec((1, 1, c2), lambda bi, i: (d, 0, 0)),
                     pl.BlockSpec((1, 1, c), lambda bi, i: (d, 0, 0))]
    args = (conv_w, conv_b.reshape(1, c), wg, bg, lam)

    prev_f, next_f = _halo_specs(tm, l, c, 0)
    hf, hf_last = pl.pallas_call(
        functools.partial(_lru_fwd_kernel, n_tiles=nt),
        grid=(b, nt),
        in_specs=[pl.BlockSpec((1, tm, c), lambda bi, i: (bi, i, 0)), prev_f, next_f, state_spec] + par(0),
        out_specs=[pl.BlockSpec((1, tm, c), lambda bi, i: (bi, i, 0)), state_spec],
        out_shape=[jax.ShapeDtypeStruct((b, l, c), F32), state],
        scratch_shapes=[pltpu.VMEM((tm, c), F32), pltpu.VMEM((tm, c), F32), pltpu.VMEM((1, c), F32)],
        compiler_params=cp,
        name="lru_fwd",
    )(z_lru, z_lru, z_lru, h0_f, *args)

    rev = lambda i: nt - 1 - i
    prev_b = pl.BlockSpec((1, SUBLANES, c), lambda bi, i: (bi, jnp.maximum(rev(i) * per - 1, 0), 0))
    next_b = pl.BlockSpec((1, SUBLANES, c), lambda bi, i: (bi, jnp.minimum((rev(i) + 1) * per, nb - 1), 0))
    y, hb_first = pl.pallas_call(
        functools.partial(_lru_bwd_kernel, n_tiles=nt),
        grid=(b, nt),
        in_specs=[pl.BlockSpec((1, tm, c), lambda bi, i: (bi, rev(i), 0)), prev_b, next_b, state_spec,
                  pl.BlockSpec((1, tm, c), lambda bi, i: (bi, rev(i), 0)),
                  pl.BlockSpec((1, tm, c), lambda bi, i: (bi, rev(i), 1))] + par(1),
        out_specs=[pl.BlockSpec((1, tm, c), lambda bi, i: (bi, rev(i), 0)), state_spec],
        out_shape=[jax.ShapeDtypeStruct((b, l, c), BF16), state],
        scratch_shapes=[pltpu.VMEM((tm, c), F32), pltpu.VMEM((tm, c), F32), pltpu.VMEM((tm, c), F32),
                        pltpu.VMEM((1, c), F32)],
        compiler_params=cp,
        name="lru_bwd",
    )(z_lru, z_lru, z_lru, h0_b, hf, z_lru, *args)
    return y, hf_last, hb_first


def _cmlp_kernel(u_ref, v_ref, g_ref, b_ref, ws_ref, bs_ref, o_ref):
    tm, c = u_ref.shape[1], u_ref.shape[2]
    gd = c // CMLP_GROUPS
    v = _gelu(v_ref[0])
    mu = jnp.mean(v, axis=-1, keepdims=True)
    var = jnp.mean(jnp.square(v - mu), axis=-1, keepdims=True)
    v = ((v - mu) * lax.rsqrt(var + EPS) * g_ref[...] + b_ref[...]).astype(BF16)
    bs = bs_ref[...]
    for ck in range(tm // CHUNK):
        rows = slice(ck * CHUNK, (ck + 1) * CHUNK)
        for g in range(CMLP_GROUPS):
            cols = slice(g * gd, (g + 1) * gd)
            mixed = _dot(ws_ref[g], v[rows, cols]) + bs[:, g:g + 1]
            o_ref[0, rows, cols] = (_gelu(u_ref[0, rows, cols]) * mixed).astype(o_ref.dtype)


def _cmlp_mixer(z_cmlp, ln_g, ln_b, w_s, b_s, tm):
    b, l, c2 = z_cmlp.shape
    c = c2 // 2
    tm = min(tm, l)
    return pl.pallas_call(
        _cmlp_kernel,
        grid=(b, l // tm),
        in_specs=[pl.BlockSpec((1, tm, c), lambda bi, i: (bi, i, 0)),
                  pl.BlockSpec((1, tm, c), lambda bi, i: (bi, i, 1)),
                  _const_spec((1, c)), _const_spec((1, c)), _const_spec(w_s.shape),
                  _const_spec((CHUNK, CMLP_GROUPS))],
        out_specs=pl.BlockSpec((1, tm, c), lambda bi, i: (bi, i, 0)),
        out_shape=jax.ShapeDtypeStruct((b, l, c), BF16),
        compiler_params=_params(("parallel", "parallel")),
        name="chunk_mlp",
    )(z_cmlp, z_cmlp, ln_g.reshape(1, c), ln_b.reshape(1, c), w_s, b_s.T)


def _rms(x, g):
    return x * lax.rsqrt(jnp.mean(x * x, axis=-1, keepdims=True) + EPS) * g


def _mla_proj_kernel(z_ref, qg_ref, kvg_ref, wq_ref, wqr_ref, wk_ref, wv_ref, place_ref,
                     qc_ref, qs_ref, kcs_ref, q_ref, k_ref, v_ref):
    z = z_ref[0]
    kvn = _rms(z[:, :KV_LORA], kvg_ref[...]).astype(BF16)
    qn = _rms(z[:, KV_LORA:KV_LORA + Q_LORA], qg_ref[...]).astype(BF16)
    kr = (z[:, KV_LORA + Q_LORA:] * kcs_ref[...]).astype(BF16)
    k = _dot(kvn, wk_ref[...]) + _dot(kr, place_ref[...])
    k_ref[0] = k.astype(k_ref.dtype)
    v_ref[0] = _dot(kvn, wv_ref[...]).astype(v_ref.dtype)
    heads = q_ref.shape[2] // HEAD_PAD
    qc = jnp.concatenate([qc_ref[...]] * heads, axis=1)
    qs = jnp.concatenate([qs_ref[...]] * heads, axis=1)
    q = _dot(qn, wq_ref[...]) * qc + _dot(qn, wqr_ref[...]) * qs
    q_ref[0] = q.astype(q_ref.dtype)


def _mla_proj(z_mla, q_g, kv_g, wq, wqr, wk, wv, place, q_cos, q_sin, k_cs, tm):
    b, l, zc = z_mla.shape
    tm = min(tm, l)
    hp = wq.shape[1]
    tab = pl.BlockSpec((tm, HEAD_PAD), lambda bi, i: (i, 0))
    tok = lambda n: pl.BlockSpec((1, tm, n), lambda bi, i: (bi, i, 0))
    return pl.pallas_call(
        _mla_proj_kernel,
        grid=(b, l // tm),
        in_specs=[tok(zc), _const_spec((1, Q_LORA)), _const_spec((1, KV_LORA)),
                  _const_spec(wq.shape), _const_spec(wqr.shape), _const_spec(wk.shape),
                  _const_spec(wv.shape), _const_spec(place.shape), tab, tab, tab],
        out_specs=[tok(hp), tok(hp), tok(wv.shape[1])],
        out_shape=[jax.ShapeDtypeStruct((b, l, hp), BF16), jax.ShapeDtypeStruct((b, l, hp), BF16),
                   jax.ShapeDtypeStruct((b, l, wv.shape[1]), BF16)],
        compiler_params=_params(("parallel", "parallel")),
        name="mla_proj",
    )(z_mla, q_g.reshape(1, Q_LORA), kv_g.reshape(1, KV_LORA), wq, wqr, wk, wv, place, q_cos, q_sin, k_cs)


def _attn_kernel(q_ref, *refs, scale, n_kv):
    kv_refs, o_ref = refs[:2 * n_kv], refs[2 * n_kv]
    c = scale * 1.4426950408889634
    nt = (((1,), (1,)), ((), ()))
    outs = []
    for hh in range(2):
        lanes = slice(hh * HEAD_PAD, (hh + 1) * HEAD_PAD)
        q = q_ref[0, :, lanes]
        scores = [lax.dot_general(q, kv_refs[2 * j][0, :, lanes], nt, preferred_element_type=F32)
                  for j in range(n_kv)]
        m = functools.reduce(jnp.maximum, [jnp.max(s, axis=-1, keepdims=True) for s in scores])
        mc = m * c
        den = None
        acc = None
        for j in range(n_kv):
            p = jnp.exp2(scores[j] * c - mc)
            pv = _dot(p.astype(BF16), kv_refs[2 * j + 1][0])
            ps = jnp.sum(p, axis=-1, keepdims=True)
            den = ps if den is None else den + ps
            acc = pv if acc is None else acc + pv
        outs.append(acc * (1.0 / den))
    lane = lax.broadcasted_iota(jnp.int32, outs[0].shape, 1)
    o_ref[0] = jnp.where(lane < HEAD_PAD // 2, outs[0], outs[1]).astype(o_ref.dtype)


def _attention(q, kvs, scale, tq):
    b, l, hp = q.shape
    heads = hp // HEAD_PAD
    tq = min(tq, l)
    in_specs = [pl.BlockSpec((1, tq, 2 * HEAD_PAD), lambda bi, h, i: (bi, i, h))]
    args = [q]
    for k, v in kvs:
        lk = k.shape[1]
        in_specs.append(pl.BlockSpec((1, lk, 2 * HEAD_PAD), lambda bi, h, i: (bi, 0, h)))
        in_specs.append(pl.BlockSpec((1, lk, HEAD_PAD), lambda bi, h, i: (bi, 0, h)))
        args += [k, v]
    return pl.pallas_call(
        functools.partial(_attn_kernel, scale=scale, n_kv=len(kvs)),
        grid=(b, heads // 2, l // tq),
        in_specs=in_specs,
        out_specs=pl.BlockSpec((1, tq, HEAD_PAD), lambda bi, h, i: (bi, i, h)),
        out_shape=jax.ShapeDtypeStruct((b, l, heads * HEAD_PAD // 2), BF16),
        compiler_params=_params(("parallel", "parallel", "parallel")),
        name="attention",
    )(*args)


def _merge_kernel(x_ref, sh_ref, sc_ref, gt_ref, g_ref, ya_ref, yb_ref, yc_ref, yd_ref,
                  wg_ref, wb_ref, wo_ref, o_ref):
    x = x_ref[0]
    d = x.shape[1]
    h = _norm_mod(x, g_ref[...], sh_ref[0], sc_ref[0]).astype(BF16)
    merged = None
    for n, y_ref in enumerate((ya_ref, yb_ref, yc_ref, yd_ref)):
        gate = jax.nn.sigmoid(_dot(h, wg_ref[:, n * d:(n + 1) * d]))
        term = gate * _dot(y_ref[0], wb_ref[n])
        merged = term if merged is None else merged + term
    o_ref[0] = x + gt_ref[0] * _dot(merged.astype(BF16), wo_ref[...])


def _merge(x, mod3, row_of, g, ys, wg, wb, wo, tm):
    b, l, d = x.shape
    tm = min(tm, l)
    c = ys[0].shape[2]
    tok = lambda n: pl.BlockSpec((1, tm, n), lambda bi, i: (bi, i, 0))
    modv = lambda j: pl.BlockSpec((1, 1, d), lambda bi, i: (row_of(bi), 0, j))
    return pl.pallas_call(
        _merge_kernel,
        grid=(b, l // tm),
        in_specs=[tok(d), modv(0), modv(1), modv(2), _const_spec((1, d)), tok(c), tok(c), tok(c), tok(c),
                  _const_spec(wg.shape), _const_spec(wb.shape), _const_spec(wo.shape)],
        out_specs=tok(d),
        out_shape=jax.ShapeDtypeStruct((b, l, d), F32),
        compiler_params=_params(("parallel", "parallel")),
        name="merge",
    )(x, mod3, mod3, mod3, g.reshape(1, d), *ys, wg, wb, wo)


def _ffn_kernel(x_ref, sh_ref, sc_ref, gt_ref, g_ref, w1_ref, w3_ref, w2_ref, fg_ref, o_ref, *, final_norm):
    x = x_ref[0]
    h = _norm_mod(x, g_ref[...], sh_ref[0], sc_ref[0]).astype(BF16)
    a = _dot(h, w1_ref[...])
    u = (a * jax.nn.sigmoid(a) * _dot(h, w3_ref[...])).astype(BF16)
    y = x + gt_ref[0] * _dot(u, w2_ref[...])
    if final_norm:
        y = _rms(y, fg_ref[...])
    o_ref[0] = y


def _ffn(x, mod3, row_of, g, w1, w3, w2, final_g, final_norm, tm):
    b, l, d = x.shape
    tm = min(tm, l)
    tok = pl.BlockSpec((1, tm, d), lambda bi, i: (bi, i, 0))
    modv = lambda j: pl.BlockSpec((1, 1, d), lambda bi, i: (row_of(bi), 0, j))
    return pl.pallas_call(
        functools.partial(_ffn_kernel, final_norm=final_norm),
        grid=(b, l // tm),
        in_specs=[tok, modv(3), modv(4), modv(5), _const_spec((1, d)),
                  _const_spec(w1.shape), _const_spec(w3.shape), _const_spec(w2.shape), _const_spec((1, d))],
        out_specs=tok,
        out_shape=jax.ShapeDtypeStruct((b, l, d), F32),
        compiler_params=_params(("parallel", "parallel")),
        name="ffn",
    )(x, mod3, mod3, mod3, g.reshape(1, d), w1, w3, w2, final_g.reshape(1, d))


def _rotate_half_cols(w):
    q = QK_ROPE // 4
    return jnp.concatenate([-w[:, q:2 * q], w[:, :q], -w[:, 3 * q:], w[:, 2 * q:3 * q]], axis=1)


def _block_diag(w):
    h, hd, _ = w.shape
    eye = jnp.eye(h, dtype=w.dtype)
    return (eye[:, None, :, None] * w[:, :, None, :]).reshape(h * hd, h * hd)


def _layer_weights(w_in_l, lru_w_a, lru_w_x, lru_b_a, lru_b_x, lru_lam, w_q_up, w_kv_up, cmlp_w_s,
                   w_branch, w_out, w_ff1, w_ff3, w_ff2):
    d = w_in_l.shape[0]
    c = d // 2
    o = {}
    i_lru_x, i_kv, i_kr, i_lru_g, i_q = 0, c, c + KV_LORA, c + KV_LORA + QK_ROPE, 2 * c + KV_LORA + QK_ROPE
    i_ab = i_q + Q_LORA
    i_ac, i_ax, i_cu, i_cv, i_gate = i_ab + c, i_ab + 2 * c, i_ab + 3 * c, i_ab + 4 * c, i_ab + 5 * c
    col = lambda s, n: w_in_l[:, s:s + n]
    k_rope = col(i_kr, QK_ROPE)
    mla_pad = jnp.zeros((d, HEAD_PAD - 2 * QK_ROPE), F32)
    o["w_lru"] = jnp.concatenate([col(i_lru_x, c), col(i_lru_g, c)], axis=1).astype(BF16)
    o["w_mla"] = jnp.concatenate([col(i_kv, KV_LORA), col(i_q, Q_LORA), k_rope, _rotate_half_cols(k_rope),
                                  mla_pad], axis=1).astype(BF16)
    o["w_conv"] = w_in_l[:, i_ab:i_cu].astype(BF16)
    o["w_cmlp"] = w_in_l[:, i_cu:i_gate].astype(BF16)
    o["w_gate"] = w_in_l[:, i_gate:].astype(BF16)
    o["lru_wg"] = jnp.stack([jnp.concatenate([_block_diag(lru_w_a[dd]), _block_diag(lru_w_x[dd])], axis=1)
                             for dd in range(2)]).astype(BF16)
    o["lru_bg"] = jnp.concatenate([lru_b_a, lru_b_x], axis=1).reshape(2, 1, 2 * c)
    o["lru_lam"] = lru_lam.reshape(2, 1, c)
    hq = QK_NOPE + QK_ROPE
    wq = w_q_up.reshape(Q_LORA, MLA_HEADS, hq)
    zq = jnp.zeros((Q_LORA, MLA_HEADS, HEAD_PAD - hq), F32)
    o["wq"] = jnp.concatenate([wq, zq], axis=2).reshape(Q_LORA, MLA_HEADS * HEAD_PAD).astype(BF16)
    rot = jnp.stack([_rotate_half_cols(wq[:, h, QK_NOPE:]) for h in range(MLA_HEADS)], axis=1)
    o["wqr"] = jnp.concatenate([jnp.zeros((Q_LORA, MLA_HEADS, QK_NOPE), F32), rot, zq],
                               axis=2).reshape(Q_LORA, MLA_HEADS * HEAD_PAD).astype(BF16)
    wkv = w_kv_up.reshape(KV_LORA, MLA_HEADS, 2 * QK_NOPE)
    o["wk"] = jnp.concatenate([wkv[:, :, :QK_NOPE], jnp.zeros((KV_LORA, MLA_HEADS, HEAD_PAD - QK_NOPE), F32)],
                              axis=2).reshape(KV_LORA, MLA_HEADS * HEAD_PAD).astype(BF16)
    o["wv"] = wkv[:, :, QK_NOPE:].reshape(KV_LORA, MLA_HEADS * QK_NOPE).astype(BF16)
    o["w_s"] = cmlp_w_s.astype(BF16)
    o["w_branch"] = w_branch.astype(BF16)
    o["w_out"] = w_out.astype(BF16)
    o["w_ff1"] = w_ff1.astype(BF16)
    o["w_ff3"] = w_ff3.astype(BF16)
    o["w_ff2"] = w_ff2.astype(BF16)
    return o


def _rope_place():
    eye = jnp.eye(QK_ROPE, dtype=F32)
    blk = jnp.zeros((HEAD_PAD, HEAD_PAD), F32)
    blk = blk.at[:QK_ROPE, QK_NOPE:QK_NOPE + QK_ROPE].set(eye)
    blk = blk.at[QK_ROPE:2 * QK_ROPE, QK_NOPE:QK_NOPE + QK_ROPE].set(eye)
    return jnp.concatenate([blk] * MLA_HEADS, axis=1).astype(BF16)


def _rope_tables(n_lat):
    t = jnp.arange(n_lat)
    n_freq = QK_ROPE // 4
    inv = ROPE_THETA ** (-jnp.arange(n_freq, dtype=F32) / n_freq)
    ang_r = (t // GRID_W).astype(F32)[:, None] * inv
    ang_c = (t % GRID_W).astype(F32)[:, None] * inv
    cos = jnp.concatenate([jnp.cos(ang_r)] * 2 + [jnp.cos(ang_c)] * 2, axis=1)
    sin = jnp.concatenate([jnp.sin(ang_r)] * 2 + [jnp.sin(ang_c)] * 2, axis=1)
    ones = jnp.ones((n_lat, QK_NOPE), F32)
    z32 = jnp.zeros((n_lat, HEAD_PAD - QK_NOPE - QK_ROPE), F32)
    q_cos = jnp.concatenate([ones, cos, z32], axis=1)
    q_sin = jnp.concatenate([jnp.zeros((n_lat, QK_NOPE), F32), sin, z32], axis=1)
    k_cs = jnp.concatenate([cos, sin, jnp.zeros((n_lat, HEAD_PAD - 2 * QK_ROPE), F32)], axis=1)
    return q_cos, q_sin, k_cs


def _identity_tables(n):
    q_cos = jnp.concatenate([jnp.ones((n, QK_NOPE + QK_ROPE), F32),
                             jnp.zeros((n, HEAD_PAD - QK_NOPE - QK_ROPE), F32)], axis=1)
    q_sin = jnp.zeros((n, HEAD_PAD), F32)
    k_cs = jnp.concatenate([jnp.ones((n, QK_ROPE), F32), jnp.zeros((n, HEAD_PAD - QK_ROPE), F32)], axis=1)
    return q_cos, q_sin, k_cs


TM_PROJ = 256
TM_MIX = 512
TQ_ATTN = 256


def kernel(x, c, ctx, c_ctx, w_mod, b_mod, norm1_g, norm2_g, w_in, conv_a_w, lru_conv_w, lru_conv_b,
           lru_w_a, lru_b_a, lru_w_x, lru_b_x, lru_lam, cmlp_ln_g, cmlp_ln_b, cmlp_w_s, cmlp_b_s,
           mla_q_norm_g, mla_kv_norm_g, mla_w_q_up, mla_w_kv_up, w_branch, w_out, w_ff1, w_ff3, w_ff2,
           final_norm_g):
    bsz, n_lat, d = x.shape
    n_ctx = ctx.shape[1]
    depth = w_in.shape[0]
    cw = d // 2
    scale = (QK_NOPE + QK_ROPE) ** -0.5

    rows = -(-(bsz + 1) // SUBLANES) * SUBLANES
    s_rows = jnp.concatenate([c, c_ctx[None, :], jnp.zeros((rows - bsz - 1, d), F32)], axis=0)
    mod = _modulation(s_rows, w_mod, b_mod)
    lat_row = lambda bi: bi
    ctx_row = lambda bi: bsz

    place = _rope_place()
    lat_tabs = _rope_tables(n_lat)
    ctx_tabs = _identity_tables(n_ctx)
    zero_state = jnp.zeros((bsz, 1, cw), F32)

    xc = ctx
    for l in range(depth):
        last = l == depth - 1
        w = _layer_weights(w_in[l], lru_w_a[l], lru_w_x[l], lru_b_a[l], lru_b_x[l], lru_lam[l],
                           mla_w_q_up[l], mla_w_kv_up[l], cmlp_w_s[l], w_branch[l], w_out[l],
                           w_ff1[l], w_ff3[l], w_ff2[l])
        mod3 = mod[l].reshape(rows, 1, 6 * d)
        proj_w = [w["w_lru"], w["w_mla"], w["w_conv"], w["w_cmlp"]]

        zc_lru, zc_mla, zc_conv, zc_cmlp = _inproj(xc, mod3, ctx_row, norm1_g[l], proj_w, TM_PROJ)
        yc_b, hc_f, hc_b = _lru_mixer(zc_lru, zero_state, zero_state, lru_conv_w[l], lru_conv_b[l],
                                      w["lru_wg"], w["lru_bg"], w["lru_lam"], TM_MIX)
        qc, kc, vc = _mla_proj(zc_mla, mla_q_norm_g[l], mla_kv_norm_g[l], w["wq"], w["wqr"], w["wk"], w["wv"],
                               place, *ctx_tabs, TM_MIX)

        z_lru, z_mla, z_conv, z_cmlp = _inproj(x, mod3, lat_row, norm1_g[l], proj_w, TM_PROJ)
        y_b, _, _ = _lru_mixer(z_lru, hc_f, hc_b, lru_conv_w[l], lru_conv_b[l],
                               w["lru_wg"], w["lru_bg"], w["lru_lam"], TM_MIX)
        q, k, v = _mla_proj(z_mla, mla_q_norm_g[l], mla_kv_norm_g[l], w["wq"], w["wqr"], w["wk"], w["wv"],
                            place, *lat_tabs, TM_MIX)
        y_d = _attention(q, [(k, v), (kc, vc)], scale, TQ_ATTN)
        y_a = _conv_mixer(z_conv, conv_a_w[l], TM_MIX)
        y_c = _cmlp_mixer(z_cmlp, cmlp_ln_g[l], cmlp_ln_b[l], w["w_s"], cmlp_b_s[l], TM_MIX)
        x = _merge(x, mod3, lat_row, norm1_g[l], (y_a, y_b, y_c, y_d), w["w_gate"], w["w_branch"], w["w_out"],
                   TM_PROJ)
        x = _ffn(x, mod3, lat_row, norm2_g[l], w["w_ff1"], w["w_ff3"], w["w_ff2"], final_norm_g, last, TM_PROJ)

        if not last:
            yc_d = _attention(qc, [(kc, vc)], scale, TQ_ATTN)
            yc_a = _conv_mixer(zc_conv, conv_a_w[l], TM_MIX)
            yc_c = _cmlp_mixer(zc_cmlp, cmlp_ln_g[l], cmlp_ln_b[l], w["w_s"], cmlp_b_s[l], TM_MIX)
            xc = _merge(xc, mod3, ctx_row, norm1_g[l], (yc_a, yc_b, yc_c, yc_d), w["w_gate"], w["w_branch"],
                        w["w_out"], TM_PROJ)
            xc = _ffn(xc, mod3, ctx_row, norm2_g[l], w["w_ff1"], w["w_ff3"], w["w_ff2"], final_norm_g, False,
                      TM_PROJ)
    return x
```

```python
import functools

import jax
import jax.numpy as jnp
from jax import lax
from jax.experimental import pallas as pl
from jax.experimental.pallas import tpu as pltpu

F32 = jnp.float32
BF16 = jnp.bfloat16

EPS = 1e-6
GRID_W = 64
N_BRANCH = 4
LRU_HEADS = 8
LRU_C = 8.0
CMLP_GROUPS = 4
CHUNK = 128
MLA_HEADS = 8
QK_NOPE = 64
QK_ROPE = 32
Q_LORA = 384
KV_LORA = 256
ROPE_THETA = 10000.0
HEAD_PAD = 128
ONES_ROWS = 16
VT_ROWS = QK_NOPE + ONES_ROWS
LOG2E = 1.4426950408889634
SUBLANES = 8
VMEM_LIMIT = 56 * 1024 * 1024


def _params(sem, vmem=VMEM_LIMIT, flags=None):
    return pltpu.CompilerParams(dimension_semantics=sem, vmem_limit_bytes=vmem, flags=flags)


def _const_spec(shape):
    zeros = (0,) * len(shape)
    return pl.BlockSpec(shape, lambda *_: zeros)


def _gelu(x):
    return jax.nn.gelu(x)


def _norm_mod(x, g, shift, scale):
    y = x * lax.rsqrt(jnp.mean(x * x, axis=-1, keepdims=True) + EPS) * g
    return y * (1.0 + scale) + shift


def _dot(a, b):
    return jnp.dot(a, b, preferred_element_type=F32)


def _mod_kernel(s_ref, w_ref, b_ref, o_ref):
    s = s_ref[...]
    s = s * jax.nn.sigmoid(s)
    o_ref[0] = _dot(s.astype(BF16), w_ref[0].astype(BF16)) + b_ref[0]


def _modulation(s_rows, w_mod, b_mod):
    n_layer, d, d6 = w_mod.shape
    r = s_rows.shape[0]
    return pl.pallas_call(
        _mod_kernel,
        grid=(n_layer, d6 // d),
        in_specs=[
            pl.BlockSpec((r, d), lambda l, j: (0, 0)),
            pl.BlockSpec((1, d, d), lambda l, j: (l, 0, j)),
            pl.BlockSpec((1, 1, d), lambda l, j: (l, 0, j)),
        ],
        out_specs=pl.BlockSpec((1, r, d), lambda l, j: (l, 0, j)),
        out_shape=jax.ShapeDtypeStruct((n_layer, r, d6), F32),
        compiler_params=_params(("parallel", "parallel")),
        name="modulation",
    )(s_rows, w_mod, b_mod.reshape(n_layer, 1, d6))


def _inproj_kernel(x_ref, sh_ref, sc_ref, g_ref, *refs):
    n = len(refs) // 2
    h = _norm_mod(x_ref[0], g_ref[...], sh_ref[0], sc_ref[0]).astype(BF16)
    for w_ref, o_ref in zip(refs[:n], refs[n:]):
        o_ref[0] = _dot(h, w_ref[...]).astype(o_ref.dtype)


def _inproj(x, mod3, row_of, g, weights, tm):
    b, l, d = x.shape
    tm = min(tm, l)
    in_specs = [
        pl.BlockSpec((1, tm, d), lambda bi, i: (bi, i, 0)),
        pl.BlockSpec((1, 1, d), lambda bi, i: (row_of(bi), 0, 0)),
        pl.BlockSpec((1, 1, d), lambda bi, i: (row_of(bi), 0, 1)),
        _const_spec((1, d)),
    ] + [_const_spec(w.shape) for w in weights]
    out_specs = [pl.BlockSpec((1, tm, w.shape[1]), lambda bi, i: (bi, i, 0)) for w in weights]
    out_shape = [jax.ShapeDtypeStruct((b, l, w.shape[1]), F32) for w in weights]
    return pl.pallas_call(
        _inproj_kernel,
        grid=(b, l // tm),
        in_specs=in_specs,
        out_specs=out_specs,
        out_shape=out_shape,
        compiler_params=_params(("parallel", "parallel")),
        name="inproj",
    )(x, mod3, mod3, g.reshape(1, d), *weights)


def _halo_specs(tm, l, c, lane_blk):
    nb = l // SUBLANES
    per = tm // SUBLANES
    prev = pl.BlockSpec((1, SUBLANES, c), lambda bi, i: (bi, jnp.maximum(i * per - 1, 0), lane_blk))
    nxt = pl.BlockSpec((1, SUBLANES, c), lambda bi, i: (bi, jnp.minimum((i + 1) * per, nb - 1), lane_blk))
    return prev, nxt


def _shift_down(p, first_row, k):
    tm = p.shape[0]
    row = lax.broadcasted_iota(jnp.int32, p.shape, 0)
    out = pltpu.roll(p, k, 0)
    for j in range(k):
        out = jnp.where(row == j, first_row[j:j + 1, :], out)
    del tm
    return out


def _shift_up(p, last_row):
    tm = p.shape[0]
    row = lax.broadcasted_iota(jnp.int32, p.shape, 0)
    return jnp.where(row == tm - 1, last_row, pltpu.roll(p, tm - 1, 0))


def _conv_kernel(ab_ref, ac_ref, ax_ref, acp_ref, axp_ref, acn_ref, axn_ref, w_ref, o_ref, *, n_tiles):
    i = pl.program_id(1)
    p = ac_ref[0] * ax_ref[0]
    p_prev = jnp.where(i > 0, acp_ref[0, SUBLANES - 1:, :] * axp_ref[0, SUBLANES - 1:, :], 0.0)
    p_next = jnp.where(i < n_tiles - 1, acn_ref[0, 0:1, :] * axn_ref[0, 0:1, :], 0.0)
    w = w_ref[...]
    conv = w[0:1] * _shift_down(p, p_prev, 1) + w[1:2] * p + w[2:3] * _shift_up(p, p_next)
    o_ref[0] = (ab_ref[0] * conv).astype(o_ref.dtype)


def _conv_mixer(z_conv, w_conv, tm):
    b, l, c3 = z_conv.shape
    c = c3 // 3
    tm = min(tm, l)
    main = lambda blk: pl.BlockSpec((1, tm, c), lambda bi, i: (bi, i, blk))
    acp, acn = _halo_specs(tm, l, c, 1)
    axp, axn = _halo_specs(tm, l, c, 2)
    return pl.pallas_call(
        functools.partial(_conv_kernel, n_tiles=l // tm),
        grid=(b, l // tm),
        in_specs=[main(0), main(1), main(2), acp, axp, acn, axn, _const_spec(w_conv.shape)],
        out_specs=pl.BlockSpec((1, tm, c), lambda bi, i: (bi, i, 0)),
        out_shape=jax.ShapeDtypeStruct((b, l, c), BF16),
        compiler_params=_params(("parallel", "parallel")),
        name="conv_mixer",
    )(z_conv, z_conv, z_conv, z_conv, z_conv, z_conv, z_conv, w_conv)


def _lru_coeffs(x_ref, xp_ref, xn_ref, cw_ref, cb_ref, wg_ref, bg_ref, lam_ref, i, n_tiles):
    x = x_ref[0]
    c = x.shape[1]
    prev = jnp.where(i > 0, xp_ref[0, SUBLANES - 2:, :], 0.0)
    nxt = jnp.where(i < n_tiles - 1, xn_ref[0, 0:1, :], 0.0)
    cw = cw_ref[...]
    x_m1 = _shift_down(x, prev[1:2], 1)
    x_m2 = _shift_down(x, prev, 2)
    xl = cw[0:1] * x_m2 + cw[1:2] * x_m1 + cw[2:3] * x + cw[3:4] * _shift_up(x, nxt) + cb_ref[...]
    gates = _dot(xl.astype(BF16), wg_ref[0]) + bg_ref[0]
    r = jax.nn.sigmoid(gates[:, :c])
    gi = jax.nn.sigmoid(gates[:, c:])
    lam = lam_ref[0]
    softplus_neg = jnp.maximum(-lam, 0.0) + jnp.log1p(jnp.exp(-jnp.abs(lam)))
    log_a = (-LRU_C * softplus_neg) * r
    a = jnp.exp(log_a)
    bb = jnp.sqrt(1.0 - a * a) * (gi * xl)
    return a, bb


def _scan_tile(a, bb, a_scr, b_scr, h_scr, carry_ref, reverse):
    tm = a.shape[0]
    r8 = lax.broadcasted_iota(jnp.int32, a.shape, 0) & (SUBLANES - 1)
    acc_a, acc_b = a, bb
    for k in (1, 2, 4):
        if reverse:
            keep = r8 < SUBLANES - k
            shift = tm - k
        else:
            keep = r8 >= k
            shift = k
        a_sh = jnp.where(keep, pltpu.roll(acc_a, shift, 0), 1.0)
        b_sh = jnp.where(keep, pltpu.roll(acc_b, shift, 0), 0.0)
        acc_b = acc_a * b_sh + acc_b
        acc_a = acc_a * a_sh
    a_scr[...] = acc_a
    b_scr[...] = acc_b
    groups = tm // SUBLANES

    def body(gi, h_in):
        g = groups - 1 - gi if reverse else gi
        rows = pl.ds(pl.multiple_of(g * SUBLANES, SUBLANES), SUBLANES)
        h8 = b_scr[rows, :] + a_scr[rows, :] * h_in
        h_scr[rows, :] = h8
        return h8[0:1, :] if reverse else h8[SUBLANES - 1:, :]

    carry_ref[...] = lax.fori_loop(0, groups, body, carry_ref[...], unroll=4)


def _lru_fwd_kernel(x_ref, xp_ref, xn_ref, h0_ref, cw_ref, cb_ref, wg_ref, bg_ref, lam_ref,
                    h_ref, hlast_ref, a_scr, b_scr, carry, *, n_tiles):
    i = pl.program_id(1)

    @pl.when(i == 0)
    def _():
        carry[...] = h0_ref[0]

    a, bb = _lru_coeffs(x_ref, xp_ref, xn_ref, cw_ref, cb_ref, wg_ref, bg_ref, lam_ref, i, n_tiles)
    _scan_tile(a, bb, a_scr, b_scr, h_ref.at[0], carry, reverse=False)
    hlast_ref[0] = carry[...]


def _lru_bwd_kernel(x_ref, xp_ref, xn_ref, h0_ref, hf_ref, g_ref, cw_ref, cb_ref, wg_ref, bg_ref, lam_ref,
                    y_ref, hfirst_ref, a_scr, b_scr, h_scr, carry, *, n_tiles):
    i = pl.program_id(1)

    @pl.when(i == 0)
    def _():
        carry[...] = h0_ref[0]

    tile = n_tiles - 1 - i
    a, bb = _lru_coeffs(x_ref, xp_ref, xn_ref, cw_ref, cb_ref, wg_ref, bg_ref, lam_ref, tile, n_tiles)
    _scan_tile(a, bb, a_scr, b_scr, h_scr, carry, reverse=True)
    hfirst_ref[0] = carry[...]
    y_ref[0] = (_gelu(g_ref[0]) * (hf_ref[0] + h_scr[...])).astype(y_ref.dtype)


def _lru_mixer(z_lru, h0_f, h0_b, conv_w, conv_b, wg, bg, lam, tm):
    b, l, c2 = z_lru.shape
    c = c2 // 2
    tm = min(tm, l)
    nt = l // tm
    per = tm // SUBLANES
    nb = l // SUBLANES
    state = jax.ShapeDtypeStruct((b, 1, c), F32)
    state_spec = pl.BlockSpec((1, 1, c), lambda bi, i: (bi, 0, 0))
    cp = _params(("parallel", "arbitrary"))
    par = lambda d: [_const_spec(conv_w.shape), _const_spec((1, c)),
                     pl.BlockSpec((1, c, c2), lambda bi, i: (d, 0, 0)),
                     pl.BlockSpec((1, 1, c2), lambda bi, i: (d, 0, 0)),
                     pl.BlockSpec((1, 1, c), lambda bi, i: (d, 0, 0))]
    args = (conv_w, conv_b.reshape(1, c), wg, bg, lam)

    prev_f, next_f = _halo_specs(tm, l, c, 0)
    hf, hf_last = pl.pallas_call(
        functools.partial(_lru_fwd_kernel, n_tiles=nt),
        grid=(b, nt),
        in_specs=[pl.BlockSpec((1, tm, c), lambda bi, i: (bi, i, 0)), prev_f, next_f, state_spec] + par(0),
        out_specs=[pl.BlockSpec((1, tm, c), lambda bi, i: (bi, i, 0)), state_spec],
        out_shape=[jax.ShapeDtypeStruct((b, l, c), F32), state],
        scratch_shapes=[pltpu.VMEM((tm, c), F32), pltpu.VMEM((tm, c), F32), pltpu.VMEM((1, c), F32)],
        compiler_params=cp,
        name="lru_fwd",
    )(z_lru, z_lru, z_lru, h0_f, *args)

    rev = lambda i: nt - 1 - i
    prev_b = pl.BlockSpec((1, SUBLANES, c), lambda bi, i: (bi, jnp.maximum(rev(i) * per - 1, 0), 0))
    next_b = pl.BlockSpec((1, SUBLANES, c), lambda bi, i: (bi, jnp.minimum((rev(i) + 1) * per, nb - 1), 0))
    y, hb_first = pl.pallas_call(
        functools.partial(_lru_bwd_kernel, n_tiles=nt),
        grid=(b, nt),
        in_specs=[pl.BlockSpec((1, tm, c), lambda bi, i: (bi, rev(i), 0)), prev_b, next_b, state_spec,
                  pl.BlockSpec((1, tm, c), lambda bi, i: (bi, rev(i), 0)),
                  pl.BlockSpec((1, tm, c), lambda bi, i: (bi, rev(i), 1))] + par(1),
        out_specs=[pl.BlockSpec((1, tm, c), lambda bi, i: (bi, rev(i), 0)), state_spec],
        out_shape=[jax.ShapeDtypeStruct((b, l, c), BF16), state],
        scratch_shapes=[pltpu.VMEM((tm, c), F32), pltpu.VMEM((tm, c), F32), pltpu.VMEM((tm, c), F32),
                        pltpu.VMEM((1, c), F32)],
        compiler_params=cp,
        name="lru_bwd",
    )(z_lru, z_lru, z_lru, h0_b, hf, z_lru, *args)
    return y, hf_last, hb_first


def _cmlp_kernel(u_ref, v_ref, g_ref, b_ref, ws_ref, bs_ref, o_ref):
    tm, c = u_ref.shape[1], u_ref.shape[2]
    gd = c // CMLP_GROUPS
    v = _gelu(v_ref[0])
    mu = jnp.mean(v, axis=-1, keepdims=True)
    var = jnp.mean(jnp.square(v - mu), axis=-1, keepdims=True)
    v = ((v - mu) * lax.rsqrt(var + EPS) * g_ref[...] + b_ref[...]).astype(BF16)
    bs = bs_ref[...]
    for ck in range(tm // CHUNK):
        rows = slice(ck * CHUNK, (ck + 1) * CHUNK)
        for g in range(CMLP_GROUPS):
            cols = slice(g * gd, (g + 1) * gd)
            mixed = _dot(ws_ref[g], v[rows, cols]) + bs[:, g:g + 1]
            o_ref[0, rows, cols] = (_gelu(u_ref[0, rows, cols]) * mixed).astype(o_ref.dtype)


def _cmlp_mixer(z_cmlp, ln_g, ln_b, w_s, b_s, tm):
    b, l, c2 = z_cmlp.shape
    c = c2 // 2
    tm = min(tm, l)
    return pl.pallas_call(
        _cmlp_kernel,
        grid=(b, l // tm),
        in_specs=[pl.BlockSpec((1, tm, c), lambda bi, i: (bi, i, 0)),
                  pl.BlockSpec((1, tm, c), lambda bi, i: (bi, i, 1)),
                  _const_spec((1, c)), _const_spec((1, c)), _const_spec(w_s.shape),
                  _const_spec((CHUNK, CMLP_GROUPS))],
        out_specs=pl.BlockSpec((1, tm, c), lambda bi, i: (bi, i, 0)),
        out_shape=jax.ShapeDtypeStruct((b, l, c), BF16),
        compiler_params=_params(("parallel", "parallel")),
        name="chunk_mlp",
    )(z_cmlp, z_cmlp, ln_g.reshape(1, c), ln_b.reshape(1, c), w_s, b_s.T)


def _rms(x, g):
    return x * lax.rsqrt(jnp.mean(x * x, axis=-1, keepdims=True) + EPS) * g


def _mla_proj_kernel(z_ref, qg_ref, kvg_ref, wq_ref, wqr_ref, wk_ref, wvt_ref, vone_ref, place_ref,
                     qc_ref, qs_ref, kcs_ref, q_ref, k_ref, vt_ref, *, q_scale):
    z = z_ref[0]
    kvn = _rms(z[:, :KV_LORA], kvg_ref[...]).astype(BF16)
    qn = _rms(z[:, KV_LORA:KV_LORA + Q_LORA], qg_ref[...]).astype(BF16)
    kr = (z[:, KV_LORA + Q_LORA:] * kcs_ref[...]).astype(BF16)
    k = _dot(kvn, wk_ref[...]) + _dot(kr, place_ref[...])
    k_ref[0] = k.astype(k_ref.dtype)
    v_t = lax.dot_general(wvt_ref[...], kvn, (((1,), (1,)), ((), ())), preferred_element_type=F32)
    vt_ref[0, 0] = (v_t + vone_ref[...]).astype(vt_ref.dtype)
    heads = q_ref.shape[2] // HEAD_PAD
    qc = jnp.concatenate([qc_ref[...]] * heads, axis=1)
    qs = jnp.concatenate([qs_ref[...]] * heads, axis=1)
    q = _dot(qn, wq_ref[...]) * qc + _dot(qn, wqr_ref[...]) * qs
    q_ref[0] = (q * q_scale).astype(q_ref.dtype)


def _mla_proj(z_mla, q_g, kv_g, wq, wqr, wk, wvt, vone, place, q_cos, q_sin, k_cs, q_scale, tm):
    b, l, zc = z_mla.shape
    tm = min(tm, l)
    hp = wq.shape[1]
    vc = wvt.shape[0]
    tab = pl.BlockSpec((tm, HEAD_PAD), lambda bi, i: (i, 0))
    tok = lambda n: pl.BlockSpec((1, tm, n), lambda bi, i: (bi, i, 0))
    return pl.pallas_call(
        functools.partial(_mla_proj_kernel, q_scale=q_scale),
        grid=(b, l // tm),
        in_specs=[tok(zc), _const_spec((1, Q_LORA)), _const_spec((1, KV_LORA)),
                  _const_spec(wq.shape), _const_spec(wqr.shape), _const_spec(wk.shape),
                  _const_spec(wvt.shape), _const_spec(vone.shape), _const_spec(place.shape), tab, tab, tab],
        out_specs=[tok(hp), tok(hp), pl.BlockSpec((1, 1, vc, tm), lambda bi, i: (bi, i, 0, 0))],
        out_shape=[jax.ShapeDtypeStruct((b, l, hp), BF16), jax.ShapeDtypeStruct((b, l, hp), BF16),
                   jax.ShapeDtypeStruct((b, l // tm, vc, tm), BF16)],
        compiler_params=_params(("parallel", "parallel")),
        name="mla_proj",
    )(z_mla, q_g.reshape(1, Q_LORA), kv_g.reshape(1, KV_LORA), wq, wqr, wk, wvt, vone, place, q_cos, q_sin, k_cs)


def _attn_kernel(q_ref, *refs, n_kv):
    kv_refs, o_ref = refs[:2 * n_kv], refs[2 * n_kv]
    s_buf, p_buf, mx_buf, alpha_buf, m_ref, acc_ref = refs[2 * n_kv + 1:]
    nt = (((1,), (1,)), ((), ()))
    hr = acc_ref.shape[0] // 2
    hv = o_ref.shape[2] // 2
    chunks = []
    for j in range(n_kv):
        k_ref, vt_ref = kv_refs[2 * j], kv_refs[2 * j + 1]
        chunks += [(k_ref, vt_ref, ci, vt_ref.shape[3]) for ci in range(vt_ref.shape[1])]
    n = len(chunks)

    def stage_qk(t):
        k_ref, _, ci, ck = chunks[t]
        for hh in range(2):
            lanes = slice(hh * HEAD_PAD, (hh + 1) * HEAD_PAD)
            s = lax.dot_general(k_ref[0, ci * ck:(ci + 1) * ck, lanes], q_ref[0, :, lanes], nt,
                                preferred_element_type=F32)
            s_buf[t % 2, hh, :ck, :] = s
            mx_buf[t % 2, hh] = jnp.max(s, axis=0, keepdims=True)

    def stage_softmax(t):
        ck = chunks[t][3]
        for hh in range(2):
            m_new = mx_buf[t % 2, hh]
            if t > 0:
                m_old = m_ref[hh]
                m_new = jnp.maximum(m_old, m_new)
                alpha_buf[t % 2, hh] = jnp.exp2(m_old - m_new)
            m_ref[hh] = m_new
            p_buf[t % 2, hh, :ck, :] = jnp.exp2(s_buf[t % 2, hh, :ck, :] - m_new).astype(BF16)

    def stage_pv(t):
        _, vt_ref, ci, ck = chunks[t]
        for hh in range(2):
            rows = slice(hh * hr, (hh + 1) * hr)
            pv = _dot(vt_ref[0, ci, rows, :], p_buf[t % 2, hh, :ck, :])
            if t == 0:
                acc_ref[rows, :] = pv
            else:
                acc_ref[rows, :] = alpha_buf[t % 2, hh] * acc_ref[rows, :] + pv

    for t in range(n + 2):
        if t < n:
            stage_qk(t)
        if 0 <= t - 1 < n:
            stage_softmax(t - 1)
        if 0 <= t - 2 < n:
            stage_pv(t - 2)
    outs = [acc_ref[hh * hr:hh * hr + hv, :] * (1.0 / acc_ref[hh * hr + hv:hh * hr + hv + 1, :])
            for hh in range(2)]
    o_ref[0] = jnp.concatenate(outs, axis=0).T.astype(o_ref.dtype)


def _attention(q, kvs, tq):
    b, l, hp = q.shape
    heads = hp // HEAD_PAD
    tq = min(tq, l)
    in_specs = [pl.BlockSpec((1, tq, 2 * HEAD_PAD), lambda bi, h, i: (bi, i, h))]
    args = [q]
    ck_max = 0
    for k, vt in kvs:
        _, n_chunks, _, ck = vt.shape
        ck_max = max(ck_max, ck)
        in_specs.append(pl.BlockSpec((1, k.shape[1], 2 * HEAD_PAD), lambda bi, h, i: (bi, 0, h)))
        in_specs.append(pl.BlockSpec((1, n_chunks, 2 * VT_ROWS, ck), lambda bi, h, i: (bi, 0, h, 0)))
        args += [k, vt]
    stat = pltpu.VMEM((2, 2, 1, tq), F32)
    return pl.pallas_call(
        functools.partial(_attn_kernel, n_kv=len(kvs)),
        grid=(b, heads // 2, l // tq),
        in_specs=in_specs,
        out_specs=pl.BlockSpec((1, tq, 2 * QK_NOPE), lambda bi, h, i: (bi, i, h)),
        out_shape=jax.ShapeDtypeStruct((b, l, heads * QK_NOPE), BF16),
        scratch_shapes=[pltpu.VMEM((2, 2, ck_max, tq), F32), pltpu.VMEM((2, 2, ck_max, tq), BF16),
                        stat, stat, pltpu.VMEM((2, 1, tq), F32), pltpu.VMEM((2 * VT_ROWS, tq), F32)],
        compiler_params=_params(("parallel", "parallel", "parallel")),
        name="attention",
    )(*args)


def _merge_kernel(x_ref, sh_ref, sc_ref, gt_ref, g_ref, ya_ref, yb_ref, yc_ref, yd_ref,
                  wg_ref, wb_ref, wo_ref, o_ref):
    x = x_ref[0]
    d = x.shape[1]
    h = _norm_mod(x, g_ref[...], sh_ref[0], sc_ref[0]).astype(BF16)
    merged = None
    for n, y_ref in enumerate((ya_ref, yb_ref, yc_ref, yd_ref)):
        gate = jax.nn.sigmoid(_dot(h, wg_ref[:, n * d:(n + 1) * d]))
        term = gate * _dot(y_ref[0], wb_ref[n])
        merged = term if merged is None else merged + term
    o_ref[0] = x + gt_ref[0] * _dot(merged.astype(BF16), wo_ref[...])


def _merge(x, mod3, row_of, g, ys, wg, wb, wo, tm):
    b, l, d = x.shape
    tm = min(tm, l)
    c = ys[0].shape[2]
    tok = lambda n: pl.BlockSpec((1, tm, n), lambda bi, i: (bi, i, 0))
    modv = lambda j: pl.BlockSpec((1, 1, d), lambda bi, i: (row_of(bi), 0, j))
    return pl.pallas_call(
        _merge_kernel,
        grid=(b, l // tm),
        in_specs=[tok(d), modv(0), modv(1), modv(2), _const_spec((1, d)), tok(c), tok(c), tok(c), tok(c),
                  _const_spec(wg.shape), _const_spec(wb.shape), _const_spec(wo.shape)],
        out_specs=tok(d),
        out_shape=jax.ShapeDtypeStruct((b, l, d), F32),
        compiler_params=_params(("parallel", "parallel")),
        name="merge",
    )(x, mod3, mod3, mod3, g.reshape(1, d), *ys, wg, wb, wo)


def _ffn_kernel(x_ref, sh_ref, sc_ref, gt_ref, g_ref, w1_ref, w3_ref, w2_ref, fg_ref, o_ref, *, final_norm):
    x = x_ref[0]
    h = _norm_mod(x, g_ref[...], sh_ref[0], sc_ref[0]).astype(BF16)
    a = _dot(h, w1_ref[...])
    u = (a * jax.nn.sigmoid(a) * _dot(h, w3_ref[...])).astype(BF16)
    y = x + gt_ref[0] * _dot(u, w2_ref[...])
    if final_norm:
        y = _rms(y, fg_ref[...])
    o_ref[0] = y


def _ffn(x, mod3, row_of, g, w1, w3, w2, final_g, final_norm, tm):
    b, l, d = x.shape
    tm = min(tm, l)
    tok = pl.BlockSpec((1, tm, d), lambda bi, i: (bi, i, 0))
    modv = lambda j: pl.BlockSpec((1, 1, d), lambda bi, i: (row_of(bi), 0, j))
    return pl.pallas_call(
        functools.partial(_ffn_kernel, final_norm=final_norm),
        grid=(b, l // tm),
        in_specs=[tok, modv(3), modv(4), modv(5), _const_spec((1, d)),
                  _const_spec(w1.shape), _const_spec(w3.shape), _const_spec(w2.shape), _const_spec((1, d))],
        out_specs=tok,
        out_shape=jax.ShapeDtypeStruct((b, l, d), F32),
        compiler_params=_params(("parallel", "parallel")),
        name="ffn",
    )(x, mod3, mod3, mod3, g.reshape(1, d), w1, w3, w2, final_g.reshape(1, d))


def _rotate_half_cols(w):
    q = QK_ROPE // 4
    return jnp.concatenate([-w[:, q:2 * q], w[:, :q], -w[:, 3 * q:], w[:, 2 * q:3 * q]], axis=1)


def _block_diag(w):
    h, hd, _ = w.shape
    eye = jnp.eye(h, dtype=w.dtype)
    return (eye[:, None, :, None] * w[:, :, None, :]).reshape(h * hd, h * hd)


def _layer_weights(w_in_l, lru_w_a, lru_w_x, lru_b_a, lru_b_x, lru_lam, w_q_up, w_kv_up, cmlp_w_s,
                   w_branch, w_out, w_ff1, w_ff3, w_ff2):
    d = w_in_l.shape[0]
    c = d // 2
    o = {}
    i_lru_x, i_kv, i_kr, i_lru_g, i_q = 0, c, c + KV_LORA, c + KV_LORA + QK_ROPE, 2 * c + KV_LORA + QK_ROPE
    i_ab = i_q + Q_LORA
    i_ac, i_ax, i_cu, i_cv, i_gate = i_ab + c, i_ab + 2 * c, i_ab + 3 * c, i_ab + 4 * c, i_ab + 5 * c
    col = lambda s, n: w_in_l[:, s:s + n]
    k_rope = col(i_kr, QK_ROPE)
    mla_pad = jnp.zeros((d, HEAD_PAD - 2 * QK_ROPE), F32)
    o["w_lru"] = jnp.concatenate([col(i_lru_x, c), col(i_lru_g, c)], axis=1).astype(BF16)
    o["w_mla"] = jnp.concatenate([col(i_kv, KV_LORA), col(i_q, Q_LORA), k_rope, _rotate_half_cols(k_rope),
                                  mla_pad], axis=1).astype(BF16)
    o["w_conv"] = w_in_l[:, i_ab:i_cu].astype(BF16)
    o["w_cmlp"] = w_in_l[:, i_cu:i_gate].astype(BF16)
    o["w_gate"] = w_in_l[:, i_gate:].astype(BF16)
    o["lru_wg"] = jnp.stack([jnp.concatenate([_block_diag(lru_w_a[dd]), _block_diag(lru_w_x[dd])], axis=1)
                             for dd in range(2)]).astype(BF16)
    o["lru_bg"] = jnp.concatenate([lru_b_a, lru_b_x], axis=1).reshape(2, 1, 2 * c)
    o["lru_lam"] = lru_lam.reshape(2, 1, c)
    hq = QK_NOPE + QK_ROPE
    wq = w_q_up.reshape(Q_LORA, MLA_HEADS, hq)
    zq = jnp.zeros((Q_LORA, MLA_HEADS, HEAD_PAD - hq), F32)
    o["wq"] = jnp.concatenate([wq, zq], axis=2).reshape(Q_LORA, MLA_HEADS * HEAD_PAD).astype(BF16)
    rot = jnp.stack([_rotate_half_cols(wq[:, h, QK_NOPE:]) for h in range(MLA_HEADS)], axis=1)
    o["wqr"] = jnp.concatenate([jnp.zeros((Q_LORA, MLA_HEADS, QK_NOPE), F32), rot, zq],
                               axis=2).reshape(Q_LORA, MLA_HEADS * HEAD_PAD).astype(BF16)
    wkv = w_kv_up.reshape(KV_LORA, MLA_HEADS, 2 * QK_NOPE)
    o["wk"] = jnp.concatenate([wkv[:, :, :QK_NOPE], jnp.zeros((KV_LORA, MLA_HEADS, HEAD_PAD - QK_NOPE), F32)],
                              axis=2).reshape(KV_LORA, MLA_HEADS * HEAD_PAD).astype(BF16)
    wv = jnp.concatenate([wkv[:, :, QK_NOPE:], jnp.zeros((KV_LORA, MLA_HEADS, ONES_ROWS), F32)], axis=2)
    o["wvt"] = wv.reshape(KV_LORA, MLA_HEADS * VT_ROWS).T.astype(BF16)
    o["w_s"] = cmlp_w_s.astype(BF16)
    o["w_branch"] = w_branch.astype(BF16)
    o["w_out"] = w_out.astype(BF16)
    o["w_ff1"] = w_ff1.astype(BF16)
    o["w_ff3"] = w_ff3.astype(BF16)
    o["w_ff2"] = w_ff2.astype(BF16)
    return o


def _rope_place():
    eye = jnp.eye(QK_ROPE, dtype=F32)
    blk = jnp.zeros((HEAD_PAD, HEAD_PAD), F32)
    blk = blk.at[:QK_ROPE, QK_NOPE:QK_NOPE + QK_ROPE].set(eye)
    blk = blk.at[QK_ROPE:2 * QK_ROPE, QK_NOPE:QK_NOPE + QK_ROPE].set(eye)
    return jnp.concatenate([blk] * MLA_HEADS, axis=1).astype(BF16)


def _rope_tables(n_lat):
    t = jnp.arange(n_lat)
    n_freq = QK_ROPE // 4
    inv = ROPE_THETA ** (-jnp.arange(n_freq, dtype=F32) / n_freq)
    ang_r = (t // GRID_W).astype(F32)[:, None] * inv
    ang_c = (t % GRID_W).astype(F32)[:, None] * inv
    cos = jnp.concatenate([jnp.cos(ang_r)] * 2 + [jnp.cos(ang_c)] * 2, axis=1)
    sin = jnp.concatenate([jnp.sin(ang_r)] * 2 + [jnp.sin(ang_c)] * 2, axis=1)
    ones = jnp.ones((n_lat, QK_NOPE), F32)
    z32 = jnp.zeros((n_lat, HEAD_PAD - QK_NOPE - QK_ROPE), F32)
    q_cos = jnp.concatenate([ones, cos, z32], axis=1)
    q_sin = jnp.concatenate([jnp.zeros((n_lat, QK_NOPE), F32), sin, z32], axis=1)
    k_cs = jnp.concatenate([cos, sin, jnp.zeros((n_lat, HEAD_PAD - 2 * QK_ROPE), F32)], axis=1)
    return q_cos, q_sin, k_cs


def _identity_tables(n):
    q_cos = jnp.concatenate([jnp.ones((n, QK_NOPE + QK_ROPE), F32),
                             jnp.zeros((n, HEAD_PAD - QK_NOPE - QK_ROPE), F32)], axis=1)
    q_sin = jnp.zeros((n, HEAD_PAD), F32)
    k_cs = jnp.concatenate([jnp.ones((n, QK_ROPE), F32), jnp.zeros((n, HEAD_PAD - QK_ROPE), F32)], axis=1)
    return q_cos, q_sin, k_cs


TM_PROJ = 256
TM_MIX = 512
TQ_ATTN = 512


def kernel(x, c, ctx, c_ctx, w_mod, b_mod, norm1_g, norm2_g, w_in, conv_a_w, lru_conv_w, lru_conv_b,
           lru_w_a, lru_b_a, lru_w_x, lru_b_x, lru_lam, cmlp_ln_g, cmlp_ln_b, cmlp_w_s, cmlp_b_s,
           mla_q_norm_g, mla_kv_norm_g, mla_w_q_up, mla_w_kv_up, w_branch, w_out, w_ff1, w_ff3, w_ff2,
           final_norm_g):
    bsz, n_lat, d = x.shape
    n_ctx = ctx.shape[1]
    depth = w_in.shape[0]
    cw = d // 2
    q_scale = (QK_NOPE + QK_ROPE) ** -0.5 * LOG2E
    row_in_head = jnp.arange(MLA_HEADS * VT_ROWS) % VT_ROWS
    vone = (row_in_head >= QK_NOPE).astype(F32)[:, None]

    rows = -(-(bsz + 1) // SUBLANES) * SUBLANES
    s_rows = jnp.concatenate([c, c_ctx[None, :], jnp.zeros((rows - bsz - 1, d), F32)], axis=0)
    mod = _modulation(s_rows, w_mod, b_mod)
    lat_row = lambda bi: bi
    ctx_row = lambda bi: bsz

    place = _rope_place()
    lat_tabs = _rope_tables(n_lat)
    ctx_tabs = _identity_tables(n_ctx)
    zero_state = jnp.zeros((bsz, 1, cw), F32)

    xc = ctx
    for l in range(depth):
        last = l == depth - 1
        w = _layer_weights(w_in[l], lru_w_a[l], lru_w_x[l], lru_b_a[l], lru_b_x[l], lru_lam[l],
                           mla_w_q_up[l], mla_w_kv_up[l], cmlp_w_s[l], w_branch[l], w_out[l],
                           w_ff1[l], w_ff3[l], w_ff2[l])
        mod3 = mod[l].reshape(rows, 1, 6 * d)
        proj_w = [w["w_lru"], w["w_mla"], w["w_conv"], w["w_cmlp"]]

        zc_lru, zc_mla, zc_conv, zc_cmlp = _inproj(xc, mod3, ctx_row, norm1_g[l], proj_w, TM_PROJ)
        yc_b, hc_f, hc_b = _lru_mixer(zc_lru, zero_state, zero_state, lru_conv_w[l], lru_conv_b[l],
                                      w["lru_wg"], w["lru_bg"], w["lru_lam"], TM_MIX)
        qc, kc, vc = _mla_proj(zc_mla, mla_q_norm_g[l], mla_kv_norm_g[l], w["wq"], w["wqr"], w["wk"], w["wvt"],
                               vone, place, *ctx_tabs, q_scale, TM_MIX)

        z_lru, z_mla, z_conv, z_cmlp = _inproj(x, mod3, lat_row, norm1_g[l], proj_w, TM_PROJ)
        y_b, _, _ = _lru_mixer(z_lru, hc_f, hc_b, lru_conv_w[l], lru_conv_b[l],
                               w["lru_wg"], w["lru_bg"], w["lru_lam"], TM_MIX)
        q, k, v = _mla_proj(z_mla, mla_q_norm_g[l], mla_kv_norm_g[l], w["wq"], w["wqr"], w["wk"], w["wvt"],
                            vone, place, *lat_tabs, q_scale, TM_MIX)
        y_d = _attention(q, [(k, v), (kc, vc)], TQ_ATTN)
        y_a = _conv_mixer(z_conv, conv_a_w[l], TM_MIX)
        y_c = _cmlp_mixer(z_cmlp, cmlp_ln_g[l], cmlp_ln_b[l], w["w_s"], cmlp_b_s[l], TM_MIX)
        x = _merge(x, mod3, lat_row, norm1_g[l], (y_a, y_b, y_c, y_d), w["w_gate"], w["w_branch"], w["w_out"],
                   TM_PROJ)
        x = _ffn(x, mod3, lat_row, norm2_g[l], w["w_ff1"], w["w_ff3"], w["w_ff2"], final_norm_g, last, TM_PROJ)

        if not last:
            yc_d = _attention(qc, [(kc, vc)], TQ_ATTN)
            yc_a = _conv_mixer(zc_conv, conv_a_w[l], TM_MIX)
            yc_c = _cmlp_mixer(zc_cmlp, cmlp_ln_g[l], cmlp_ln_b[l], w["w_s"], cmlp_b_s[l], TM_MIX)
            xc = _merge(xc, mod3, ctx_row, norm1_g[l], (yc_a, yc_b, yc_c, yc_d), w["w_gate"], w["w_branch"],
                        w["w_out"], TM_PROJ)
            xc = _ffn(xc, mod3, ctx_row, norm2_g[l], w["w_ff1"], w["w_ff3"], w["w_ff2"], final_norm_g, False,
                      TM_PROJ)
    return x
```

```python
import functools

import jax
import jax.numpy as jnp
from jax import lax
from jax.experimental import pallas as pl
from jax.experimental.pallas import tpu as pltpu

F32 = jnp.float32
BF16 = jnp.bfloat16

EPS = 1e-6
GRID_W = 64
N_BRANCH = 4
LRU_HEADS = 8
LRU_C = 8.0
CMLP_GROUPS = 4
CHUNK = 128
MLA_HEADS = 8
QK_NOPE = 64
QK_ROPE = 32
Q_LORA = 384
KV_LORA = 256
ROPE_THETA = 10000.0
HEAD_PAD = 128
ONES_ROWS = 16
VT_ROWS = QK_NOPE + ONES_ROWS
LOG2E = 1.4426950408889634
SUBLANES = 8
VMEM_LIMIT = 56 * 1024 * 1024


def _params(sem, vmem=VMEM_LIMIT, flags=None):
    return pltpu.CompilerParams(dimension_semantics=sem, vmem_limit_bytes=vmem, flags=flags)


def _const_spec(shape):
    zeros = (0,) * len(shape)
    return pl.BlockSpec(shape, lambda *_: zeros, pipeline_mode=pl.Buffered(1))


def _gelu(x):
    return jax.nn.gelu(x)


def _norm_mod(x, g, shift, scale):
    y = x * lax.rsqrt(jnp.mean(x * x, axis=-1, keepdims=True) + EPS) * g
    return y * (1.0 + scale) + shift


def _dot(a, b):
    return jnp.dot(a, b, preferred_element_type=F32)


def _mod_kernel(s_ref, w_ref, b_ref, o_ref):
    s = s_ref[...]
    s = s * jax.nn.sigmoid(s)
    o_ref[0] = _dot(s.astype(BF16), w_ref[0].astype(BF16)) + b_ref[0]


def _modulation(s_rows, w_mod, b_mod):
    n_layer, d, d6 = w_mod.shape
    r = s_rows.shape[0]
    return pl.pallas_call(
        _mod_kernel,
        grid=(n_layer, d6 // d),
        in_specs=[
            pl.BlockSpec((r, d), lambda l, j: (0, 0)),
            pl.BlockSpec((1, d, d), lambda l, j: (l, 0, j)),
            pl.BlockSpec((1, 1, d), lambda l, j: (l, 0, j)),
        ],
        out_specs=pl.BlockSpec((1, r, d), lambda l, j: (l, 0, j)),
        out_shape=jax.ShapeDtypeStruct((n_layer, r, d6), F32),
        compiler_params=_params(("parallel", "parallel")),
        name="modulation",
    )(s_rows, w_mod, b_mod.reshape(n_layer, 1, d6))


def _inproj_kernel(x_ref, sh_ref, sc_ref, g_ref, *refs):
    n = len(refs) // 2
    h = _norm_mod(x_ref[0], g_ref[...], sh_ref[0], sc_ref[0]).astype(BF16)
    for w_ref, o_ref in zip(refs[:n], refs[n:]):
        o_ref[0] = _dot(h, w_ref[...]).astype(o_ref.dtype)


def _inproj(x, mod3, row_of, g, weights, tm):
    b, l, d = x.shape
    tm = min(tm, l)
    in_specs = [
        pl.BlockSpec((1, tm, d), lambda bi, i: (bi, i, 0)),
        pl.BlockSpec((1, 1, d), lambda bi, i: (row_of(bi), 0, 0)),
        pl.BlockSpec((1, 1, d), lambda bi, i: (row_of(bi), 0, 1)),
        _const_spec((1, d)),
    ] + [_const_spec(w.shape) for w in weights]
    out_specs = [pl.BlockSpec((1, tm, w.shape[1]), lambda bi, i: (bi, i, 0)) for w in weights]
    out_shape = [jax.ShapeDtypeStruct((b, l, w.shape[1]), F32) for w in weights]
    return pl.pallas_call(
        _inproj_kernel,
        grid=(b, l // tm),
        in_specs=in_specs,
        out_specs=out_specs,
        out_shape=out_shape,
        compiler_params=_params(("parallel", "parallel")),
        name="inproj",
    )(x, mod3, mod3, g.reshape(1, d), *weights)


def _halo_specs(tm, l, c, lane_blk):
    nb = l // SUBLANES
    per = tm // SUBLANES
    prev = pl.BlockSpec((1, SUBLANES, c), lambda bi, i: (bi, jnp.maximum(i * per - 1, 0), lane_blk))
    nxt = pl.BlockSpec((1, SUBLANES, c), lambda bi, i: (bi, jnp.minimum((i + 1) * per, nb - 1), lane_blk))
    return prev, nxt


def _shift_down(p, first_row, k):
    row = lax.broadcasted_iota(jnp.int32, p.shape, 0)
    out = pltpu.roll(p, k, 0)
    for j in range(k):
        out = jnp.where(row == j, first_row[j:j + 1, :], out)
    return out


def _shift_up(p, last_row):
    tm = p.shape[0]
    row = lax.broadcasted_iota(jnp.int32, p.shape, 0)
    return jnp.where(row == tm - 1, last_row, pltpu.roll(p, tm - 1, 0))


def _conv_kernel(ab_ref, ac_ref, ax_ref, acp_ref, axp_ref, acn_ref, axn_ref, w_ref, o_ref, *, n_tiles):
    i = pl.program_id(1)
    p = ac_ref[0] * ax_ref[0]
    p_prev = jnp.where(i > 0, acp_ref[0, SUBLANES - 1:, :] * axp_ref[0, SUBLANES - 1:, :], 0.0)
    p_next = jnp.where(i < n_tiles - 1, acn_ref[0, 0:1, :] * axn_ref[0, 0:1, :], 0.0)
    w = w_ref[...]
    conv = w[0:1] * _shift_down(p, p_prev, 1) + w[1:2] * p + w[2:3] * _shift_up(p, p_next)
    o_ref[0] = (ab_ref[0] * conv).astype(o_ref.dtype)


def _conv_mixer(z_conv, w_conv, tm):
    b, l, c3 = z_conv.shape
    c = c3 // 3
    tm = min(tm, l)
    main = lambda blk: pl.BlockSpec((1, tm, c), lambda bi, i: (bi, i, blk))
    acp, acn = _halo_specs(tm, l, c, 1)
    axp, axn = _halo_specs(tm, l, c, 2)
    return pl.pallas_call(
        functools.partial(_conv_kernel, n_tiles=l // tm),
        grid=(b, l // tm),
        in_specs=[main(0), main(1), main(2), acp, axp, acn, axn, _const_spec(w_conv.shape)],
        out_specs=pl.BlockSpec((1, tm, c), lambda bi, i: (bi, i, 0)),
        out_shape=jax.ShapeDtypeStruct((b, l, c), BF16),
        compiler_params=_params(("parallel", "parallel")),
        name="conv_mixer",
    )(z_conv, z_conv, z_conv, z_conv, z_conv, z_conv, z_conv, w_conv)


def _lru_coeffs(x_ref, xp_ref, xn_ref, cw_ref, cb_ref, wg_ref, bg_ref, lam_ref, i, n_tiles):
    x = x_ref[0]
    c = x.shape[1]
    prev = jnp.where(i > 0, xp_ref[0, SUBLANES - 2:, :], 0.0)
    nxt = jnp.where(i < n_tiles - 1, xn_ref[0, 0:1, :], 0.0)
    cw = cw_ref[...]
    x_m1 = _shift_down(x, prev[1:2], 1)
    x_m2 = _shift_down(x, prev, 2)
    xl = cw[0:1] * x_m2 + cw[1:2] * x_m1 + cw[2:3] * x + cw[3:4] * _shift_up(x, nxt) + cb_ref[...]
    gates = _dot(xl.astype(BF16), wg_ref[0]) + bg_ref[0]
    r = jax.nn.sigmoid(gates[:, :c])
    gi = jax.nn.sigmoid(gates[:, c:])
    lam = lam_ref[0]
    softplus_neg = jnp.maximum(-lam, 0.0) + jnp.log1p(jnp.exp(-jnp.abs(lam)))
    log_a = (-LRU_C * softplus_neg) * r
    a = jnp.exp(log_a)
    bb = jnp.sqrt(1.0 - a * a) * (gi * xl)
    return a, bb


def _scan_tile(a, bb, a_scr, b_scr, h_scr, carry_ref, reverse):
    tm, c = a.shape
    groups = tm // SUBLANES
    acc_a = a.reshape(groups, SUBLANES, c)
    acc_b = bb.reshape(groups, SUBLANES, c)
    r8 = lax.broadcasted_iota(jnp.int32, acc_a.shape, 1)
    for k in (1, 2, 4):
        if reverse:
            keep = r8 < SUBLANES - k
            shift = SUBLANES - k
        else:
            keep = r8 >= k
            shift = k
        a_sh = jnp.where(keep, pltpu.roll(acc_a, shift, 1), 1.0)
        b_sh = jnp.where(keep, pltpu.roll(acc_b, shift, 1), 0.0)
        acc_b = acc_a * b_sh + acc_b
        acc_a = acc_a * a_sh
    a_scr[...] = acc_a.reshape(tm, c)
    b_scr[...] = acc_b.reshape(tm, c)

    def body(gi, h_in):
        g = groups - 1 - gi if reverse else gi
        rows = pl.ds(pl.multiple_of(g * SUBLANES, SUBLANES), SUBLANES)
        h8 = b_scr[rows, :] + a_scr[rows, :] * h_in
        h_scr[rows, :] = h8
        return h8[0:1, :] if reverse else h8[SUBLANES - 1:, :]

    carry_ref[...] = lax.fori_loop(0, groups, body, carry_ref[...], unroll=4)


def _lru_fwd_kernel(x_ref, xp_ref, xn_ref, h0_ref, cw_ref, cb_ref, wg_ref, bg_ref, lam_ref,
                    h_ref, hlast_ref, a_scr, b_scr, carry, *, n_tiles):
    i = pl.program_id(1)

    @pl.when(i == 0)
    def _():
        carry[...] = h0_ref[0]

    a, bb = _lru_coeffs(x_ref, xp_ref, xn_ref, cw_ref, cb_ref, wg_ref, bg_ref, lam_ref, i, n_tiles)
    _scan_tile(a, bb, a_scr, b_scr, h_ref.at[0], carry, reverse=False)
    hlast_ref[0] = carry[...]


def _lru_bwd_kernel(x_ref, xp_ref, xn_ref, h0_ref, hf_ref, g_ref, cw_ref, cb_ref, wg_ref, bg_ref, lam_ref,
                    y_ref, hfirst_ref, a_scr, b_scr, h_scr, carry, *, n_tiles):
    i = pl.program_id(1)

    @pl.when(i == 0)
    def _():
        carry[...] = h0_ref[0]

    tile = n_tiles - 1 - i
    a, bb = _lru_coeffs(x_ref, xp_ref, xn_ref, cw_ref, cb_ref, wg_ref, bg_ref, lam_ref, tile, n_tiles)
    _scan_tile(a, bb, a_scr, b_scr, h_scr, carry, reverse=True)
    hfirst_ref[0] = carry[...]
    y_ref[0] = (_gelu(g_ref[0]) * (hf_ref[0] + h_scr[...])).astype(y_ref.dtype)


def _lru_mixer(z_lru, h0_f, h0_b, conv_w, conv_b, wg, bg, lam, tm):
    b, l, c2 = z_lru.shape
    c = c2 // 2
    tm = min(tm, l)
    nt = l // tm
    per = tm // SUBLANES
    nb = l // SUBLANES
    state = jax.ShapeDtypeStruct((b, 1, c), F32)
    state_spec = pl.BlockSpec((1, 1, c), lambda bi, i: (bi, 0, 0))
    cp = _params(("parallel", "arbitrary"))
    par = lambda d: [_const_spec(conv_w.shape), _const_spec((1, c)),
                     pl.BlockSpec((1, c, c2), lambda bi, i: (d, 0, 0)),
                     pl.BlockSpec((1, 1, c2), lambda bi, i: (d, 0, 0)),
                     pl.BlockSpec((1, 1, c), lambda bi, i: (d, 0, 0))]
    args = (conv_w, conv_b.reshape(1, c), wg, bg, lam)

    prev_f, next_f = _halo_specs(tm, l, c, 0)
    hf, hf_last = pl.pallas_call(
        functools.partial(_lru_fwd_kernel, n_tiles=nt),
        grid=(b, nt),
        in_specs=[pl.BlockSpec((1, tm, c), lambda bi, i: (bi, i, 0)), prev_f, next_f, state_spec] + par(0),
        out_specs=[pl.BlockSpec((1, tm, c), lambda bi, i: (bi, i, 0)), state_spec],
        out_shape=[jax.ShapeDtypeStruct((b, l, c), F32), state],
        scratch_shapes=[pltpu.VMEM((tm, c), F32), pltpu.VMEM((tm, c), F32), pltpu.VMEM((1, c), F32)],
        compiler_params=cp,
        name="lru_fwd",
    )(z_lru, z_lru, z_lru, h0_f, *args)

    rev = lambda i: nt - 1 - i
    prev_b = pl.BlockSpec((1, SUBLANES, c), lambda bi, i: (bi, jnp.maximum(rev(i) * per - 1, 0), 0))
    next_b = pl.BlockSpec((1, SUBLANES, c), lambda bi, i: (bi, jnp.minimum((rev(i) + 1) * per, nb - 1), 0))
    y, hb_first = pl.pallas_call(
        functools.partial(_lru_bwd_kernel, n_tiles=nt),
        grid=(b, nt),
        in_specs=[pl.BlockSpec((1, tm, c), lambda bi, i: (bi, rev(i), 0)), prev_b, next_b, state_spec,
                  pl.BlockSpec((1, tm, c), lambda bi, i: (bi, rev(i), 0)),
                  pl.BlockSpec((1, tm, c), lambda bi, i: (bi, rev(i), 1))] + par(1),
        out_specs=[pl.BlockSpec((1, tm, c), lambda bi, i: (bi, rev(i), 0)), state_spec],
        out_shape=[jax.ShapeDtypeStruct((b, l, c), BF16), state],
        scratch_shapes=[pltpu.VMEM((tm, c), F32), pltpu.VMEM((tm, c), F32), pltpu.VMEM((tm, c), F32),
                        pltpu.VMEM((1, c), F32)],
        compiler_params=cp,
        name="lru_bwd",
    )(z_lru, z_lru, z_lru, h0_b, hf, z_lru, *args)
    return y, hf_last, hb_first


def _cmlp_kernel(u_ref, v_ref, g_ref, b_ref, ws_ref, bs_ref, o_ref):
    tm, c = u_ref.shape[1], u_ref.shape[2]
    gd = c // CMLP_GROUPS
    v = _gelu(v_ref[0])
    mu = jnp.mean(v, axis=-1, keepdims=True)
    var = jnp.mean(jnp.square(v - mu), axis=-1, keepdims=True)
    v = ((v - mu) * lax.rsqrt(var + EPS) * g_ref[...] + b_ref[...]).astype(BF16)
    bs = bs_ref[...]
    for ck in range(tm // CHUNK):
        rows = slice(ck * CHUNK, (ck + 1) * CHUNK)
        for g in range(CMLP_GROUPS):
            cols = slice(g * gd, (g + 1) * gd)
            mixed = _dot(ws_ref[g], v[rows, cols]) + bs[:, g:g + 1]
            o_ref[0, rows, cols] = (_gelu(u_ref[0, rows, cols]) * mixed).astype(o_ref.dtype)


def _cmlp_mixer(z_cmlp, ln_g, ln_b, w_s, b_s, tm):
    b, l, c2 = z_cmlp.shape
    c = c2 // 2
    tm = min(tm, l)
    return pl.pallas_call(
        _cmlp_kernel,
        grid=(b, l // tm),
        in_specs=[pl.BlockSpec((1, tm, c), lambda bi, i: (bi, i, 0)),
                  pl.BlockSpec((1, tm, c), lambda bi, i: (bi, i, 1)),
                  _const_spec((1, c)), _const_spec((1, c)), _const_spec(w_s.shape),
                  _const_spec((CHUNK, CMLP_GROUPS))],
        out_specs=pl.BlockSpec((1, tm, c), lambda bi, i: (bi, i, 0)),
        out_shape=jax.ShapeDtypeStruct((b, l, c), BF16),
        compiler_params=_params(("parallel", "parallel")),
        name="chunk_mlp",
    )(z_cmlp, z_cmlp, ln_g.reshape(1, c), ln_b.reshape(1, c), w_s, b_s.T)


def _rms(x, g):
    return x * lax.rsqrt(jnp.mean(x * x, axis=-1, keepdims=True) + EPS) * g


def _mla_proj_kernel(z_ref, qg_ref, kvg_ref, wq_ref, wqr_ref, wk_ref, wvt_ref, vone_ref, place_ref,
                     qc_ref, qs_ref, kcs_ref, q_ref, k_ref, vt_ref, *, q_scale):
    z = z_ref[0]
    kvn = _rms(z[:, :KV_LORA], kvg_ref[...]).astype(BF16)
    qn = _rms(z[:, KV_LORA:KV_LORA + Q_LORA], qg_ref[...]).astype(BF16)
    kr = (z[:, KV_LORA + Q_LORA:] * kcs_ref[...]).astype(BF16)
    k = _dot(kvn, wk_ref[...]) + _dot(kr, place_ref[...])
    k_ref[0] = k.astype(k_ref.dtype)
    v_t = lax.dot_general(wvt_ref[...], kvn, (((1,), (1,)), ((), ())), preferred_element_type=F32)
    vt_ref[0, 0] = (v_t + vone_ref[...]).astype(vt_ref.dtype)
    heads = q_ref.shape[2] // HEAD_PAD
    qc = jnp.concatenate([qc_ref[...]] * heads, axis=1)
    qs = jnp.concatenate([qs_ref[...]] * heads, axis=1)
    q = _dot(qn, wq_ref[...]) * qc + _dot(qn, wqr_ref[...]) * qs
    q_ref[0] = (q * q_scale).astype(q_ref.dtype)


def _mla_proj(z_mla, q_g, kv_g, wq, wqr, wk, wvt, vone, place, q_cos, q_sin, k_cs, q_scale, tm):
    b, l, zc = z_mla.shape
    tm = min(tm, l)
    hp = wq.shape[1]
    vc = wvt.shape[0]
    tab = pl.BlockSpec((tm, HEAD_PAD), lambda bi, i: (i, 0))
    tok = lambda n: pl.BlockSpec((1, tm, n), lambda bi, i: (bi, i, 0))
    return pl.pallas_call(
        functools.partial(_mla_proj_kernel, q_scale=q_scale),
        grid=(b, l // tm),
        in_specs=[tok(zc), _const_spec((1, Q_LORA)), _const_spec((1, KV_LORA)),
                  _const_spec(wq.shape), _const_spec(wqr.shape), _const_spec(wk.shape),
                  _const_spec(wvt.shape), _const_spec(vone.shape), _const_spec(place.shape), tab, tab, tab],
        out_specs=[tok(hp), tok(hp), pl.BlockSpec((1, 1, vc, tm), lambda bi, i: (bi, i, 0, 0))],
        out_shape=[jax.ShapeDtypeStruct((b, l, hp), BF16), jax.ShapeDtypeStruct((b, l, hp), BF16),
                   jax.ShapeDtypeStruct((b, l // tm, vc, tm), BF16)],
        compiler_params=_params(("parallel", "parallel")),
        name="mla_proj",
    )(z_mla, q_g.reshape(1, Q_LORA), kv_g.reshape(1, KV_LORA), wq, wqr, wk, wvt, vone, place, q_cos, q_sin, k_cs)


def _attn_kernel(q_ref, *refs, n_kv):
    kv_refs, o_ref = refs[:2 * n_kv], refs[2 * n_kv]
    s_buf, p_buf, mx_buf, alpha_buf, m_ref, acc_ref, qt_ref = refs[2 * n_kv + 1:]
    hr = acc_ref.shape[0] // 2
    hv = o_ref.shape[2] // 2
    chunks = []
    for j in range(n_kv):
        k_ref, vt_ref = kv_refs[2 * j], kv_refs[2 * j + 1]
        chunks += [(k_ref, vt_ref, ci, vt_ref.shape[3]) for ci in range(vt_ref.shape[1])]
    n = len(chunks)

    qt_ref[...] = q_ref[0].T

    def stage_qk(t):
        k_ref, _, ci, ck = chunks[t]
        for hh in range(2):
            lanes = slice(hh * HEAD_PAD, (hh + 1) * HEAD_PAD)
            s = _dot(k_ref[0, ci * ck:(ci + 1) * ck, lanes], qt_ref[lanes, :])
            s_buf[t % 2, hh, :ck, :] = s
            mx_buf[t % 2, hh] = jnp.max(s, axis=0, keepdims=True)

    def stage_softmax(t):
        ck = chunks[t][3]
        for hh in range(2):
            m_new = mx_buf[t % 2, hh]
            if t > 0:
                m_old = m_ref[hh]
                m_new = jnp.maximum(m_old, m_new)
                alpha_buf[t % 2, hh] = jnp.exp2(m_old - m_new)
            m_ref[hh] = m_new
            p_buf[t % 2, hh, :ck, :] = jnp.exp2(s_buf[t % 2, hh, :ck, :] - m_new).astype(BF16)

    def stage_pv(t):
        _, vt_ref, ci, ck = chunks[t]
        for hh in range(2):
            rows = slice(hh * hr, (hh + 1) * hr)
            pv = _dot(vt_ref[0, ci, rows, :], p_buf[t % 2, hh, :ck, :])
            if t == 0:
                acc_ref[rows, :] = pv
            else:
                acc_ref[rows, :] = alpha_buf[t % 2, hh] * acc_ref[rows, :] + pv

    for t in range(n + 2):
        if 0 <= t - 2 < n:
            stage_pv(t - 2)
        if t < n:
            stage_qk(t)
        if 0 <= t - 1 < n:
            stage_softmax(t - 1)
    outs = [acc_ref[hh * hr:hh * hr + hv, :] * (1.0 / acc_ref[hh * hr + hv:hh * hr + hv + 1, :])
            for hh in range(2)]
    o_ref[0] = jnp.concatenate(outs, axis=0).T.astype(o_ref.dtype)


def _attention(q, kvs, tq):
    b, l, hp = q.shape
    heads = hp // HEAD_PAD
    tq = min(tq, l)
    in_specs = [pl.BlockSpec((1, tq, 2 * HEAD_PAD), lambda bi, h, i: (bi, i, h))]
    args = [q]
    ck_max = 0
    for k, vt in kvs:
        _, n_chunks, _, ck = vt.shape
        ck_max = max(ck_max, ck)
        in_specs.append(pl.BlockSpec((1, k.shape[1], 2 * HEAD_PAD), lambda bi, h, i: (bi, 0, h)))
        in_specs.append(pl.BlockSpec((1, n_chunks, 2 * VT_ROWS, ck), lambda bi, h, i: (bi, 0, h, 0)))
        args += [k, vt]
    stat = pltpu.VMEM((2, 2, 1, tq), F32)
    return pl.pallas_call(
        functools.partial(_attn_kernel, n_kv=len(kvs)),
        grid=(b, heads // 2, l // tq),
        in_specs=in_specs,
        out_specs=pl.BlockSpec((1, tq, 2 * QK_NOPE), lambda bi, h, i: (bi, i, h)),
        out_shape=jax.ShapeDtypeStruct((b, l, heads * QK_NOPE), BF16),
        scratch_shapes=[pltpu.VMEM((2, 2, ck_max, tq), F32), pltpu.VMEM((2, 2, ck_max, tq), BF16),
                        stat, stat, pltpu.VMEM((2, 1, tq), F32), pltpu.VMEM((2 * VT_ROWS, tq), F32),
                        pltpu.VMEM((2 * HEAD_PAD, tq), BF16)],
        compiler_params=_params(("parallel", "parallel", "parallel")),
        name="attention",
    )(*args)


def _merge_kernel(x_ref, sh_ref, sc_ref, gt_ref, g_ref, ya_ref, yb_ref, yc_ref, yd_ref,
                  wg_ref, wb_ref, wo_ref, o_ref):
    x = x_ref[0]
    d = x.shape[1]
    h = _norm_mod(x, g_ref[...], sh_ref[0], sc_ref[0]).astype(BF16)
    merged = None
    for n, y_ref in enumerate((ya_ref, yb_ref, yc_ref, yd_ref)):
        gate = jax.nn.sigmoid(_dot(h, wg_ref[:, n * d:(n + 1) * d]))
        term = gate * _dot(y_ref[0], wb_ref[n])
        merged = term if merged is None else merged + term
    o_ref[0] = x + gt_ref[0] * _dot(merged.astype(BF16), wo_ref[...])


def _merge(x, mod3, row_of, g, ys, wg, wb, wo, tm):
    b, l, d = x.shape
    tm = min(tm, l)
    c = ys[0].shape[2]
    tok = lambda n: pl.BlockSpec((1, tm, n), lambda bi, i: (bi, i, 0))
    modv = lambda j: pl.BlockSpec((1, 1, d), lambda bi, i: (row_of(bi), 0, j))
    return pl.pallas_call(
        _merge_kernel,
        grid=(b, l // tm),
        in_specs=[tok(d), modv(0), modv(1), modv(2), _const_spec((1, d)), tok(c), tok(c), tok(c), tok(c),
                  _const_spec(wg.shape), _const_spec(wb.shape), _const_spec(wo.shape)],
        out_specs=tok(d),
        out_shape=jax.ShapeDtypeStruct((b, l, d), F32),
        compiler_params=_params(("parallel", "parallel")),
        name="merge",
    )(x, mod3, mod3, mod3, g.reshape(1, d), *ys, wg, wb, wo)


def _ffn_kernel(x_ref, sh_ref, sc_ref, gt_ref, g_ref, w1_ref, w3_ref, w2_ref, fg_ref, o_ref, *, final_norm):
    x = x_ref[0]
    h = _norm_mod(x, g_ref[...], sh_ref[0], sc_ref[0]).astype(BF16)
    a = _dot(h, w1_ref[...])
    u = (a * jax.nn.sigmoid(a) * _dot(h, w3_ref[...])).astype(BF16)
    y = x + gt_ref[0] * _dot(u, w2_ref[...])
    if final_norm:
        y = _rms(y, fg_ref[...])
    o_ref[0] = y


def _ffn(x, mod3, row_of, g, w1, w3, w2, final_g, final_norm, tm):
    b, l, d = x.shape
    tm = min(tm, l)
    tok = pl.BlockSpec((1, tm, d), lambda bi, i: (bi, i, 0))
    modv = lambda j: pl.BlockSpec((1, 1, d), lambda bi, i: (row_of(bi), 0, j))
    return pl.pallas_call(
        functools.partial(_ffn_kernel, final_norm=final_norm),
        grid=(b, l // tm),
        in_specs=[tok, modv(3), modv(4), modv(5), _const_spec((1, d)),
                  _const_spec(w1.shape), _const_spec(w3.shape), _const_spec(w2.shape), _const_spec((1, d))],
        out_specs=tok,
        out_shape=jax.ShapeDtypeStruct((b, l, d), F32),
        compiler_params=_params(("parallel", "parallel")),
        name="ffn",
    )(x, mod3, mod3, mod3, g.reshape(1, d), w1, w3, w2, final_g.reshape(1, d))


def _rotate_half_cols(w):
    q = QK_ROPE // 4
    return jnp.concatenate([-w[:, q:2 * q], w[:, :q], -w[:, 3 * q:], w[:, 2 * q:3 * q]], axis=1)


def _block_diag(w):
    h, hd, _ = w.shape
    eye = jnp.eye(h, dtype=w.dtype)
    return (eye[:, None, :, None] * w[:, :, None, :]).reshape(h * hd, h * hd)


def _layer_weights(w_in_l, lru_w_a, lru_w_x, lru_b_a, lru_b_x, lru_lam, w_q_up, w_kv_up, cmlp_w_s,
                   w_branch, w_out, w_ff1, w_ff3, w_ff2):
    d = w_in_l.shape[0]
    c = d // 2
    o = {}
    i_lru_x, i_kv, i_kr, i_lru_g, i_q = 0, c, c + KV_LORA, c + KV_LORA + QK_ROPE, 2 * c + KV_LORA + QK_ROPE
    i_ab = i_q + Q_LORA
    i_ac, i_ax, i_cu, i_cv, i_gate = i_ab + c, i_ab + 2 * c, i_ab + 3 * c, i_ab + 4 * c, i_ab + 5 * c
    col = lambda s, n: w_in_l[:, s:s + n]
    k_rope = col(i_kr, QK_ROPE)
    mla_pad = jnp.zeros((d, HEAD_PAD - 2 * QK_ROPE), F32)
    o["w_lru"] = jnp.concatenate([col(i_lru_x, c), col(i_lru_g, c)], axis=1).astype(BF16)
    o["w_mla"] = jnp.concatenate([col(i_kv, KV_LORA), col(i_q, Q_LORA), k_rope, _rotate_half_cols(k_rope),
                                  mla_pad], axis=1).astype(BF16)
    o["w_conv"] = w_in_l[:, i_ab:i_cu].astype(BF16)
    o["w_cmlp"] = w_in_l[:, i_cu:i_gate].astype(BF16)
    o["w_gate"] = w_in_l[:, i_gate:].astype(BF16)
    o["lru_wg"] = jnp.stack([jnp.concatenate([_block_diag(lru_w_a[dd]), _block_diag(lru_w_x[dd])], axis=1)
                             for dd in range(2)]).astype(BF16)
    o["lru_bg"] = jnp.concatenate([lru_b_a, lru_b_x], axis=1).reshape(2, 1, 2 * c)
    o["lru_lam"] = lru_lam.reshape(2, 1, c)
    hq = QK_NOPE + QK_ROPE
    wq = w_q_up.reshape(Q_LORA, MLA_HEADS, hq)
    zq = jnp.zeros((Q_LORA, MLA_HEADS, HEAD_PAD - hq), F32)
    o["wq"] = jnp.concatenate([wq, zq], axis=2).reshape(Q_LORA, MLA_HEADS * HEAD_PAD).astype(BF16)
    rot = jnp.stack([_rotate_half_cols(wq[:, h, QK_NOPE:]) for h in range(MLA_HEADS)], axis=1)
    o["wqr"] = jnp.concatenate([jnp.zeros((Q_LORA, MLA_HEADS, QK_NOPE), F32), rot, zq],
                               axis=2).reshape(Q_LORA, MLA_HEADS * HEAD_PAD).astype(BF16)
    wkv = w_kv_up.reshape(KV_LORA, MLA_HEADS, 2 * QK_NOPE)
    o["wk"] = jnp.concatenate([wkv[:, :, :QK_NOPE], jnp.zeros((KV_LORA, MLA_HEADS, HEAD_PAD - QK_NOPE), F32)],
                              axis=2).reshape(KV_LORA, MLA_HEADS * HEAD_PAD).astype(BF16)
    wv = jnp.concatenate([wkv[:, :, QK_NOPE:], jnp.zeros((KV_LORA, MLA_HEADS, ONES_ROWS), F32)], axis=2)
    o["wvt"] = wv.reshape(KV_LORA, MLA_HEADS * VT_ROWS).T.astype(BF16)
    o["w_s"] = cmlp_w_s.astype(BF16)
    o["w_branch"] = w_branch.astype(BF16)
    o["w_out"] = w_out.astype(BF16)
    o["w_ff1"] = w_ff1.astype(BF16)
    o["w_ff3"] = w_ff3.astype(BF16)
    o["w_ff2"] = w_ff2.astype(BF16)
    return o


def _rope_place():
    eye = jnp.eye(QK_ROPE, dtype=F32)
    blk = jnp.zeros((HEAD_PAD, HEAD_PAD), F32)
    blk = blk.at[:QK_ROPE, QK_NOPE:QK_NOPE + QK_ROPE].set(eye)
    blk = blk.at[QK_ROPE:2 * QK_ROPE, QK_NOPE:QK_NOPE + QK_ROPE].set(eye)
    return jnp.concatenate([blk] * MLA_HEADS, axis=1).astype(BF16)


def _rope_tables(n_lat):
    t = jnp.arange(n_lat)
    n_freq = QK_ROPE // 4
    inv = ROPE_THETA ** (-jnp.arange(n_freq, dtype=F32) / n_freq)
    ang_r = (t // GRID_W).astype(F32)[:, None] * inv
    ang_c = (t % GRID_W).astype(F32)[:, None] * inv
    cos = jnp.concatenate([jnp.cos(ang_r)] * 2 + [jnp.cos(ang_c)] * 2, axis=1)
    sin = jnp.concatenate([jnp.sin(ang_r)] * 2 + [jnp.sin(ang_c)] * 2, axis=1)
    ones = jnp.ones((n_lat, QK_NOPE), F32)
    z32 = jnp.zeros((n_lat, HEAD_PAD - QK_NOPE - QK_ROPE), F32)
    q_cos = jnp.concatenate([ones, cos, z32], axis=1)
    q_sin = jnp.concatenate([jnp.zeros((n_lat, QK_NOPE), F32), sin, z32], axis=1)
    k_cs = jnp.concatenate([cos, sin, jnp.zeros((n_lat, HEAD_PAD - 2 * QK_ROPE), F32)], axis=1)
    return q_cos, q_sin, k_cs


def _identity_tables(n):
    q_cos = jnp.concatenate([jnp.ones((n, QK_NOPE + QK_ROPE), F32),
                             jnp.zeros((n, HEAD_PAD - QK_NOPE - QK_ROPE), F32)], axis=1)
    q_sin = jnp.zeros((n, HEAD_PAD), F32)
    k_cs = jnp.concatenate([jnp.ones((n, QK_ROPE), F32), jnp.zeros((n, HEAD_PAD - QK_ROPE), F32)], axis=1)
    return q_cos, q_sin, k_cs


TM_PROJ = 512
TM_MIX = 512
TQ_ATTN = 512
CK_ATTN = 256


def kernel(x, c, ctx, c_ctx, w_mod, b_mod, norm1_g, norm2_g, w_in, conv_a_w, lru_conv_w, lru_conv_b,
           lru_w_a, lru_b_a, lru_w_x, lru_b_x, lru_lam, cmlp_ln_g, cmlp_ln_b, cmlp_w_s, cmlp_b_s,
           mla_q_norm_g, mla_kv_norm_g, mla_w_q_up, mla_w_kv_up, w_branch, w_out, w_ff1, w_ff3, w_ff2,
           final_norm_g):
    bsz, n_lat, d = x.shape
    n_ctx = ctx.shape[1]
    depth = w_in.shape[0]
    cw = d // 2
    q_scale = (QK_NOPE + QK_ROPE) ** -0.5 * LOG2E
    row_in_head = jnp.arange(MLA_HEADS * VT_ROWS) % VT_ROWS
    vone = (row_in_head >= QK_NOPE).astype(F32)[:, None]

    rows = -(-(bsz + 1) // SUBLANES) * SUBLANES
    s_rows = jnp.concatenate([c, c_ctx[None, :], jnp.zeros((rows - bsz - 1, d), F32)], axis=0)
    mod = _modulation(s_rows, w_mod, b_mod)
    lat_row = lambda bi: bi
    ctx_row = lambda bi: bsz

    place = _rope_place()
    lat_tabs = _rope_tables(n_lat)
    ctx_tabs = _identity_tables(n_ctx)
    zero_state = jnp.zeros((bsz, 1, cw), F32)

    xc = ctx
    for l in range(depth):
        last = l == depth - 1
        w = _layer_weights(w_in[l], lru_w_a[l], lru_w_x[l], lru_b_a[l], lru_b_x[l], lru_lam[l],
                           mla_w_q_up[l], mla_w_kv_up[l], cmlp_w_s[l], w_branch[l], w_out[l],
                           w_ff1[l], w_ff3[l], w_ff2[l])
        mod3 = mod[l].reshape(rows, 1, 6 * d)
        proj_w = [w["w_lru"], w["w_mla"], w["w_conv"], w["w_cmlp"]]

        zc_lru, zc_mla, zc_conv, zc_cmlp = _inproj(xc, mod3, ctx_row, norm1_g[l], proj_w, TM_PROJ)
        yc_b, hc_f, hc_b = _lru_mixer(zc_lru, zero_state, zero_state, lru_conv_w[l], lru_conv_b[l],
                                      w["lru_wg"], w["lru_bg"], w["lru_lam"], TM_MIX)
        qc, kc, vc = _mla_proj(zc_mla, mla_q_norm_g[l], mla_kv_norm_g[l], w["wq"], w["wqr"], w["wk"], w["wvt"],
                               vone, place, *ctx_tabs, q_scale, CK_ATTN)

        z_lru, z_mla, z_conv, z_cmlp = _inproj(x, mod3, lat_row, norm1_g[l], proj_w, TM_PROJ)
        y_b, _, _ = _lru_mixer(z_lru, hc_f, hc_b, lru_conv_w[l], lru_conv_b[l],
                               w["lru_wg"], w["lru_bg"], w["lru_lam"], TM_MIX)
        q, k, v = _mla_proj(z_mla, mla_q_norm_g[l], mla_kv_norm_g[l], w["wq"], w["wqr"], w["wk"], w["wvt"],
                            vone, place, *lat_tabs, q_scale, CK_ATTN)
        y_d = _attention(q, [(k, v), (kc, vc)], TQ_ATTN)
        y_a = _conv_mixer(z_conv, conv_a_w[l], TM_MIX)
        y_c = _cmlp_mixer(z_cmlp, cmlp_ln_g[l], cmlp_ln_b[l], w["w_s"], cmlp_b_s[l], TM_MIX)
        x = _merge(x, mod3, lat_row, norm1_g[l], (y_a, y_b, y_c, y_d), w["w_gate"], w["w_branch"], w["w_out"],
                   TM_PROJ)
        x = _ffn(x, mod3, lat_row, norm2_g[l], w["w_ff1"], w["w_ff3"], w["w_ff2"], final_norm_g, last, TM_PROJ)

        if not last:
            yc_d = _attention(qc, [(kc, vc)], TQ_ATTN)
            yc_a = _conv_mixer(zc_conv, conv_a_w[l], TM_MIX)
            yc_c = _cmlp_mixer(zc_cmlp, cmlp_ln_g[l], cmlp_ln_b[l], w["w_s"], cmlp_b_s[l], TM_MIX)
            xc = _merge(xc, mod3, ctx_row, norm1_g[l], (yc_a, yc_b, yc_c, yc_d), w["w_gate"], w["w_branch"],
                        w["w_out"], TM_PROJ)
            xc = _ffn(xc, mod3, ctx_row, norm2_g[l], w["w_ff1"], w["w_ff3"], w["w_ff2"], final_norm_g, False,
                      TM_PROJ)
    return x
```

```python
import functools

import jax
import jax.numpy as jnp
from jax import lax
from jax.experimental import pallas as pl
from jax.experimental.pallas import tpu as pltpu

F32 = jnp.float32
BF16 = jnp.bfloat16

EPS = 1e-6
GRID_W = 64
N_BRANCH = 4
LRU_HEADS = 8
LRU_C = 8.0
CMLP_GROUPS = 4
CHUNK = 128
MLA_HEADS = 8
QK_NOPE = 64
QK_ROPE = 32
Q_LORA = 384
KV_LORA = 256
ROPE_THETA = 10000.0
HEAD_PAD = 128
ONES_ROWS = 16
VT_ROWS = QK_NOPE + ONES_ROWS
LOG2E = 1.4426950408889634
SUBLANES = 8
VMEM_LIMIT = 56 * 1024 * 1024


def _params(sem, vmem=VMEM_LIMIT, flags=None):
    return pltpu.CompilerParams(dimension_semantics=sem, vmem_limit_bytes=vmem, flags=flags)


def _const_spec(shape):
    zeros = (0,) * len(shape)
    return pl.BlockSpec(shape, lambda *_: zeros, pipeline_mode=pl.Buffered(1))


def _gelu(x):
    return jax.nn.gelu(x)


def _norm_mod(x, g, shift, scale):
    y = x * lax.rsqrt(jnp.mean(x * x, axis=-1, keepdims=True) + EPS) * g
    return y * (1.0 + scale) + shift


def _dot(a, b):
    return jnp.dot(a, b, preferred_element_type=F32)


def _mod_kernel(s_ref, w_ref, b_ref, o_ref):
    s = s_ref[...]
    s = s * jax.nn.sigmoid(s)
    o_ref[0] = _dot(s.astype(BF16), w_ref[0].astype(BF16)) + b_ref[0]


def _modulation(s_rows, w_mod, b_mod):
    n_layer, d, d6 = w_mod.shape
    r = s_rows.shape[0]
    return pl.pallas_call(
        _mod_kernel,
        grid=(n_layer, d6 // d),
        in_specs=[
            pl.BlockSpec((r, d), lambda l, j: (0, 0)),
            pl.BlockSpec((1, d, d), lambda l, j: (l, 0, j)),
            pl.BlockSpec((1, 1, d), lambda l, j: (l, 0, j)),
        ],
        out_specs=pl.BlockSpec((1, r, d), lambda l, j: (l, 0, j)),
        out_shape=jax.ShapeDtypeStruct((n_layer, r, d6), F32),
        compiler_params=_params(("parallel", "parallel")),
        name="modulation",
    )(s_rows, w_mod, b_mod.reshape(n_layer, 1, d6))


def _inproj_kernel(x_ref, sh_ref, sc_ref, g_ref, w_lru, w_conv, w_cmlp, w_mla,
                   lng_ref, lnb_ref, ws_ref, bs_ref,
                   qg_ref, kvg_ref, wq_ref, wqr_ref, wk_ref, wvt_ref, vone_ref, place_ref, qc_ref, qs_ref, kcs_ref,
                   o_lru, o_conv, o_yc, q_ref, k_ref, vt_ref, *, q_scale):
    h = _norm_mod(x_ref[0], g_ref[...], sh_ref[0], sc_ref[0]).astype(BF16)
    o_lru[0] = _dot(h, w_lru[...])
    zc = _dot(h, w_conv[...])
    c = zc.shape[1] // 3
    o_conv[0, :, :c] = zc[:, :c]
    o_conv[0, :, c:] = zc[:, c:2 * c] * zc[:, 2 * c:]
    _cmlp_body(_dot(h, w_cmlp[...]), lng_ref, lnb_ref, ws_ref, bs_ref, o_yc)
    _mla_body(_dot(h, w_mla[...]), qg_ref, kvg_ref, wq_ref, wqr_ref, wk_ref, wvt_ref, vone_ref, place_ref,
              qc_ref, qs_ref, kcs_ref, q_ref, k_ref, vt_ref, q_scale)


def _inproj(x, mod3, row_of, g, w, cmlp, mla, tabs, q_scale, tm):
    b, l, d = x.shape
    tm = min(tm, l)
    c = d // 2
    ln_g, ln_b, b_s = cmlp
    q_g, kv_g, vone, place = mla
    hp = w["wq"].shape[1]
    vrows = w["wvt"].shape[0]
    consts = [g.reshape(1, d), w["w_lru"], w["w_conv"], w["w_cmlp"], w["w_mla"],
              ln_g.reshape(1, c), ln_b.reshape(1, c), w["w_s"], b_s.T,
              q_g.reshape(1, Q_LORA), kv_g.reshape(1, KV_LORA), w["wq"], w["wqr"], w["wk"], w["wvt"], vone, place]
    tab = pl.BlockSpec((tm, HEAD_PAD), lambda bi, i: (i, 0))
    tok = lambda n: pl.BlockSpec((1, tm, n), lambda bi, i: (bi, i, 0))
    in_specs = ([tok(d),
                 pl.BlockSpec((1, 1, d), lambda bi, i: (row_of(bi), 0, 0)),
                 pl.BlockSpec((1, 1, d), lambda bi, i: (row_of(bi), 0, 1))]
                + [_const_spec(a.shape) for a in consts] + [tab, tab, tab])
    out_specs = [tok(2 * c), tok(2 * c), tok(c), tok(hp), tok(hp),
                 pl.BlockSpec((1, vrows, tm), lambda bi, i: (bi, 0, i))]
    out_shape = [jax.ShapeDtypeStruct((b, l, 2 * c), F32), jax.ShapeDtypeStruct((b, l, 2 * c), F32),
                 jax.ShapeDtypeStruct((b, l, c), BF16), jax.ShapeDtypeStruct((b, l, hp), BF16),
                 jax.ShapeDtypeStruct((b, l, hp), BF16), jax.ShapeDtypeStruct((b, vrows, l), BF16)]
    return pl.pallas_call(
        functools.partial(_inproj_kernel, q_scale=q_scale),
        grid=(b, l // tm),
        in_specs=in_specs,
        out_specs=out_specs,
        out_shape=out_shape,
        compiler_params=_params(("parallel", "parallel")),
        name="inproj",
    )(x, mod3, mod3, *consts, *tabs)


def _halo_specs(tm, l, c, lane_blk):
    nb = l // SUBLANES
    per = tm // SUBLANES
    prev = pl.BlockSpec((1, SUBLANES, c), lambda bi, i: (bi, jnp.maximum(i * per - 1, 0), lane_blk))
    nxt = pl.BlockSpec((1, SUBLANES, c), lambda bi, i: (bi, jnp.minimum((i + 1) * per, nb - 1), lane_blk))
    return prev, nxt


def _shift_down(p, first_row, k):
    row = lax.broadcasted_iota(jnp.int32, p.shape, 0)
    out = pltpu.roll(p, k, 0)
    for j in range(k):
        out = jnp.where(row == j, first_row[j:j + 1, :], out)
    return out


def _shift_up(p, last_row):
    tm = p.shape[0]
    row = lax.broadcasted_iota(jnp.int32, p.shape, 0)
    return jnp.where(row == tm - 1, last_row, pltpu.roll(p, tm - 1, 0))


def _conv_kernel(ab_ref, p_ref, pp_ref, pn_ref, w_ref, o_ref, *, n_tiles):
    i = pl.program_id(1)
    p = p_ref[0]
    p_prev = jnp.where(i > 0, pp_ref[0, SUBLANES - 1:, :], 0.0)
    p_next = jnp.where(i < n_tiles - 1, pn_ref[0, 0:1, :], 0.0)
    w = w_ref[...]
    conv = w[0:1] * _shift_down(p, p_prev, 1) + w[1:2] * p + w[2:3] * _shift_up(p, p_next)
    o_ref[0] = (ab_ref[0] * conv).astype(o_ref.dtype)


def _conv_mixer(z_conv, w_conv, tm):
    b, l, c2 = z_conv.shape
    c = c2 // 2
    tm = min(tm, l)
    main = lambda blk: pl.BlockSpec((1, tm, c), lambda bi, i: (bi, i, blk))
    prev, nxt = _halo_specs(tm, l, c, 1)
    return pl.pallas_call(
        functools.partial(_conv_kernel, n_tiles=l // tm),
        grid=(b, l // tm),
        in_specs=[main(0), main(1), prev, nxt, _const_spec(w_conv.shape)],
        out_specs=pl.BlockSpec((1, tm, c), lambda bi, i: (bi, i, 0)),
        out_shape=jax.ShapeDtypeStruct((b, l, c), BF16),
        compiler_params=_params(("parallel", "parallel")),
        name="conv_mixer",
    )(z_conv, z_conv, z_conv, z_conv, w_conv)


def _lru_coeffs(x_ref, xp_ref, xn_ref, cw_ref, cb_ref, wg_ref, bg_ref, lam_ref, i, n_tiles):
    x = x_ref[0]
    c = x.shape[1]
    prev = jnp.where(i > 0, xp_ref[0, SUBLANES - 2:, :], 0.0)
    nxt = jnp.where(i < n_tiles - 1, xn_ref[0, 0:1, :], 0.0)
    cw = cw_ref[...]
    x_m1 = _shift_down(x, prev[1:2], 1)
    x_m2 = _shift_down(x, prev, 2)
    xl = cw[0:1] * x_m2 + cw[1:2] * x_m1 + cw[2:3] * x + cw[3:4] * _shift_up(x, nxt) + cb_ref[...]
    gates = _dot(xl.astype(BF16), wg_ref[0]) + bg_ref[0]
    r = jax.nn.sigmoid(gates[:, :c])
    gi = jax.nn.sigmoid(gates[:, c:])
    lam = lam_ref[0]
    softplus_neg = jnp.maximum(-lam, 0.0) + jnp.log1p(jnp.exp(-jnp.abs(lam)))
    a = jnp.exp2((-LRU_C * LOG2E * softplus_neg) * r)
    y = 1.0 - a * a
    root = jnp.where(y > 0.0, y * lax.rsqrt(y), 0.0)
    return a, root * (gi * xl)


def _scan_tile(a, bb, a_scr, b_scr, h_scr, carry_ref, reverse):
    tm, c = a.shape
    groups = tm // SUBLANES
    acc_a = a.reshape(groups, SUBLANES, c)
    acc_b = bb.reshape(groups, SUBLANES, c)
    r8 = lax.broadcasted_iota(jnp.int32, acc_a.shape, 1)
    for k in (1, 2, 4):
        if reverse:
            keep = r8 < SUBLANES - k
            shift = SUBLANES - k
        else:
            keep = r8 >= k
            shift = k
        a_sh = jnp.where(keep, pltpu.roll(acc_a, shift, 1), 1.0)
        b_sh = jnp.where(keep, pltpu.roll(acc_b, shift, 1), 0.0)
        acc_b = acc_a * b_sh + acc_b
        acc_a = acc_a * a_sh
    a_scr[...] = acc_a.reshape(tm, c)
    b_scr[...] = acc_b.reshape(tm, c)

    def body(gi, h_in):
        g = groups - 1 - gi if reverse else gi
        rows = pl.ds(pl.multiple_of(g * SUBLANES, SUBLANES), SUBLANES)
        h8 = b_scr[rows, :] + a_scr[rows, :] * h_in
        h_scr[rows, :] = h8
        return h8[0:1, :] if reverse else h8[SUBLANES - 1:, :]

    carry_ref[...] = lax.fori_loop(0, groups, body, carry_ref[...], unroll=4)


def _lru_fwd_kernel(x_ref, xp_ref, xn_ref, h0_ref, cw_ref, cb_ref, wg_ref, bg_ref, lam_ref,
                    h_ref, hlast_ref, a_scr, b_scr, carry, *, n_tiles):
    i = pl.program_id(1)

    @pl.when(i == 0)
    def _():
        carry[...] = h0_ref[0]

    a, bb = _lru_coeffs(x_ref, xp_ref, xn_ref, cw_ref, cb_ref, wg_ref, bg_ref, lam_ref, i, n_tiles)
    _scan_tile(a, bb, a_scr, b_scr, h_ref.at[0], carry, reverse=False)
    hlast_ref[0] = carry[...]


def _lru_bwd_kernel(x_ref, xp_ref, xn_ref, h0_ref, hf_ref, g_ref, cw_ref, cb_ref, wg_ref, bg_ref, lam_ref,
                    y_ref, hfirst_ref, a_scr, b_scr, h_scr, carry, *, n_tiles):
    i = pl.program_id(1)

    @pl.when(i == 0)
    def _():
        carry[...] = h0_ref[0]

    tile = n_tiles - 1 - i
    a, bb = _lru_coeffs(x_ref, xp_ref, xn_ref, cw_ref, cb_ref, wg_ref, bg_ref, lam_ref, tile, n_tiles)
    _scan_tile(a, bb, a_scr, b_scr, h_scr, carry, reverse=True)
    hfirst_ref[0] = carry[...]
    y_ref[0] = (_gelu(g_ref[0]) * (hf_ref[0] + h_scr[...])).astype(y_ref.dtype)


def _lru_mixer(z_lru, h0_f, h0_b, conv_w, conv_b, wg, bg, lam, tm):
    b, l, c2 = z_lru.shape
    c = c2 // 2
    tm = min(tm, l)
    nt = l // tm
    per = tm // SUBLANES
    nb = l // SUBLANES
    state = jax.ShapeDtypeStruct((b, 1, c), F32)
    state_spec = pl.BlockSpec((1, 1, c), lambda bi, i: (bi, 0, 0))
    cp = _params(("parallel", "arbitrary"))
    par = lambda d: [_const_spec(conv_w.shape), _const_spec((1, c)),
                     pl.BlockSpec((1, c, c2), lambda bi, i: (d, 0, 0)),
                     pl.BlockSpec((1, 1, c2), lambda bi, i: (d, 0, 0)),
                     pl.BlockSpec((1, 1, c), lambda bi, i: (d, 0, 0))]
    args = (conv_w, conv_b.reshape(1, c), wg, bg, lam)

    prev_f, next_f = _halo_specs(tm, l, c, 0)
    hf, hf_last = pl.pallas_call(
        functools.partial(_lru_fwd_kernel, n_tiles=nt),
        grid=(b, nt),
        in_specs=[pl.BlockSpec((1, tm, c), lambda bi, i: (bi, i, 0)), prev_f, next_f, state_spec] + par(0),
        out_specs=[pl.BlockSpec((1, tm, c), lambda bi, i: (bi, i, 0)), state_spec],
        out_shape=[jax.ShapeDtypeStruct((b, l, c), F32), state],
        scratch_shapes=[pltpu.VMEM((tm, c), F32), pltpu.VMEM((tm, c), F32), pltpu.VMEM((1, c), F32)],
        compiler_params=cp,
        name="lru_fwd",
    )(z_lru, z_lru, z_lru, h0_f, *args)

    rev = lambda i: nt - 1 - i
    prev_b = pl.BlockSpec((1, SUBLANES, c), lambda bi, i: (bi, jnp.maximum(rev(i) * per - 1, 0), 0))
    next_b = pl.BlockSpec((1, SUBLANES, c), lambda bi, i: (bi, jnp.minimum((rev(i) + 1) * per, nb - 1), 0))
    y, hb_first = pl.pallas_call(
        functools.partial(_lru_bwd_kernel, n_tiles=nt),
        grid=(b, nt),
        in_specs=[pl.BlockSpec((1, tm, c), lambda bi, i: (bi, rev(i), 0)), prev_b, next_b, state_spec,
                  pl.BlockSpec((1, tm, c), lambda bi, i: (bi, rev(i), 0)),
                  pl.BlockSpec((1, tm, c), lambda bi, i: (bi, rev(i), 1))] + par(1),
        out_specs=[pl.BlockSpec((1, tm, c), lambda bi, i: (bi, rev(i), 0)), state_spec],
        out_shape=[jax.ShapeDtypeStruct((b, l, c), BF16), state],
        scratch_shapes=[pltpu.VMEM((tm, c), F32), pltpu.VMEM((tm, c), F32), pltpu.VMEM((tm, c), F32),
                        pltpu.VMEM((1, c), F32)],
        compiler_params=cp,
        name="lru_bwd",
    )(z_lru, z_lru, z_lru, h0_b, hf, z_lru, *args)
    return y, hf_last, hb_first


def _cmlp_body(z, g_ref, b_ref, ws_ref, bs_ref, o_ref):
    tm, c = z.shape[0], z.shape[1] // 2
    gd = c // CMLP_GROUPS
    u = _gelu(z[:, :c])
    v = _gelu(z[:, c:])
    mu = jnp.mean(v, axis=-1, keepdims=True)
    var = jnp.mean(jnp.square(v - mu), axis=-1, keepdims=True)
    v = ((v - mu) * lax.rsqrt(var + EPS) * g_ref[...] + b_ref[...]).astype(BF16)
    bs = bs_ref[...]
    for ck in range(tm // CHUNK):
        rows = slice(ck * CHUNK, (ck + 1) * CHUNK)
        for g in range(CMLP_GROUPS):
            cols = slice(g * gd, (g + 1) * gd)
            mixed = _dot(ws_ref[g], v[rows, cols]) + bs[:, g:g + 1]
            o_ref[0, rows, cols] = (u[rows, cols] * mixed).astype(o_ref.dtype)


def _rms(x, g):
    return x * lax.rsqrt(jnp.mean(x * x, axis=-1, keepdims=True) + EPS) * g


def _mla_body(z, qg_ref, kvg_ref, wq_ref, wqr_ref, wk_ref, wvt_ref, vone_ref, place_ref,
              qc_ref, qs_ref, kcs_ref, q_ref, k_ref, vt_ref, q_scale):
    kvn = _rms(z[:, :KV_LORA], kvg_ref[...]).astype(BF16)
    qn = _rms(z[:, KV_LORA:KV_LORA + Q_LORA], qg_ref[...]).astype(BF16)
    kr = (z[:, KV_LORA + Q_LORA:] * kcs_ref[...]).astype(BF16)
    k = _dot(kvn, wk_ref[...]) + _dot(kr, place_ref[...])
    k_ref[0] = k.astype(k_ref.dtype)
    v_t = lax.dot_general(wvt_ref[...], kvn, (((1,), (1,)), ((), ())), preferred_element_type=F32)
    vt_ref[0] = (v_t + vone_ref[...]).astype(vt_ref.dtype)
    heads = q_ref.shape[2] // HEAD_PAD
    qc = jnp.concatenate([qc_ref[...]] * heads, axis=1)
    qs = jnp.concatenate([qs_ref[...]] * heads, axis=1)
    q = _dot(qn, wq_ref[...]) * qc + _dot(qn, wqr_ref[...]) * qs
    q_ref[0] = (q * q_scale).astype(q_ref.dtype)


def _attn_kernel(q_ref, *refs, n_kv, tu, kb):
    kv_refs, o_ref = refs[:2 * n_kv], refs[2 * n_kv]
    s_buf, p_buf, mx_buf, acc_ref, ot_ref, qt_ref = refs[2 * n_kv + 1:]
    hv = o_ref.shape[2] // 2
    hr = kv_refs[1].shape[1] // 2
    tq = q_ref.shape[1]
    blocks = []
    off = 0
    for j in range(n_kv):
        k_ref, vt_ref = kv_refs[2 * j], kv_refs[2 * j + 1]
        lk = k_ref.shape[1]
        for s0 in range(0, lk, kb):
            blocks.append((k_ref, vt_ref, s0, off + s0, min(kb, lk - s0)))
        off += lk
    units = [(qi, hh) for qi in range(tq // tu) for hh in range(2)]
    n = len(units)

    qt_ref[...] = q_ref[0].T

    def stage_qk(u, bi):
        qi, hh = units[u]
        k_ref, _, s0, g0, size = blocks[bi]
        lanes = slice(hh * HEAD_PAD, (hh + 1) * HEAD_PAD)
        s = _dot(k_ref[0, s0:s0 + size, lanes], qt_ref[lanes, qi * tu:(qi + 1) * tu])
        s_buf[g0:g0 + size, :] = s
        m = jnp.max(s, axis=0, keepdims=True)
        mx_buf[u % 2] = m if bi == 0 else jnp.maximum(mx_buf[u % 2], m)

    def stage_exp(u, bi):
        _, _, _, g0, size = blocks[bi]
        p_buf[g0:g0 + size, :] = jnp.exp2(s_buf[g0:g0 + size, :] - mx_buf[u % 2]).astype(BF16)

    def stage_pv(u, bi):
        qi, hh = units[u]
        _, vt_ref, s0, g0, size = blocks[bi]
        pv = _dot(vt_ref[0, hh * hr:(hh + 1) * hr, s0:s0 + size], p_buf[g0:g0 + size, :])
        if bi > 0:
            pv = pv + acc_ref[...]
        if bi < len(blocks) - 1:
            acc_ref[...] = pv
        else:
            ot_ref[hh * hv:(hh + 1) * hv, qi * tu:(qi + 1) * tu] = pv[:hv] * (1.0 / pv[hv:hv + 1])

    for t in range(n + 2):
        for bi in range(len(blocks)):
            if 0 <= t - 2 < n:
                stage_pv(t - 2, bi)
            if 0 <= t - 1 < n:
                stage_exp(t - 1, bi)
            if t < n:
                stage_qk(t, bi)
    o_ref[0] = ot_ref[...].T.astype(o_ref.dtype)


def _attention(q, kvs, tq, tu):
    b, l, hp = q.shape
    heads = hp // HEAD_PAD
    tq = min(tq, l)
    tu = min(tu, tq)
    in_specs = [pl.BlockSpec((1, tq, 2 * HEAD_PAD), lambda bi, h, i: (bi, i, h))]
    args = [q]
    lk_total = 0
    for k, vt in kvs:
        lk = k.shape[1]
        lk_total += lk
        in_specs.append(pl.BlockSpec((1, lk, 2 * HEAD_PAD), lambda bi, h, i: (bi, 0, h)))
        in_specs.append(pl.BlockSpec((1, 2 * VT_ROWS, lk), lambda bi, h, i: (bi, h, 0)))
        args += [k, vt]
    return pl.pallas_call(
        functools.partial(_attn_kernel, n_kv=len(kvs), tu=tu, kb=KB_ATTN),
        grid=(b, heads // 2, l // tq),
        in_specs=in_specs,
        out_specs=pl.BlockSpec((1, tq, 2 * QK_NOPE), lambda bi, h, i: (bi, i, h)),
        out_shape=jax.ShapeDtypeStruct((b, l, heads * QK_NOPE), BF16),
        scratch_shapes=[pltpu.VMEM((lk_total, tu), F32), pltpu.VMEM((lk_total, tu), BF16),
                        pltpu.VMEM((2, 1, tu), F32), pltpu.VMEM((VT_ROWS, tu), F32),
                        pltpu.VMEM((2 * QK_NOPE, tq), F32),
                        pltpu.VMEM((2 * HEAD_PAD, tq), BF16)],
        compiler_params=_params(("parallel", "parallel", "parallel")),
        name="attention",
    )(*args)


def _merge_kernel(x_ref, sh_ref, sc_ref, gt_ref, g_ref, ya_ref, yb_ref, yc_ref, yd_ref,
                  wg_ref, wb_ref, wo_ref, o_ref):
    x = x_ref[0]
    d = x.shape[1]
    h = _norm_mod(x, g_ref[...], sh_ref[0], sc_ref[0]).astype(BF16)
    merged = None
    for n, y_ref in enumerate((ya_ref, yb_ref, yc_ref, yd_ref)):
        gate = jax.nn.sigmoid(_dot(h, wg_ref[:, n * d:(n + 1) * d]))
        term = gate * _dot(y_ref[0], wb_ref[n])
        merged = term if merged is None else merged + term
    o_ref[0] = x + gt_ref[0] * _dot(merged.astype(BF16), wo_ref[...])


def _merge(x, mod3, row_of, g, ys, wg, wb, wo, tm):
    b, l, d = x.shape
    tm = min(tm, l)
    c = ys[0].shape[2]
    tok = lambda n: pl.BlockSpec((1, tm, n), lambda bi, i: (bi, i, 0))
    modv = lambda j: pl.BlockSpec((1, 1, d), lambda bi, i: (row_of(bi), 0, j))
    return pl.pallas_call(
        _merge_kernel,
        grid=(b, l // tm),
        in_specs=[tok(d), modv(0), modv(1), modv(2), _const_spec((1, d)), tok(c), tok(c), tok(c), tok(c),
                  _const_spec(wg.shape), _const_spec(wb.shape), _const_spec(wo.shape)],
        out_specs=tok(d),
        out_shape=jax.ShapeDtypeStruct((b, l, d), F32),
        compiler_params=_params(("parallel", "parallel")),
        name="merge",
    )(x, mod3, mod3, mod3, g.reshape(1, d), *ys, wg, wb, wo)


def _ffn_kernel(x_ref, sh_ref, sc_ref, gt_ref, g_ref, w1_ref, w3_ref, w2_ref, fg_ref, o_ref, *, final_norm):
    x = x_ref[0]
    h = _norm_mod(x, g_ref[...], sh_ref[0], sc_ref[0]).astype(BF16)
    a = _dot(h, w1_ref[...])
    u = (a * jax.nn.sigmoid(a) * _dot(h, w3_ref[...])).astype(BF16)
    y = x + gt_ref[0] * _dot(u, w2_ref[...])
    if final_norm:
        y = _rms(y, fg_ref[...])
    o_ref[0] = y


def _ffn(x, mod3, row_of, g, w1, w3, w2, final_g, final_norm, tm):
    b, l, d = x.shape
    tm = min(tm, l)
    tok = pl.BlockSpec((1, tm, d), lambda bi, i: (bi, i, 0))
    modv = lambda j: pl.BlockSpec((1, 1, d), lambda bi, i: (row_of(bi), 0, j))
    return pl.pallas_call(
        functools.partial(_ffn_kernel, final_norm=final_norm),
        grid=(b, l // tm),
        in_specs=[tok, modv(3), modv(4), modv(5), _const_spec((1, d)),
                  _const_spec(w1.shape), _const_spec(w3.shape), _const_spec(w2.shape), _const_spec((1, d))],
        out_specs=tok,
        out_shape=jax.ShapeDtypeStruct((b, l, d), F32),
        compiler_params=_params(("parallel", "parallel")),
        name="ffn",
    )(x, mod3, mod3, mod3, g.reshape(1, d), w1, w3, w2, final_g.reshape(1, d))


def _rotate_half_cols(w):
    q = QK_ROPE // 4
    return jnp.concatenate([-w[:, q:2 * q], w[:, :q], -w[:, 3 * q:], w[:, 2 * q:3 * q]], axis=1)


def _block_diag(w):
    h, hd, _ = w.shape
    eye = jnp.eye(h, dtype=w.dtype)
    return (eye[:, None, :, None] * w[:, :, None, :]).reshape(h * hd, h * hd)


def _layer_weights(w_in_l, lru_w_a, lru_w_x, lru_b_a, lru_b_x, lru_lam, w_q_up, w_kv_up, cmlp_w_s,
                   w_branch, w_out, w_ff1, w_ff3, w_ff2):
    d = w_in_l.shape[0]
    c = d // 2
    o = {}
    i_lru_x, i_kv, i_kr, i_lru_g, i_q = 0, c, c + KV_LORA, c + KV_LORA + QK_ROPE, 2 * c + KV_LORA + QK_ROPE
    i_ab = i_q + Q_LORA
    i_ac, i_ax, i_cu, i_cv, i_gate = i_ab + c, i_ab + 2 * c, i_ab + 3 * c, i_ab + 4 * c, i_ab + 5 * c
    col = lambda s, n: w_in_l[:, s:s + n]
    k_rope = col(i_kr, QK_ROPE)
    mla_pad = jnp.zeros((d, HEAD_PAD - 2 * QK_ROPE), F32)
    o["w_lru"] = jnp.concatenate([col(i_lru_x, c), col(i_lru_g, c)], axis=1).astype(BF16)
    o["w_mla"] = jnp.concatenate([col(i_kv, KV_LORA), col(i_q, Q_LORA), k_rope, _rotate_half_cols(k_rope),
                                  mla_pad], axis=1).astype(BF16)
    o["w_conv"] = w_in_l[:, i_ab:i_cu].astype(BF16)
    o["w_cmlp"] = w_in_l[:, i_cu:i_gate].astype(BF16)
    o["w_gate"] = w_in_l[:, i_gate:].astype(BF16)
    o["lru_wg"] = jnp.stack([jnp.concatenate([_block_diag(lru_w_a[dd]), _block_diag(lru_w_x[dd])], axis=1)
                             for dd in range(2)]).astype(BF16)
    o["lru_bg"] = jnp.concatenate([lru_b_a, lru_b_x], axis=1).reshape(2, 1, 2 * c)
    o["lru_lam"] = lru_lam.reshape(2, 1, c)
    hq = QK_NOPE + QK_ROPE
    wq = w_q_up.reshape(Q_LORA, MLA_HEADS, hq)
    zq = jnp.zeros((Q_LORA, MLA_HEADS, HEAD_PAD - hq), F32)
    o["wq"] = jnp.concatenate([wq, zq], axis=2).reshape(Q_LORA, MLA_HEADS * HEAD_PAD).astype(BF16)
    rot = jnp.stack([_rotate_half_cols(wq[:, h, QK_NOPE:]) for h in range(MLA_HEADS)], axis=1)
    o["wqr"] = jnp.concatenate([jnp.zeros((Q_LORA, MLA_HEADS, QK_NOPE), F32), rot, zq],
                               axis=2).reshape(Q_LORA, MLA_HEADS * HEAD_PAD).astype(BF16)
    wkv = w_kv_up.reshape(KV_LORA, MLA_HEADS, 2 * QK_NOPE)
    o["wk"] = jnp.concatenate([wkv[:, :, :QK_NOPE], jnp.zeros((KV_LORA, MLA_HEADS, HEAD_PAD - QK_NOPE), F32)],
                              axis=2).reshape(KV_LORA, MLA_HEADS * HEAD_PAD).astype(BF16)
    wv = jnp.concatenate([wkv[:, :, QK_NOPE:], jnp.zeros((KV_LORA, MLA_HEADS, ONES_ROWS), F32)], axis=2)
    o["wvt"] = wv.reshape(KV_LORA, MLA_HEADS * VT_ROWS).T.astype(BF16)
    o["w_s"] = cmlp_w_s.astype(BF16)
    o["w_branch"] = w_branch.astype(BF16)
    o["w_out"] = w_out.astype(BF16)
    o["w_ff1"] = w_ff1.astype(BF16)
    o["w_ff3"] = w_ff3.astype(BF16)
    o["w_ff2"] = w_ff2.astype(BF16)
    return o


def _rope_place():
    eye = jnp.eye(QK_ROPE, dtype=F32)
    blk = jnp.zeros((HEAD_PAD, HEAD_PAD), F32)
    blk = blk.at[:QK_ROPE, QK_NOPE:QK_NOPE + QK_ROPE].set(eye)
    blk = blk.at[QK_ROPE:2 * QK_ROPE, QK_NOPE:QK_NOPE + QK_ROPE].set(eye)
    return jnp.concatenate([blk] * MLA_HEADS, axis=1).astype(BF16)


def _rope_tables(n_lat):
    t = jnp.arange(n_lat)
    n_freq = QK_ROPE // 4
    inv = ROPE_THETA ** (-jnp.arange(n_freq, dtype=F32) / n_freq)
    ang_r = (t // GRID_W).astype(F32)[:, None] * inv
    ang_c = (t % GRID_W).astype(F32)[:, None] * inv
    cos = jnp.concatenate([jnp.cos(ang_r)] * 2 + [jnp.cos(ang_c)] * 2, axis=1)
    sin = jnp.concatenate([jnp.sin(ang_r)] * 2 + [jnp.sin(ang_c)] * 2, axis=1)
    ones = jnp.ones((n_lat, QK_NOPE), F32)
    z32 = jnp.zeros((n_lat, HEAD_PAD - QK_NOPE - QK_ROPE), F32)
    q_cos = jnp.concatenate([ones, cos, z32], axis=1)
    q_sin = jnp.concatenate([jnp.zeros((n_lat, QK_NOPE), F32), sin, z32], axis=1)
    k_cs = jnp.concatenate([cos, sin, jnp.zeros((n_lat, HEAD_PAD - 2 * QK_ROPE), F32)], axis=1)
    return q_cos, q_sin, k_cs


def _identity_tables(n):
    q_cos = jnp.concatenate([jnp.ones((n, QK_NOPE + QK_ROPE), F32),
                             jnp.zeros((n, HEAD_PAD - QK_NOPE - QK_ROPE), F32)], axis=1)
    q_sin = jnp.zeros((n, HEAD_PAD), F32)
    k_cs = jnp.concatenate([jnp.ones((n, QK_ROPE), F32), jnp.zeros((n, HEAD_PAD - QK_ROPE), F32)], axis=1)
    return q_cos, q_sin, k_cs


TM_PROJ = 512
TM_MIX = 512
TQ_ATTN = 2048
TU_ATTN = 512
KB_ATTN = 1024


def kernel(x, c, ctx, c_ctx, w_mod, b_mod, norm1_g, norm2_g, w_in, conv_a_w, lru_conv_w, lru_conv_b,
           lru_w_a, lru_b_a, lru_w_x, lru_b_x, lru_lam, cmlp_ln_g, cmlp_ln_b, cmlp_w_s, cmlp_b_s,
           mla_q_norm_g, mla_kv_norm_g, mla_w_q_up, mla_w_kv_up, w_branch, w_out, w_ff1, w_ff3, w_ff2,
           final_norm_g):
    bsz, n_lat, d = x.shape
    n_ctx = ctx.shape[1]
    depth = w_in.shape[0]
    cw = d // 2
    q_scale = (QK_NOPE + QK_ROPE) ** -0.5 * LOG2E
    row_in_head = jnp.arange(MLA_HEADS * VT_ROWS) % VT_ROWS
    vone = (row_in_head >= QK_NOPE).astype(F32)[:, None]

    rows = -(-(bsz + 1) // SUBLANES) * SUBLANES
    s_rows = jnp.concatenate([c, c_ctx[None, :], jnp.zeros((rows - bsz - 1, d), F32)], axis=0)
    mod = _modulation(s_rows, w_mod, b_mod)
    lat_row = lambda bi: bi
    ctx_row = lambda bi: bsz

    place = _rope_place()
    lat_tabs = _rope_tables(n_lat)
    ctx_tabs = _identity_tables(n_ctx)
    zero_state = jnp.zeros((bsz, 1, cw), F32)

    xc = ctx
    for l in range(depth):
        last = l == depth - 1
        w = _layer_weights(w_in[l], lru_w_a[l], lru_w_x[l], lru_b_a[l], lru_b_x[l], lru_lam[l],
                           mla_w_q_up[l], mla_w_kv_up[l], cmlp_w_s[l], w_branch[l], w_out[l],
                           w_ff1[l], w_ff3[l], w_ff2[l])
        mod3 = mod[l].reshape(rows, 1, 6 * d)
        cmlp = (cmlp_ln_g[l], cmlp_ln_b[l], cmlp_b_s[l])
        mla = (mla_q_norm_g[l], mla_kv_norm_g[l], vone, place)

        zc_lru, zc_conv, yc_c, qc, kc, vc = _inproj(xc, mod3, ctx_row, norm1_g[l], w, cmlp, mla, ctx_tabs,
                                                    q_scale, TM_PROJ)
        yc_b, hc_f, hc_b = _lru_mixer(zc_lru, zero_state, zero_state, lru_conv_w[l], lru_conv_b[l],
                                      w["lru_wg"], w["lru_bg"], w["lru_lam"], TM_MIX)

        z_lru, z_conv, y_c, q, k, v = _inproj(x, mod3, lat_row, norm1_g[l], w, cmlp, mla, lat_tabs,
                                              q_scale, TM_PROJ)
        y_b, _, _ = _lru_mixer(z_lru, hc_f, hc_b, lru_conv_w[l], lru_conv_b[l],
                               w["lru_wg"], w["lru_bg"], w["lru_lam"], TM_MIX)
        y_d = _attention(q, [(k, v), (kc, vc)], TQ_ATTN, TU_ATTN)
        y_a = _conv_mixer(z_conv, conv_a_w[l], TM_MIX)
        x = _merge(x, mod3, lat_row, norm1_g[l], (y_a, y_b, y_c, y_d), w["w_gate"], w["w_branch"], w["w_out"],
                   TM_PROJ)
        x = _ffn(x, mod3, lat_row, norm2_g[l], w["w_ff1"], w["w_ff3"], w["w_ff2"], final_norm_g, last, TM_PROJ)

        if not last:
            yc_d = _attention(qc, [(kc, vc)], TQ_ATTN, TU_ATTN)
            yc_a = _conv_mixer(zc_conv, conv_a_w[l], TM_MIX)
            xc = _merge(xc, mod3, ctx_row, norm1_g[l], (yc_a, yc_b, yc_c, yc_d), w["w_gate"], w["w_branch"],
                        w["w_out"], TM_PROJ)
            xc = _ffn(xc, mod3, ctx_row, norm2_g[l], w["w_ff1"], w["w_ff3"], w["w_ff2"], final_norm_g, False,
                      TM_PROJ)
    return x
```

```python
import functools

import jax
import jax.numpy as jnp
from jax import lax
from jax.experimental import pallas as pl
from jax.experimental.pallas import tpu as pltpu

F32 = jnp.float32
BF16 = jnp.bfloat16

EPS = 1e-6
GRID_W = 64
N_BRANCH = 4
LRU_HEADS = 8
LRU_C = 8.0
CMLP_GROUPS = 4
CHUNK = 128
MLA_HEADS = 8
QK_NOPE = 64
QK_ROPE = 32
Q_LORA = 384
KV_LORA = 256
ROPE_THETA = 10000.0
HEAD_PAD = 128
ONES_ROWS = 16
VT_ROWS = QK_NOPE + ONES_ROWS
LOG2E = 1.4426950408889634
SUBLANES = 8
VMEM_LIMIT = 56 * 1024 * 1024


def _params(sem, vmem=VMEM_LIMIT, flags=None):
    return pltpu.CompilerParams(dimension_semantics=sem, vmem_limit_bytes=vmem, flags=flags)


def _const_spec(shape):
    zeros = (0,) * len(shape)
    return pl.BlockSpec(shape, lambda *_: zeros, pipeline_mode=pl.Buffered(1))


def _gelu(x):
    return jax.nn.gelu(x)


def _norm_mod(x, g, shift, scale):
    y = x * lax.rsqrt(jnp.mean(x * x, axis=-1, keepdims=True) + EPS) * g
    return y * (1.0 + scale) + shift


def _dot(a, b):
    return jnp.dot(a, b, preferred_element_type=F32)


def _mod_kernel(s_ref, w_ref, b_ref, o_ref):
    s = s_ref[...]
    s = s * jax.nn.sigmoid(s)
    o_ref[0] = _dot(s.astype(BF16), w_ref[0].astype(BF16)) + b_ref[0]


def _modulation(s_rows, w_mod, b_mod):
    n_layer, d, d6 = w_mod.shape
    r = s_rows.shape[0]
    return pl.pallas_call(
        _mod_kernel,
        grid=(n_layer, d6 // d),
        in_specs=[
            pl.BlockSpec((r, d), lambda l, j: (0, 0)),
            pl.BlockSpec((1, d, d), lambda l, j: (l, 0, j)),
            pl.BlockSpec((1, 1, d), lambda l, j: (l, 0, j)),
        ],
        out_specs=pl.BlockSpec((1, r, d), lambda l, j: (l, 0, j)),
        out_shape=jax.ShapeDtypeStruct((n_layer, r, d6), F32),
        compiler_params=_params(("parallel", "parallel")),
        name="modulation",
    )(s_rows, w_mod, b_mod.reshape(n_layer, 1, d6))


def _inproj_kernel(x_ref, sh_ref, sc_ref, g_ref, w_lru, w_conv, w_cmlp, w_mla,
                   lng_ref, lnb_ref, ws_ref, bs_ref,
                   qg_ref, kvg_ref, wq_ref, wk_ref, wvt_ref, vone_ref, qc_ref, qs_ref, kcs_ref,
                   o_lru, o_conv, o_yc, q_ref, k_ref, vt_ref, *, q_scale):
    h = _norm_mod(x_ref[0], g_ref[...], sh_ref[0], sc_ref[0]).astype(BF16)
    o_lru[0] = _dot(h, w_lru[...])
    zc = _dot(h, w_conv[...])
    c = zc.shape[1] // 3
    o_conv[0, :, :c] = zc[:, :c]
    o_conv[0, :, c:] = zc[:, c:2 * c] * zc[:, 2 * c:]
    _cmlp_body(_dot(h, w_cmlp[...]), lng_ref, lnb_ref, ws_ref, bs_ref, o_yc)
    _mla_body(_dot(h, w_mla[...]), qg_ref, kvg_ref, wq_ref, wk_ref, wvt_ref, vone_ref,
              qc_ref, qs_ref, kcs_ref, q_ref, k_ref, vt_ref, q_scale)


def _inproj(x, mod3, row_of, g, w, cmlp, mla, tabs, q_scale, tm):
    b, l, d = x.shape
    tm = min(tm, l)
    c = d // 2
    ln_g, ln_b, b_s = cmlp
    q_g, kv_g, vone = mla
    hp = w["wq"].shape[1]
    vrows = w["wvt"].shape[0]
    consts = [g.reshape(1, d), w["w_lru"], w["w_conv"], w["w_cmlp"], w["w_mla"],
              ln_g.reshape(1, c), ln_b.reshape(1, c), w["w_s"], b_s.T,
              q_g.reshape(1, Q_LORA), kv_g.reshape(1, KV_LORA), w["wq"], w["wk"], w["wvt"], vone]
    tab = pl.BlockSpec((tm, HEAD_PAD), lambda bi, i: (i, 0))
    tok = lambda n: pl.BlockSpec((1, tm, n), lambda bi, i: (bi, i, 0))
    in_specs = ([tok(d),
                 pl.BlockSpec((1, 1, d), lambda bi, i: (row_of(bi), 0, 0)),
                 pl.BlockSpec((1, 1, d), lambda bi, i: (row_of(bi), 0, 1))]
                + [_const_spec(a.shape) for a in consts] + [tab, tab, tab])
    out_specs = [tok(2 * c), tok(2 * c), tok(c), tok(hp), tok(hp),
                 pl.BlockSpec((1, vrows, tm), lambda bi, i: (bi, 0, i))]
    out_shape = [jax.ShapeDtypeStruct((b, l, 2 * c), F32), jax.ShapeDtypeStruct((b, l, 2 * c), F32),
                 jax.ShapeDtypeStruct((b, l, c), BF16), jax.ShapeDtypeStruct((b, l, hp), BF16),
                 jax.ShapeDtypeStruct((b, l, hp), BF16), jax.ShapeDtypeStruct((b, vrows, l), BF16)]
    return pl.pallas_call(
        functools.partial(_inproj_kernel, q_scale=q_scale),
        grid=(b, l // tm),
        in_specs=in_specs,
        out_specs=out_specs,
        out_shape=out_shape,
        compiler_params=_params(("parallel", "parallel")),
        name="inproj",
    )(x, mod3, mod3, *consts, *tabs)


def _halo_specs(tm, l, c, lane_blk):
    nb = l // SUBLANES
    per = tm // SUBLANES
    prev = pl.BlockSpec((1, SUBLANES, c), lambda bi, i: (bi, jnp.maximum(i * per - 1, 0), lane_blk))
    nxt = pl.BlockSpec((1, SUBLANES, c), lambda bi, i: (bi, jnp.minimum((i + 1) * per, nb - 1), lane_blk))
    return prev, nxt


def _shift_down(p, first_row, k):
    row = lax.broadcasted_iota(jnp.int32, p.shape, 0)
    out = pltpu.roll(p, k, 0)
    for j in range(k):
        out = jnp.where(row == j, first_row[j:j + 1, :], out)
    return out


def _shift_up(p, last_row):
    tm = p.shape[0]
    row = lax.broadcasted_iota(jnp.int32, p.shape, 0)
    return jnp.where(row == tm - 1, last_row, pltpu.roll(p, tm - 1, 0))


def _conv_kernel(ab_ref, p_ref, pp_ref, pn_ref, w_ref, o_ref, *, n_tiles):
    i = pl.program_id(1)
    p = p_ref[0]
    p_prev = jnp.where(i > 0, pp_ref[0, SUBLANES - 1:, :], 0.0)
    p_next = jnp.where(i < n_tiles - 1, pn_ref[0, 0:1, :], 0.0)
    w = w_ref[...]
    conv = w[0:1] * _shift_down(p, p_prev, 1) + w[1:2] * p + w[2:3] * _shift_up(p, p_next)
    o_ref[0] = (ab_ref[0] * conv).astype(o_ref.dtype)


def _conv_mixer(z_conv, w_conv, tm):
    b, l, c2 = z_conv.shape
    c = c2 // 2
    tm = min(tm, l)
    main = lambda blk: pl.BlockSpec((1, tm, c), lambda bi, i: (bi, i, blk))
    prev, nxt = _halo_specs(tm, l, c, 1)
    return pl.pallas_call(
        functools.partial(_conv_kernel, n_tiles=l // tm),
        grid=(b, l // tm),
        in_specs=[main(0), main(1), prev, nxt, _const_spec(w_conv.shape)],
        out_specs=pl.BlockSpec((1, tm, c), lambda bi, i: (bi, i, 0)),
        out_shape=jax.ShapeDtypeStruct((b, l, c), BF16),
        compiler_params=_params(("parallel", "parallel")),
        name="conv_mixer",
    )(z_conv, z_conv, z_conv, z_conv, w_conv)


def _lru_conv(x_ref, xp_ref, xn_ref, cw_ref, cb_ref, i, n_tiles):
    x = x_ref[0]
    prev = jnp.where(i > 0, xp_ref[0, SUBLANES - 2:, :], 0.0)
    nxt = jnp.where(i < n_tiles - 1, xn_ref[0, 0:1, :], 0.0)
    cw = cw_ref[...]
    x_m1 = _shift_down(x, prev[1:2], 1)
    x_m2 = _shift_down(x, prev, 2)
    return cw[0:1] * x_m2 + cw[1:2] * x_m1 + cw[2:3] * x + cw[3:4] * _shift_up(x, nxt) + cb_ref[...]


def _lru_coeffs(xl, wg_ref, bg_ref, lam_ref):
    c = xl.shape[1]
    gates = _dot(xl.astype(BF16), wg_ref[0]) + bg_ref[0]
    r = jax.nn.sigmoid(gates[:, :c])
    gi = jax.nn.sigmoid(gates[:, c:])
    lam = lam_ref[0]
    softplus_neg = jnp.maximum(-lam, 0.0) + jnp.log1p(jnp.exp(-jnp.abs(lam)))
    a = jnp.exp2((-LRU_C * LOG2E * softplus_neg) * r)
    y = 1.0 - a * a
    root = jnp.where(y > 0.0, y * lax.rsqrt(y), 0.0)
    return a, root * (gi * xl)


def _scan_tile(a, bb, a_scr, b_scr, h_scr, carry_ref, reverse):
    tm, c = a.shape
    groups = tm // SUBLANES
    acc_a = a.reshape(groups, SUBLANES, c)
    acc_b = bb.reshape(groups, SUBLANES, c)
    r8 = lax.broadcasted_iota(jnp.int32, acc_a.shape, 1)
    for k in (1, 2, 4):
        if reverse:
            keep = r8 < SUBLANES - k
            shift = SUBLANES - k
        else:
            keep = r8 >= k
            shift = k
        a_sh = jnp.where(keep, pltpu.roll(acc_a, shift, 1), 1.0)
        b_sh = jnp.where(keep, pltpu.roll(acc_b, shift, 1), 0.0)
        acc_b = acc_a * b_sh + acc_b
        acc_a = acc_a * a_sh
    a_scr[...] = acc_a.reshape(tm, c)
    b_scr[...] = acc_b.reshape(tm, c)

    def body(gi, h_in):
        g = groups - 1 - gi if reverse else gi
        rows = pl.ds(pl.multiple_of(g * SUBLANES, SUBLANES), SUBLANES)
        h8 = b_scr[rows, :] + a_scr[rows, :] * h_in
        h_scr[rows, :] = h8
        return h8[0:1, :] if reverse else h8[SUBLANES - 1:, :]

    carry_ref[...] = lax.fori_loop(0, groups, body, carry_ref[...], unroll=4)


def _lru_fwd_kernel(x_ref, xp_ref, xn_ref, h0_ref, cw_ref, cb_ref, wg_ref, bg_ref, lam_ref,
                    h_ref, xl_ref, hlast_ref, a_scr, b_scr, carry, *, n_tiles):
    i = pl.program_id(1)

    @pl.when(i == 0)
    def _():
        carry[...] = h0_ref[0]

    xl = _lru_conv(x_ref, xp_ref, xn_ref, cw_ref, cb_ref, i, n_tiles)
    xl_ref[0] = xl
    a, bb = _lru_coeffs(xl, wg_ref, bg_ref, lam_ref)
    _scan_tile(a, bb, a_scr, b_scr, h_ref.at[0], carry, reverse=False)
    hlast_ref[0] = carry[...]


def _lru_bwd_kernel(xl_ref, h0_ref, hf_ref, g_ref, wg_ref, bg_ref, lam_ref,
                    y_ref, hfirst_ref, a_scr, b_scr, h_scr, carry):
    i = pl.program_id(1)

    @pl.when(i == 0)
    def _():
        carry[...] = h0_ref[0]

    a, bb = _lru_coeffs(xl_ref[0], wg_ref, bg_ref, lam_ref)
    _scan_tile(a, bb, a_scr, b_scr, h_scr, carry, reverse=True)
    hfirst_ref[0] = carry[...]
    y_ref[0] = (_gelu(g_ref[0]) * (hf_ref[0] + h_scr[...])).astype(y_ref.dtype)


def _lru_mixer(z_lru, h0_f, h0_b, conv_w, conv_b, wg, bg, lam, tm):
    b, l, c2 = z_lru.shape
    c = c2 // 2
    tm = min(tm, l)
    nt = l // tm
    state = jax.ShapeDtypeStruct((b, 1, c), F32)
    state_spec = pl.BlockSpec((1, 1, c), lambda bi, i: (bi, 0, 0))
    cp = _params(("parallel", "arbitrary"))
    gate = lambda d: [pl.BlockSpec((1, c, c2), lambda bi, i: (d, 0, 0)),
                      pl.BlockSpec((1, 1, c2), lambda bi, i: (d, 0, 0)),
                      pl.BlockSpec((1, 1, c), lambda bi, i: (d, 0, 0))]
    tile_f = pl.BlockSpec((1, tm, c), lambda bi, i: (bi, i, 0))

    prev_f, next_f = _halo_specs(tm, l, c, 0)
    hf, xl, hf_last = pl.pallas_call(
        functools.partial(_lru_fwd_kernel, n_tiles=nt),
        grid=(b, nt),
        in_specs=[tile_f, prev_f, next_f, state_spec, _const_spec(conv_w.shape), _const_spec((1, c))] + gate(0),
        out_specs=[tile_f, tile_f, state_spec],
        out_shape=[jax.ShapeDtypeStruct((b, l, c), F32), jax.ShapeDtypeStruct((b, l, c), F32), state],
        scratch_shapes=[pltpu.VMEM((tm, c), F32), pltpu.VMEM((tm, c), F32), pltpu.VMEM((1, c), F32)],
        compiler_params=cp,
        name="lru_fwd",
    )(z_lru, z_lru, z_lru, h0_f, conv_w, conv_b.reshape(1, c), wg, bg, lam)

    tile_b = lambda blk: pl.BlockSpec((1, tm, c), lambda bi, i: (bi, nt - 1 - i, blk))
    y, hb_first = pl.pallas_call(
        _lru_bwd_kernel,
        grid=(b, nt),
        in_specs=[tile_b(0), state_spec, tile_b(0), tile_b(1)] + gate(1),
        out_specs=[tile_b(0), state_spec],
        out_shape=[jax.ShapeDtypeStruct((b, l, c), BF16), state],
        scratch_shapes=[pltpu.VMEM((tm, c), F32), pltpu.VMEM((tm, c), F32), pltpu.VMEM((tm, c), F32),
                        pltpu.VMEM((1, c), F32)],
        compiler_params=cp,
        name="lru_bwd",
    )(xl, h0_b, hf, z_lru, wg, bg, lam)
    return y, hf_last, hb_first


def _cmlp_body(z, g_ref, b_ref, ws_ref, bs_ref, o_ref):
    tm, c = z.shape[0], z.shape[1] // 2
    gd = c // CMLP_GROUPS
    u = _gelu(z[:, :c])
    v = _gelu(z[:, c:])
    mu = jnp.mean(v, axis=-1, keepdims=True)
    var = jnp.mean(jnp.square(v - mu), axis=-1, keepdims=True)
    v = ((v - mu) * lax.rsqrt(var + EPS) * g_ref[...] + b_ref[...]).astype(BF16)
    bs = bs_ref[...]
    for ck in range(tm // CHUNK):
        rows = slice(ck * CHUNK, (ck + 1) * CHUNK)
        for g in range(CMLP_GROUPS):
            cols = slice(g * gd, (g + 1) * gd)
            mixed = _dot(ws_ref[g], v[rows, cols]) + bs[:, g:g + 1]
            o_ref[0, rows, cols] = (u[rows, cols] * mixed).astype(o_ref.dtype)


def _rms(x, g):
    return x * lax.rsqrt(jnp.mean(x * x, axis=-1, keepdims=True) + EPS) * g


def _mla_body(z, qg_ref, kvg_ref, wq_ref, wk_ref, wvt_ref, vone_ref,
              qc_ref, qs_ref, kcs_ref, q_ref, k_ref, vt_ref, q_scale):
    heads = q_ref.shape[2] // HEAD_PAD
    kvn = _rms(z[:, :KV_LORA], kvg_ref[...]).astype(BF16)
    qn = _rms(z[:, KV_LORA:KV_LORA + Q_LORA], qg_ref[...]).astype(BF16)
    t = z[:, KV_LORA + Q_LORA:] * kcs_ref[...]
    t = t + pltpu.roll(t, HEAD_PAD - QK_ROPE, 1)
    lane = lax.broadcasted_iota(jnp.int32, t.shape, 1)
    rope = jnp.where((lane >= QK_NOPE) & (lane < QK_NOPE + QK_ROPE), pltpu.roll(t, QK_NOPE, 1), 0.0)
    k = _dot(kvn, wk_ref[...]) + jnp.concatenate([rope] * heads, axis=1)
    k_ref[0] = k.astype(k_ref.dtype)
    v_t = lax.dot_general(wvt_ref[...], kvn, (((1,), (1,)), ((), ())), preferred_element_type=F32)
    vt_ref[0] = (v_t + vone_ref[...]).astype(vt_ref.dtype)
    qa = _dot(qn, wq_ref[...])
    qc = jnp.concatenate([qc_ref[...]] * heads, axis=1)
    qs = jnp.concatenate([qs_ref[...]] * heads, axis=1)
    q = qa * qc + pltpu.roll(qa, qa.shape[1] - QK_ROPE, 1) * qs
    q_ref[0] = (q * q_scale).astype(q_ref.dtype)


def _attn_kernel(q_ref, *refs, n_kv, tu, kb):
    kv_refs, o_ref = refs[:2 * n_kv], refs[2 * n_kv]
    s_buf, p_buf, mx_buf, acc_ref, ot_ref, qt_ref = refs[2 * n_kv + 1:]
    hv = o_ref.shape[2] // 2
    hr = kv_refs[1].shape[1] // 2
    tq = q_ref.shape[1]
    blocks = []
    off = 0
    for j in range(n_kv):
        k_ref, vt_ref = kv_refs[2 * j], kv_refs[2 * j + 1]
        lk = k_ref.shape[1]
        for s0 in range(0, lk, kb):
            blocks.append((k_ref, vt_ref, s0, off + s0, min(kb, lk - s0)))
        off += lk
    units = [(qi, hh) for qi in range(tq // tu) for hh in range(2)]
    n = len(units)

    qt_ref[...] = q_ref[0].T

    def stage_qk(u, bi):
        qi, hh = units[u]
        k_ref, _, s0, g0, size = blocks[bi]
        lanes = slice(hh * HEAD_PAD, (hh + 1) * HEAD_PAD)
        s = _dot(k_ref[0, s0:s0 + size, lanes], qt_ref[lanes, qi * tu:(qi + 1) * tu])
        s_buf[g0:g0 + size, :] = s
        m = jnp.max(s, axis=0, keepdims=True)
        mx_buf[u % 2] = m if bi == 0 else jnp.maximum(mx_buf[u % 2], m)

    def stage_exp(u, bi):
        _, _, _, g0, size = blocks[bi]
        p_buf[g0:g0 + size, :] = jnp.exp2(s_buf[g0:g0 + size, :] - mx_buf[u % 2]).astype(BF16)

    def stage_pv(u, bi):
        qi, hh = units[u]
        _, vt_ref, s0, g0, size = blocks[bi]
        pv = _dot(vt_ref[0, hh * hr:(hh + 1) * hr, s0:s0 + size], p_buf[g0:g0 + size, :])
        if bi > 0:
            pv = pv + acc_ref[...]
        if bi < len(blocks) - 1:
            acc_ref[...] = pv
        else:
            ot_ref[hh * hv:(hh + 1) * hv, qi * tu:(qi + 1) * tu] = pv[:hv] * (1.0 / pv[hv:hv + 1])

    for t in range(n + 2):
        for bi in range(len(blocks)):
            if 0 <= t - 2 < n:
                stage_pv(t - 2, bi)
            if 0 <= t - 1 < n:
                stage_exp(t - 1, bi)
            if t < n:
                stage_qk(t, bi)
    o_ref[0] = ot_ref[...].T.astype(o_ref.dtype)


def _attention(q, kvs, tq, tu):
    b, l, hp = q.shape
    heads = hp // HEAD_PAD
    tq = min(tq, l)
    tu = min(tu, tq)
    in_specs = [pl.BlockSpec((1, tq, 2 * HEAD_PAD), lambda bi, h, i: (bi, i, h))]
    args = [q]
    lk_total = 0
    for k, vt in kvs:
        lk = k.shape[1]
        lk_total += lk
        in_specs.append(pl.BlockSpec((1, lk, 2 * HEAD_PAD), lambda bi, h, i: (bi, 0, h)))
        in_specs.append(pl.BlockSpec((1, 2 * VT_ROWS, lk), lambda bi, h, i: (bi, h, 0)))
        args += [k, vt]
    return pl.pallas_call(
        functools.partial(_attn_kernel, n_kv=len(kvs), tu=tu, kb=KB_ATTN),
        grid=(b, heads // 2, l // tq),
        in_specs=in_specs,
        out_specs=pl.BlockSpec((1, tq, 2 * QK_NOPE), lambda bi, h, i: (bi, i, h)),
        out_shape=jax.ShapeDtypeStruct((b, l, heads * QK_NOPE), BF16),
        scratch_shapes=[pltpu.VMEM((lk_total, tu), F32), pltpu.VMEM((lk_total, tu), BF16),
                        pltpu.VMEM((2, 1, tu), F32), pltpu.VMEM((VT_ROWS, tu), F32),
                        pltpu.VMEM((2 * QK_NOPE, tq), F32),
                        pltpu.VMEM((2 * HEAD_PAD, tq), BF16)],
        compiler_params=_params(("parallel", "parallel", "parallel")),
        name="attention",
    )(*args)


def _merge_kernel(x_ref, sh_ref, sc_ref, gt_ref, g_ref, ya_ref, yb_ref, yc_ref, yd_ref,
                  wg_ref, wb_ref, wo_ref, o_ref):
    x = x_ref[0]
    d = x.shape[1]
    h = _norm_mod(x, g_ref[...], sh_ref[0], sc_ref[0]).astype(BF16)
    merged = None
    for n, y_ref in enumerate((ya_ref, yb_ref, yc_ref, yd_ref)):
        gate = jax.nn.sigmoid(_dot(h, wg_ref[:, n * d:(n + 1) * d]))
        term = gate * _dot(y_ref[0], wb_ref[n])
        merged = term if merged is None else merged + term
    o_ref[0] = x + gt_ref[0] * _dot(merged.astype(BF16), wo_ref[...])


def _merge(x, mod3, row_of, g, ys, wg, wb, wo, tm):
    b, l, d = x.shape
    tm = min(tm, l)
    c = ys[0].shape[2]
    tok = lambda n: pl.BlockSpec((1, tm, n), lambda bi, i: (bi, i, 0))
    modv = lambda j: pl.BlockSpec((1, 1, d), lambda bi, i: (row_of(bi), 0, j))
    return pl.pallas_call(
        _merge_kernel,
        grid=(b, l // tm),
        in_specs=[tok(d), modv(0), modv(1), modv(2), _const_spec((1, d)), tok(c), tok(c), tok(c), tok(c),
                  _const_spec(wg.shape), _const_spec(wb.shape), _const_spec(wo.shape)],
        out_specs=tok(d),
        out_shape=jax.ShapeDtypeStruct((b, l, d), F32),
        compiler_params=_params(("parallel", "parallel")),
        name="merge",
    )(x, mod3, mod3, mod3, g.reshape(1, d), *ys, wg, wb, wo)


def _ffn_kernel(x_ref, sh_ref, sc_ref, gt_ref, g_ref, w1_ref, w3_ref, w2_ref, fg_ref, o_ref, *, final_norm):
    x = x_ref[0]
    h = _norm_mod(x, g_ref[...], sh_ref[0], sc_ref[0]).astype(BF16)
    a = _dot(h, w1_ref[...])
    u = (a * jax.nn.sigmoid(a) * _dot(h, w3_ref[...])).astype(BF16)
    y = x + gt_ref[0] * _dot(u, w2_ref[...])
    if final_norm:
        y = _rms(y, fg_ref[...])
    o_ref[0] = y


def _ffn(x, mod3, row_of, g, w1, w3, w2, final_g, final_norm, tm):
    b, l, d = x.shape
    tm = min(tm, l)
    tok = pl.BlockSpec((1, tm, d), lambda bi, i: (bi, i, 0))
    modv = lambda j: pl.BlockSpec((1, 1, d), lambda bi, i: (row_of(bi), 0, j))
    return pl.pallas_call(
        functools.partial(_ffn_kernel, final_norm=final_norm),
        grid=(b, l // tm),
        in_specs=[tok, modv(3), modv(4), modv(5), _const_spec((1, d)),
                  _const_spec(w1.shape), _const_spec(w3.shape), _const_spec(w2.shape), _const_spec((1, d))],
        out_specs=tok,
        out_shape=jax.ShapeDtypeStruct((b, l, d), F32),
        compiler_params=_params(("parallel", "parallel")),
        name="ffn",
    )(x, mod3, mod3, mod3, g.reshape(1, d), w1, w3, w2, final_g.reshape(1, d))


def _rotate_half_cols(w):
    q = QK_ROPE // 4
    return jnp.concatenate([-w[:, q:2 * q], w[:, :q], -w[:, 3 * q:], w[:, 2 * q:3 * q]], axis=1)


def _block_diag(w):
    h, hd, _ = w.shape
    eye = jnp.eye(h, dtype=w.dtype)
    return (eye[:, None, :, None] * w[:, :, None, :]).reshape(h * hd, h * hd)


def _layer_weights(w_in_l, lru_w_a, lru_w_x, lru_b_a, lru_b_x, lru_lam, w_q_up, w_kv_up, cmlp_w_s,
                   w_branch, w_out, w_ff1, w_ff3, w_ff2):
    d = w_in_l.shape[0]
    c = d // 2
    o = {}
    i_lru_x, i_kv, i_kr, i_lru_g, i_q = 0, c, c + KV_LORA, c + KV_LORA + QK_ROPE, 2 * c + KV_LORA + QK_ROPE
    i_ab = i_q + Q_LORA
    i_ac, i_ax, i_cu, i_cv, i_gate = i_ab + c, i_ab + 2 * c, i_ab + 3 * c, i_ab + 4 * c, i_ab + 5 * c
    col = lambda s, n: w_in_l[:, s:s + n]
    k_rope = col(i_kr, QK_ROPE)
    mla_pad = jnp.zeros((d, HEAD_PAD - 2 * QK_ROPE), F32)
    o["w_lru"] = jnp.concatenate([col(i_lru_x, c), col(i_lru_g, c)], axis=1).astype(BF16)
    o["w_mla"] = jnp.concatenate([col(i_kv, KV_LORA), col(i_q, Q_LORA), k_rope, _rotate_half_cols(k_rope),
                                  mla_pad], axis=1).astype(BF16)
    o["w_conv"] = w_in_l[:, i_ab:i_cu].astype(BF16)
    o["w_cmlp"] = w_in_l[:, i_cu:i_gate].astype(BF16)
    o["w_gate"] = w_in_l[:, i_gate:].astype(BF16)
    o["lru_wg"] = jnp.stack([jnp.concatenate([_block_diag(lru_w_a[dd]), _block_diag(lru_w_x[dd])], axis=1)
                             for dd in range(2)]).astype(BF16)
    o["lru_bg"] = jnp.concatenate([lru_b_a, lru_b_x], axis=1).reshape(2, 1, 2 * c)
    o["lru_lam"] = lru_lam.reshape(2, 1, c)
    wq = w_q_up.reshape(Q_LORA, MLA_HEADS, QK_NOPE + QK_ROPE)
    rot = jnp.stack([_rotate_half_cols(wq[:, h, QK_NOPE:]) for h in range(MLA_HEADS)], axis=1)
    o["wq"] = jnp.concatenate([wq, rot], axis=2).reshape(Q_LORA, MLA_HEADS * HEAD_PAD).astype(BF16)
    wkv = w_kv_up.reshape(KV_LORA, MLA_HEADS, 2 * QK_NOPE)
    o["wk"] = jnp.concatenate([wkv[:, :, :QK_NOPE], jnp.zeros((KV_LORA, MLA_HEADS, HEAD_PAD - QK_NOPE), F32)],
                              axis=2).reshape(KV_LORA, MLA_HEADS * HEAD_PAD).astype(BF16)
    wv = jnp.concatenate([wkv[:, :, QK_NOPE:], jnp.zeros((KV_LORA, MLA_HEADS, ONES_ROWS), F32)], axis=2)
    o["wvt"] = wv.reshape(KV_LORA, MLA_HEADS * VT_ROWS).T.astype(BF16)
    o["w_s"] = cmlp_w_s.astype(BF16)
    o["w_branch"] = w_branch.astype(BF16)
    o["w_out"] = w_out.astype(BF16)
    o["w_ff1"] = w_ff1.astype(BF16)
    o["w_ff3"] = w_ff3.astype(BF16)
    o["w_ff2"] = w_ff2.astype(BF16)
    return o


def _rope_tables(n_lat):
    t = jnp.arange(n_lat)
    n_freq = QK_ROPE // 4
    inv = ROPE_THETA ** (-jnp.arange(n_freq, dtype=F32) / n_freq)
    ang_r = (t // GRID_W).astype(F32)[:, None] * inv
    ang_c = (t % GRID_W).astype(F32)[:, None] * inv
    cos = jnp.concatenate([jnp.cos(ang_r)] * 2 + [jnp.cos(ang_c)] * 2, axis=1)
    sin = jnp.concatenate([jnp.sin(ang_r)] * 2 + [jnp.sin(ang_c)] * 2, axis=1)
    ones = jnp.ones((n_lat, QK_NOPE), F32)
    z32 = jnp.zeros((n_lat, HEAD_PAD - QK_NOPE - QK_ROPE), F32)
    q_cos = jnp.concatenate([ones, cos, z32], axis=1)
    q_sin = jnp.concatenate([jnp.zeros((n_lat, QK_NOPE), F32), sin, z32], axis=1)
    k_cs = jnp.concatenate([cos, sin, jnp.zeros((n_lat, HEAD_PAD - 2 * QK_ROPE), F32)], axis=1)
    return q_cos, q_sin, k_cs


def _identity_tables(n):
    q_cos = jnp.concatenate([jnp.ones((n, QK_NOPE + QK_ROPE), F32),
                             jnp.zeros((n, HEAD_PAD - QK_NOPE - QK_ROPE), F32)], axis=1)
    q_sin = jnp.zeros((n, HEAD_PAD), F32)
    k_cs = jnp.concatenate([jnp.ones((n, QK_ROPE), F32), jnp.zeros((n, HEAD_PAD - QK_ROPE), F32)], axis=1)
    return q_cos, q_sin, k_cs


TM_PROJ = 512
TM_MIX = 512
TQ_ATTN = 2048
TU_ATTN = 512
KB_ATTN = 1024


def kernel(x, c, ctx, c_ctx, w_mod, b_mod, norm1_g, norm2_g, w_in, conv_a_w, lru_conv_w, lru_conv_b,
           lru_w_a, lru_b_a, lru_w_x, lru_b_x, lru_lam, cmlp_ln_g, cmlp_ln_b, cmlp_w_s, cmlp_b_s,
           mla_q_norm_g, mla_kv_norm_g, mla_w_q_up, mla_w_kv_up, w_branch, w_out, w_ff1, w_ff3, w_ff2,
           final_norm_g):
    bsz, n_lat, d = x.shape
    n_ctx = ctx.shape[1]
    depth = w_in.shape[0]
    cw = d // 2
    q_scale = (QK_NOPE + QK_ROPE) ** -0.5 * LOG2E
    row_in_head = jnp.arange(MLA_HEADS * VT_ROWS) % VT_ROWS
    vone = (row_in_head >= QK_NOPE).astype(F32)[:, None]

    rows = -(-(bsz + 1) // SUBLANES) * SUBLANES
    s_rows = jnp.concatenate([c, c_ctx[None, :], jnp.zeros((rows - bsz - 1, d), F32)], axis=0)
    mod = _modulation(s_rows, w_mod, b_mod)
    lat_row = lambda bi: bi
    ctx_row = lambda bi: bsz

    lat_tabs = _rope_tables(n_lat)
    ctx_tabs = _identity_tables(n_ctx)
    zero_state = jnp.zeros((bsz, 1, cw), F32)

    xc = ctx
    for l in range(depth):
        last = l == depth - 1
        w = _layer_weights(w_in[l], lru_w_a[l], lru_w_x[l], lru_b_a[l], lru_b_x[l], lru_lam[l],
                           mla_w_q_up[l], mla_w_kv_up[l], cmlp_w_s[l], w_branch[l], w_out[l],
                           w_ff1[l], w_ff3[l], w_ff2[l])
        mod3 = mod[l].reshape(rows, 1, 6 * d)
        cmlp = (cmlp_ln_g[l], cmlp_ln_b[l], cmlp_b_s[l])
        mla = (mla_q_norm_g[l], mla_kv_norm_g[l], vone)

        zc_lru, zc_conv, yc_c, qc, kc, vc = _inproj(xc, mod3, ctx_row, norm1_g[l], w, cmlp, mla, ctx_tabs,
                                                    q_scale, TM_PROJ)
        yc_b, hc_f, hc_b = _lru_mixer(zc_lru, zero_state, zero_state, lru_conv_w[l], lru_conv_b[l],
                                      w["lru_wg"], w["lru_bg"], w["lru_lam"], TM_MIX)

        z_lru, z_conv, y_c, q, k, v = _inproj(x, mod3, lat_row, norm1_g[l], w, cmlp, mla, lat_tabs,
                                              q_scale, TM_PROJ)
        y_b, _, _ = _lru_mixer(z_lru, hc_f, hc_b, lru_conv_w[l], lru_conv_b[l],
                               w["lru_wg"], w["lru_bg"], w["lru_lam"], TM_MIX)
        y_d = _attention(q, [(k, v), (kc, vc)], TQ_ATTN, TU_ATTN)
        y_a = _conv_mixer(z_conv, conv_a_w[l], TM_MIX)
        x = _merge(x, mod3, lat_row, norm1_g[l], (y_a, y_b, y_c, y_d), w["w_gate"], w["w_branch"], w["w_out"],
                   TM_PROJ)
        x = _ffn(x, mod3, lat_row, norm2_g[l], w["w_ff1"], w["w_ff3"], w["w_ff2"], final_norm_g, last, TM_PROJ)

        if not last:
            yc_d = _attention(qc, [(kc, vc)], TQ_ATTN, TU_ATTN)
            yc_a = _conv_mixer(zc_conv, conv_a_w[l], TM_MIX)
            xc = _merge(xc, mod3, ctx_row, norm1_g[l], (yc_a, yc_b, yc_c, yc_d), w["w_gate"], w["w_branch"],
                        w["w_out"], TM_PROJ)
            xc = _ffn(xc, mod3, ctx_row, norm2_g[l], w["w_ff1"], w["w_ff3"], w["w_ff2"], final_norm_g, False,
                      TM_PROJ)
    return x
```

```python
import functools

import jax
import jax.numpy as jnp
from jax import lax
from jax.experimental import pallas as pl
from jax.experimental.pallas import tpu as pltpu

F32 = jnp.float32
BF16 = jnp.bfloat16

EPS = 1e-6
GRID_W = 64
N_BRANCH = 4
LRU_HEADS = 8
LRU_C = 8.0
CMLP_GROUPS = 4
CHUNK = 128
MLA_HEADS = 8
QK_NOPE = 64
QK_ROPE = 32
Q_LORA = 384
KV_LORA = 256
ROPE_THETA = 10000.0
HEAD_PAD = 128
ONES_ROWS = 16
VT_ROWS = QK_NOPE + ONES_ROWS
LOG2E = 1.4426950408889634
SUBLANES = 8
VMEM_LIMIT = 56 * 1024 * 1024


def _params(sem, vmem=VMEM_LIMIT, flags=None):
    return pltpu.CompilerParams(dimension_semantics=sem, vmem_limit_bytes=vmem, flags=flags)


def _const_spec(shape):
    zeros = (0,) * len(shape)
    return pl.BlockSpec(shape, lambda *_: zeros, pipeline_mode=pl.Buffered(1))


def _gelu(x):
    return jax.nn.gelu(x)


def _norm_mod(x, g, shift, scale):
    y = x * lax.rsqrt(jnp.mean(x * x, axis=-1, keepdims=True) + EPS) * g
    return y * (1.0 + scale) + shift


def _dot(a, b):
    return jnp.dot(a, b, preferred_element_type=F32)


def _mod_kernel(s_ref, w_ref, b_ref, o_ref):
    s = s_ref[...]
    s = s * jax.nn.sigmoid(s)
    o_ref[0] = _dot(s.astype(BF16), w_ref[0].astype(BF16)) + b_ref[0]


def _modulation(s_rows, w_mod, b_mod):
    n_layer, d, d6 = w_mod.shape
    r = s_rows.shape[0]
    return pl.pallas_call(
        _mod_kernel,
        grid=(n_layer, d6 // d),
        in_specs=[
            pl.BlockSpec((r, d), lambda l, j: (0, 0)),
            pl.BlockSpec((1, d, d), lambda l, j: (l, 0, j)),
            pl.BlockSpec((1, 1, d), lambda l, j: (l, 0, j)),
        ],
        out_specs=pl.BlockSpec((1, r, d), lambda l, j: (l, 0, j)),
        out_shape=jax.ShapeDtypeStruct((n_layer, r, d6), F32),
        compiler_params=_params(("parallel", "parallel")),
        name="modulation",
    )(s_rows, w_mod, b_mod.reshape(n_layer, 1, d6))


def _inproj_kernel(x_ref, sh_ref, sc_ref, g_ref, w_lru, w_conv, w_cmlp, w_mla,
                   lng_ref, lnb_ref, ws_ref, bs_ref,
                   qg_ref, kvg_ref, wq_ref, wk_ref, wvt_ref, vone_ref, qc_ref, qs_ref, kcs_ref,
                   o_lru, o_conv, o_yc, q_ref, k_ref, vt_ref, *, q_scale):
    h = _norm_mod(x_ref[0], g_ref[...], sh_ref[0], sc_ref[0]).astype(BF16)
    zl = _dot(h, w_lru[...])
    zc = _dot(h, w_conv[...])
    c = zc.shape[1] // 3
    o_lru[0, :, :c] = zl[:, :c]
    o_lru[0, :, c:] = _gelu(zl[:, c:])
    o_conv[0, :, :c] = zc[:, :c]
    o_conv[0, :, c:] = zc[:, c:2 * c] * zc[:, 2 * c:]
    _cmlp_body(_dot(h, w_cmlp[...]), lng_ref, lnb_ref, ws_ref, bs_ref, o_yc)
    _mla_body(_dot(h, w_mla[...]), qg_ref, kvg_ref, wq_ref, wk_ref, wvt_ref, vone_ref,
              qc_ref, qs_ref, kcs_ref, q_ref, k_ref, vt_ref, q_scale)


def _inproj(x, mod3, row_of, g, w, cmlp, mla, tabs, q_scale, tm):
    b, l, d = x.shape
    tm = min(tm, l)
    c = d // 2
    ln_g, ln_b, b_s = cmlp
    q_g, kv_g, vone = mla
    hp = w["wq"].shape[1]
    vrows = w["wvt"].shape[0]
    consts = [g.reshape(1, d), w["w_lru"], w["w_conv"], w["w_cmlp"], w["w_mla"],
              ln_g.reshape(1, c), ln_b.reshape(1, c), w["w_s"], b_s.T,
              q_g.reshape(1, Q_LORA), kv_g.reshape(1, KV_LORA), w["wq"], w["wk"], w["wvt"], vone]
    tab = pl.BlockSpec((tm, HEAD_PAD), lambda bi, i: (i, 0))
    tok = lambda n: pl.BlockSpec((1, tm, n), lambda bi, i: (bi, i, 0))
    in_specs = ([tok(d),
                 pl.BlockSpec((1, 1, d), lambda bi, i: (row_of(bi), 0, 0)),
                 pl.BlockSpec((1, 1, d), lambda bi, i: (row_of(bi), 0, 1))]
                + [_const_spec(a.shape) for a in consts] + [tab, tab, tab])
    out_specs = [tok(2 * c), tok(2 * c), tok(c), tok(hp), tok(hp),
                 pl.BlockSpec((1, vrows, tm), lambda bi, i: (bi, 0, i))]
    out_shape = [jax.ShapeDtypeStruct((b, l, 2 * c), F32), jax.ShapeDtypeStruct((b, l, 2 * c), F32),
                 jax.ShapeDtypeStruct((b, l, c), BF16), jax.ShapeDtypeStruct((b, l, hp), BF16),
                 jax.ShapeDtypeStruct((b, l, hp), BF16), jax.ShapeDtypeStruct((b, vrows, l), BF16)]
    return pl.pallas_call(
        functools.partial(_inproj_kernel, q_scale=q_scale),
        grid=(b, l // tm),
        in_specs=in_specs,
        out_specs=out_specs,
        out_shape=out_shape,
        compiler_params=_params(("parallel", "parallel")),
        name="inproj",
    )(x, mod3, mod3, *consts, *tabs)


def _halo_specs(tm, l, c, lane_blk):
    nb = l // SUBLANES
    per = tm // SUBLANES
    prev = pl.BlockSpec((1, SUBLANES, c), lambda bi, i: (bi, jnp.maximum(i * per - 1, 0), lane_blk))
    nxt = pl.BlockSpec((1, SUBLANES, c), lambda bi, i: (bi, jnp.minimum((i + 1) * per, nb - 1), lane_blk))
    return prev, nxt


def _shift_down(p, first_row, k):
    row = lax.broadcasted_iota(jnp.int32, p.shape, 0)
    out = pltpu.roll(p, k, 0)
    for j in range(k):
        out = jnp.where(row == j, first_row[j:j + 1, :], out)
    return out


def _shift_up(p, last_row):
    tm = p.shape[0]
    row = lax.broadcasted_iota(jnp.int32, p.shape, 0)
    return jnp.where(row == tm - 1, last_row, pltpu.roll(p, tm - 1, 0))


def _lru_conv(x_ref, xp_ref, xn_ref, cw_ref, cb_ref, i, n_tiles):
    x = x_ref[0]
    prev = jnp.where(i > 0, xp_ref[0, SUBLANES - 2:, :], 0.0)
    nxt = jnp.where(i < n_tiles - 1, xn_ref[0, 0:1, :], 0.0)
    cw = cw_ref[...]
    x_m1 = _shift_down(x, prev[1:2], 1)
    x_m2 = _shift_down(x, prev, 2)
    return cw[0:1] * x_m2 + cw[1:2] * x_m1 + cw[2:3] * x + cw[3:4] * _shift_up(x, nxt) + cb_ref[...]


def _lru_coeffs(xl, wg_ref, bg_ref, lam_ref):
    c = xl.shape[1]
    gates = _dot(xl.astype(BF16), wg_ref[0]) + bg_ref[0]
    r = jax.nn.sigmoid(gates[:, :c])
    gi = jax.nn.sigmoid(gates[:, c:])
    lam = lam_ref[0]
    softplus_neg = jnp.maximum(-lam, 0.0) + jnp.log1p(jnp.exp(-jnp.abs(lam)))
    a = jnp.exp2((-LRU_C * LOG2E * softplus_neg) * r)
    y = 1.0 - a * a
    root = jnp.where(y > 0.0, y * lax.rsqrt(y), 0.0)
    return a, root * (gi * xl)


def _scan_tile(a, bb, a_scr, b_scr, h_scr, carry_ref, reverse):
    tm, c = a.shape
    groups = tm // SUBLANES
    acc_a = a.reshape(groups, SUBLANES, c)
    acc_b = bb.reshape(groups, SUBLANES, c)
    r8 = lax.broadcasted_iota(jnp.int32, acc_a.shape, 1)
    for k in (1, 2, 4):
        if reverse:
            keep = r8 < SUBLANES - k
            shift = SUBLANES - k
        else:
            keep = r8 >= k
            shift = k
        a_sh = jnp.where(keep, pltpu.roll(acc_a, shift, 1), 1.0)
        b_sh = jnp.where(keep, pltpu.roll(acc_b, shift, 1), 0.0)
        acc_b = acc_a * b_sh + acc_b
        acc_a = acc_a * a_sh
    a_scr[...] = acc_a.reshape(tm, c)
    b_scr[...] = acc_b.reshape(tm, c)

    def body(gi, h_in):
        g = groups - 1 - gi if reverse else gi
        rows = pl.ds(pl.multiple_of(g * SUBLANES, SUBLANES), SUBLANES)
        h8 = b_scr[rows, :] + a_scr[rows, :] * h_in
        h_scr[rows, :] = h8
        return h8[0:1, :] if reverse else h8[SUBLANES - 1:, :]

    carry_ref[...] = lax.fori_loop(0, groups, body, carry_ref[...], unroll=4)


def _lru_fwd_kernel(x_ref, xp_ref, xn_ref, h0_ref, cw_ref, cb_ref, wg_ref, bg_ref, lam_ref,
                    h_ref, xl_ref, hlast_ref, a_scr, b_scr, carry, *, n_tiles):
    i = pl.program_id(1)

    @pl.when(i == 0)
    def _():
        carry[...] = h0_ref[0]

    xl = _lru_conv(x_ref, xp_ref, xn_ref, cw_ref, cb_ref, i, n_tiles)
    xl_ref[0] = xl
    a, bb = _lru_coeffs(xl, wg_ref, bg_ref, lam_ref)
    _scan_tile(a, bb, a_scr, b_scr, h_ref.at[0], carry, reverse=False)
    hlast_ref[0] = carry[...]


def _lru_bwd_kernel(xl_ref, h0_ref, hf_ref, g_ref, wg_ref, bg_ref, lam_ref,
                    y_ref, hfirst_ref, a_scr, b_scr, h_scr, carry):
    i = pl.program_id(1)

    @pl.when(i == 0)
    def _():
        carry[...] = h0_ref[0]

    a, bb = _lru_coeffs(xl_ref[0], wg_ref, bg_ref, lam_ref)
    _scan_tile(a, bb, a_scr, b_scr, h_scr, carry, reverse=True)
    hfirst_ref[0] = carry[...]
    y_ref[0] = (g_ref[0] * (hf_ref[0] + h_scr[...])).astype(y_ref.dtype)


def _lru_mixer(z_lru, h0_f, h0_b, conv_w, conv_b, wg, bg, lam, tm):
    b, l, c2 = z_lru.shape
    c = c2 // 2
    tm = min(tm, l)
    nt = l // tm
    state = jax.ShapeDtypeStruct((b, 1, c), F32)
    state_spec = pl.BlockSpec((1, 1, c), lambda bi, i: (bi, 0, 0))
    cp = _params(("parallel", "arbitrary"))
    gate = lambda d: [pl.BlockSpec((1, c, c2), lambda bi, i: (d, 0, 0)),
                      pl.BlockSpec((1, 1, c2), lambda bi, i: (d, 0, 0)),
                      pl.BlockSpec((1, 1, c), lambda bi, i: (d, 0, 0))]
    tile_f = pl.BlockSpec((1, tm, c), lambda bi, i: (bi, i, 0))

    prev_f, next_f = _halo_specs(tm, l, c, 0)
    hf, xl, hf_last = pl.pallas_call(
        functools.partial(_lru_fwd_kernel, n_tiles=nt),
        grid=(b, nt),
        in_specs=[tile_f, prev_f, next_f, state_spec, _const_spec(conv_w.shape), _const_spec((1, c))] + gate(0),
        out_specs=[tile_f, tile_f, state_spec],
        out_shape=[jax.ShapeDtypeStruct((b, l, c), F32), jax.ShapeDtypeStruct((b, l, c), F32), state],
        scratch_shapes=[pltpu.VMEM((tm, c), F32), pltpu.VMEM((tm, c), F32), pltpu.VMEM((1, c), F32)],
        compiler_params=cp,
        name="lru_fwd",
    )(z_lru, z_lru, z_lru, h0_f, conv_w, conv_b.reshape(1, c), wg, bg, lam)

    tile_b = lambda blk: pl.BlockSpec((1, tm, c), lambda bi, i: (bi, nt - 1 - i, blk))
    y, hb_first = pl.pallas_call(
        _lru_bwd_kernel,
        grid=(b, nt),
        in_specs=[tile_b(0), state_spec, tile_b(0), tile_b(1)] + gate(1),
        out_specs=[tile_b(0), state_spec],
        out_shape=[jax.ShapeDtypeStruct((b, l, c), BF16), state],
        scratch_shapes=[pltpu.VMEM((tm, c), F32), pltpu.VMEM((tm, c), F32), pltpu.VMEM((tm, c), F32),
                        pltpu.VMEM((1, c), F32)],
        compiler_params=cp,
        name="lru_bwd",
    )(xl, h0_b, hf, z_lru, wg, bg, lam)
    return y, hf_last, hb_first


def _cmlp_body(z, g_ref, b_ref, ws_ref, bs_ref, o_ref):
    tm, c = z.shape[0], z.shape[1] // 2
    gd = c // CMLP_GROUPS
    u = _gelu(z[:, :c])
    v = _gelu(z[:, c:])
    mu = jnp.mean(v, axis=-1, keepdims=True)
    var = jnp.mean(jnp.square(v - mu), axis=-1, keepdims=True)
    v = ((v - mu) * lax.rsqrt(var + EPS) * g_ref[...] + b_ref[...]).astype(BF16)
    bs = bs_ref[...]
    for ck in range(tm // CHUNK):
        rows = slice(ck * CHUNK, (ck + 1) * CHUNK)
        for g in range(CMLP_GROUPS):
            cols = slice(g * gd, (g + 1) * gd)
            mixed = _dot(ws_ref[g], v[rows, cols]) + bs[:, g:g + 1]
            o_ref[0, rows, cols] = (u[rows, cols] * mixed).astype(o_ref.dtype)


def _rms(x, g):
    return x * lax.rsqrt(jnp.mean(x * x, axis=-1, keepdims=True) + EPS) * g


def _mla_body(z, qg_ref, kvg_ref, wq_ref, wk_ref, wvt_ref, vone_ref,
              qc_ref, qs_ref, kcs_ref, q_ref, k_ref, vt_ref, q_scale):
    heads = q_ref.shape[2] // HEAD_PAD
    kvn = _rms(z[:, :KV_LORA], kvg_ref[...]).astype(BF16)
    qn = _rms(z[:, KV_LORA:KV_LORA + Q_LORA], qg_ref[...]).astype(BF16)
    t = z[:, KV_LORA + Q_LORA:] * kcs_ref[...]
    t = t + pltpu.roll(t, HEAD_PAD - QK_ROPE, 1)
    lane = lax.broadcasted_iota(jnp.int32, t.shape, 1)
    rope = jnp.where((lane >= QK_NOPE) & (lane < QK_NOPE + QK_ROPE), pltpu.roll(t, QK_NOPE, 1), 0.0)
    k = _dot(kvn, wk_ref[...]) + jnp.concatenate([rope] * heads, axis=1)
    k_ref[0] = k.astype(k_ref.dtype)
    v_t = lax.dot_general(wvt_ref[...], kvn, (((1,), (1,)), ((), ())), preferred_element_type=F32)
    vt_ref[0] = (v_t + vone_ref[...]).astype(vt_ref.dtype)
    qa = _dot(qn, wq_ref[...])
    qc = jnp.concatenate([qc_ref[...]] * heads, axis=1)
    qs = jnp.concatenate([qs_ref[...]] * heads, axis=1)
    q = qa * qc + pltpu.roll(qa, qa.shape[1] - QK_ROPE, 1) * qs
    q_ref[0] = (q * q_scale).astype(q_ref.dtype)


def _attn_kernel(q_ref, *refs, n_kv, tu, kb):
    kv_refs, o_ref = refs[:2 * n_kv], refs[2 * n_kv]
    s_buf, p_buf, mx_buf, acc_ref, ot_ref, qt_ref = refs[2 * n_kv + 1:]
    hv = o_ref.shape[2] // 2
    hr = kv_refs[1].shape[1] // 2
    tq = q_ref.shape[1]
    blocks = []
    off = 0
    for j in range(n_kv):
        k_ref, vt_ref = kv_refs[2 * j], kv_refs[2 * j + 1]
        lk = k_ref.shape[1]
        for s0 in range(0, lk, kb):
            blocks.append((k_ref, vt_ref, s0, off + s0, min(kb, lk - s0)))
        off += lk
    units = [(qi, hh) for qi in range(tq // tu) for hh in range(2)]
    n = len(units)

    qt_ref[...] = q_ref[0].T

    def stage_qk(u, bi):
        qi, hh = units[u]
        k_ref, _, s0, g0, size = blocks[bi]
        lanes = slice(hh * HEAD_PAD, (hh + 1) * HEAD_PAD)
        s = _dot(k_ref[0, s0:s0 + size, lanes], qt_ref[lanes, qi * tu:(qi + 1) * tu])
        s_buf[g0:g0 + size, :] = s
        m = jnp.max(s, axis=0, keepdims=True)
        mx_buf[u % 2] = m if bi == 0 else jnp.maximum(mx_buf[u % 2], m)

    def stage_exp(u, bi):
        _, _, _, g0, size = blocks[bi]
        p_buf[g0:g0 + size, :] = jnp.exp2(s_buf[g0:g0 + size, :] - mx_buf[u % 2]).astype(BF16)

    def stage_pv(u, bi):
        qi, hh = units[u]
        _, vt_ref, s0, g0, size = blocks[bi]
        pv = _dot(vt_ref[0, hh * hr:(hh + 1) * hr, s0:s0 + size], p_buf[g0:g0 + size, :])
        if bi > 0:
            pv = pv + acc_ref[...]
        if bi < len(blocks) - 1:
            acc_ref[...] = pv
        else:
            ot_ref[hh * hv:(hh + 1) * hv, qi * tu:(qi + 1) * tu] = pv[:hv] * (1.0 / pv[hv:hv + 1])

    for t in range(n + 2):
        for bi in range(len(blocks)):
            if 0 <= t - 2 < n:
                stage_pv(t - 2, bi)
            if 0 <= t - 1 < n:
                stage_exp(t - 1, bi)
            if t < n:
                stage_qk(t, bi)
    o_ref[0] = ot_ref[...].T.astype(o_ref.dtype)


def _attention(q, kvs, tq, tu):
    b, l, hp = q.shape
    heads = hp // HEAD_PAD
    tq = min(tq, l)
    tu = min(tu, tq)
    in_specs = [pl.BlockSpec((1, tq, 2 * HEAD_PAD), lambda bi, h, i: (bi, i, h))]
    args = [q]
    lk_total = 0
    for k, vt in kvs:
        lk = k.shape[1]
        lk_total += lk
        in_specs.append(pl.BlockSpec((1, lk, 2 * HEAD_PAD), lambda bi, h, i: (bi, 0, h)))
        in_specs.append(pl.BlockSpec((1, 2 * VT_ROWS, lk), lambda bi, h, i: (bi, h, 0)))
        args += [k, vt]
    return pl.pallas_call(
        functools.partial(_attn_kernel, n_kv=len(kvs), tu=tu, kb=KB_ATTN),
        grid=(b, heads // 2, l // tq),
        in_specs=in_specs,
        out_specs=pl.BlockSpec((1, tq, 2 * QK_NOPE), lambda bi, h, i: (bi, i, h)),
        out_shape=jax.ShapeDtypeStruct((b, l, heads * QK_NOPE), BF16),
        scratch_shapes=[pltpu.VMEM((lk_total, tu), F32), pltpu.VMEM((lk_total, tu), BF16),
                        pltpu.VMEM((2, 1, tu), F32), pltpu.VMEM((VT_ROWS, tu), F32),
                        pltpu.VMEM((2 * QK_NOPE, tq), F32),
                        pltpu.VMEM((2 * HEAD_PAD, tq), BF16)],
        compiler_params=_params(("parallel", "parallel", "parallel")),
        name="attention",
    )(*args)


def _merge_kernel(x_ref, sh_ref, sc_ref, gt_ref, g_ref, ab_ref, p_ref, pp_ref, pn_ref, cw_ref,
                  yb_ref, yc_ref, yd_ref, wg_ref, wb_ref, wo_ref, o_ref, *, n_tiles):
    i = pl.program_id(1)
    p = p_ref[0]
    p_prev = jnp.where(i > 0, pp_ref[0, SUBLANES - 1:, :], 0.0)
    p_next = jnp.where(i < n_tiles - 1, pn_ref[0, 0:1, :], 0.0)
    cw = cw_ref[...]
    conv = cw[0:1] * _shift_down(p, p_prev, 1) + cw[1:2] * p + cw[2:3] * _shift_up(p, p_next)
    y_a = (ab_ref[0] * conv).astype(BF16)

    x = x_ref[0]
    d = x.shape[1]
    h = _norm_mod(x, g_ref[...], sh_ref[0], sc_ref[0]).astype(BF16)
    merged = None
    for n, y in enumerate((y_a, yb_ref[0], yc_ref[0], yd_ref[0])):
        gate = jax.nn.sigmoid(_dot(h, wg_ref[:, n * d:(n + 1) * d]))
        term = gate * _dot(y, wb_ref[n])
        merged = term if merged is None else merged + term
    o_ref[0] = x + gt_ref[0] * _dot(merged.astype(BF16), wo_ref[...])


def _merge(x, mod3, row_of, g, z_conv, conv_w, ys, wg, wb, wo, tm):
    b, l, d = x.shape
    tm = min(tm, l)
    c = ys[0].shape[2]
    tok = lambda n: pl.BlockSpec((1, tm, n), lambda bi, i: (bi, i, 0))
    modv = lambda j: pl.BlockSpec((1, 1, d), lambda bi, i: (row_of(bi), 0, j))
    prev, nxt = _halo_specs(tm, l, c, 1)
    return pl.pallas_call(
        functools.partial(_merge_kernel, n_tiles=l // tm),
        grid=(b, l // tm),
        in_specs=[tok(d), modv(0), modv(1), modv(2), _const_spec((1, d)),
                  tok(c), pl.BlockSpec((1, tm, c), lambda bi, i: (bi, i, 1)), prev, nxt, _const_spec(conv_w.shape),
                  tok(c), tok(c), tok(c),
                  _const_spec(wg.shape), _const_spec(wb.shape), _const_spec(wo.shape)],
        out_specs=tok(d),
        out_shape=jax.ShapeDtypeStruct((b, l, d), F32),
        compiler_params=_params(("parallel", "parallel")),
        name="merge",
    )(x, mod3, mod3, mod3, g.reshape(1, d), z_conv, z_conv, z_conv, z_conv, conv_w, *ys, wg, wb, wo)


def _ffn_kernel(x_ref, sh_ref, sc_ref, gt_ref, g_ref, w1_ref, w3_ref, w2_ref, fg_ref, o_ref, *, final_norm):
    x = x_ref[0]
    h = _norm_mod(x, g_ref[...], sh_ref[0], sc_ref[0]).astype(BF16)
    a = _dot(h, w1_ref[...])
    u = (a * jax.nn.sigmoid(a) * _dot(h, w3_ref[...])).astype(BF16)
    y = x + gt_ref[0] * _dot(u, w2_ref[...])
    if final_norm:
        y = _rms(y, fg_ref[...])
    o_ref[0] = y


def _ffn(x, mod3, row_of, g, w1, w3, w2, final_g, final_norm, tm):
    b, l, d = x.shape
    tm = min(tm, l)
    tok = pl.BlockSpec((1, tm, d), lambda bi, i: (bi, i, 0))
    modv = lambda j: pl.BlockSpec((1, 1, d), lambda bi, i: (row_of(bi), 0, j))
    return pl.pallas_call(
        functools.partial(_ffn_kernel, final_norm=final_norm),
        grid=(b, l // tm),
        in_specs=[tok, modv(3), modv(4), modv(5), _const_spec((1, d)),
                  _const_spec(w1.shape), _const_spec(w3.shape), _const_spec(w2.shape), _const_spec((1, d))],
        out_specs=tok,
        out_shape=jax.ShapeDtypeStruct((b, l, d), F32),
        compiler_params=_params(("parallel", "parallel")),
        name="ffn",
    )(x, mod3, mod3, mod3, g.reshape(1, d), w1, w3, w2, final_g.reshape(1, d))


def _rotate_half_cols(w):
    q = QK_ROPE // 4
    return jnp.concatenate([-w[:, q:2 * q], w[:, :q], -w[:, 3 * q:], w[:, 2 * q:3 * q]], axis=1)


def _block_diag(w):
    h, hd, _ = w.shape
    eye = jnp.eye(h, dtype=w.dtype)
    return (eye[:, None, :, None] * w[:, :, None, :]).reshape(h * hd, h * hd)


def _layer_weights(w_in_l, lru_w_a, lru_w_x, lru_b_a, lru_b_x, lru_lam, w_q_up, w_kv_up, cmlp_w_s,
                   w_branch, w_out, w_ff1, w_ff3, w_ff2):
    d = w_in_l.shape[0]
    c = d // 2
    o = {}
    i_lru_x, i_kv, i_kr, i_lru_g, i_q = 0, c, c + KV_LORA, c + KV_LORA + QK_ROPE, 2 * c + KV_LORA + QK_ROPE
    i_ab = i_q + Q_LORA
    i_ac, i_ax, i_cu, i_cv, i_gate = i_ab + c, i_ab + 2 * c, i_ab + 3 * c, i_ab + 4 * c, i_ab + 5 * c
    col = lambda s, n: w_in_l[:, s:s + n]
    k_rope = col(i_kr, QK_ROPE)
    mla_pad = jnp.zeros((d, HEAD_PAD - 2 * QK_ROPE), F32)
    o["w_lru"] = jnp.concatenate([col(i_lru_x, c), col(i_lru_g, c)], axis=1).astype(BF16)
    o["w_mla"] = jnp.concatenate([col(i_kv, KV_LORA), col(i_q, Q_LORA), k_rope, _rotate_half_cols(k_rope),
                                  mla_pad], axis=1).astype(BF16)
    o["w_conv"] = w_in_l[:, i_ab:i_cu].astype(BF16)
    o["w_cmlp"] = w_in_l[:, i_cu:i_gate].astype(BF16)
    o["w_gate"] = w_in_l[:, i_gate:].astype(BF16)
    o["lru_wg"] = jnp.stack([jnp.concatenate([_block_diag(lru_w_a[dd]), _block_diag(lru_w_x[dd])], axis=1)
                             for dd in range(2)]).astype(BF16)
    o["lru_bg"] = jnp.concatenate([lru_b_a, lru_b_x], axis=1).reshape(2, 1, 2 * c)
    o["lru_lam"] = lru_lam.reshape(2, 1, c)
    wq = w_q_up.reshape(Q_LORA, MLA_HEADS, QK_NOPE + QK_ROPE)
    rot = jnp.stack([_rotate_half_cols(wq[:, h, QK_NOPE:]) for h in range(MLA_HEADS)], axis=1)
    o["wq"] = jnp.concatenate([wq, rot], axis=2).reshape(Q_LORA, MLA_HEADS * HEAD_PAD).astype(BF16)
    wkv = w_kv_up.reshape(KV_LORA, MLA_HEADS, 2 * QK_NOPE)
    o["wk"] = jnp.concatenate([wkv[:, :, :QK_NOPE], jnp.zeros((KV_LORA, MLA_HEADS, HEAD_PAD - QK_NOPE), F32)],
                              axis=2).reshape(KV_LORA, MLA_HEADS * HEAD_PAD).astype(BF16)
    wv = jnp.concatenate([wkv[:, :, QK_NOPE:], jnp.zeros((KV_LORA, MLA_HEADS, ONES_ROWS), F32)], axis=2)
    o["wvt"] = wv.reshape(KV_LORA, MLA_HEADS * VT_ROWS).T.astype(BF16)
    o["w_s"] = cmlp_w_s.astype(BF16)
    o["w_branch"] = w_branch.astype(BF16)
    o["w_out"] = w_out.astype(BF16)
    o["w_ff1"] = w_ff1.astype(BF16)
    o["w_ff3"] = w_ff3.astype(BF16)
    o["w_ff2"] = w_ff2.astype(BF16)
    return o


def _rope_tables(n_lat):
    t = jnp.arange(n_lat)
    n_freq = QK_ROPE // 4
    inv = ROPE_THETA ** (-jnp.arange(n_freq, dtype=F32) / n_freq)
    ang_r = (t // GRID_W).astype(F32)[:, None] * inv
    ang_c = (t % GRID_W).astype(F32)[:, None] * inv
    cos = jnp.concatenate([jnp.cos(ang_r)] * 2 + [jnp.cos(ang_c)] * 2, axis=1)
    sin = jnp.concatenate([jnp.sin(ang_r)] * 2 + [jnp.sin(ang_c)] * 2, axis=1)
    ones = jnp.ones((n_lat, QK_NOPE), F32)
    z32 = jnp.zeros((n_lat, HEAD_PAD - QK_NOPE - QK_ROPE), F32)
    q_cos = jnp.concatenate([ones, cos, z32], axis=1)
    q_sin = jnp.concatenate([jnp.zeros((n_lat, QK_NOPE), F32), sin, z32], axis=1)
    k_cs = jnp.concatenate([cos, sin, jnp.zeros((n_lat, HEAD_PAD - 2 * QK_ROPE), F32)], axis=1)
    return q_cos, q_sin, k_cs


def _identity_tables(n):
    q_cos = jnp.concatenate([jnp.ones((n, QK_NOPE + QK_ROPE), F32),
                             jnp.zeros((n, HEAD_PAD - QK_NOPE - QK_ROPE), F32)], axis=1)
    q_sin = jnp.zeros((n, HEAD_PAD), F32)
    k_cs = jnp.concatenate([jnp.ones((n, QK_ROPE), F32), jnp.zeros((n, HEAD_PAD - QK_ROPE), F32)], axis=1)
    return q_cos, q_sin, k_cs


TM_PROJ = 512
TM_MIX = 1024
TQ_ATTN = 2048
TU_ATTN = 512
KB_ATTN = 1024


def kernel(x, c, ctx, c_ctx, w_mod, b_mod, norm1_g, norm2_g, w_in, conv_a_w, lru_conv_w, lru_conv_b,
           lru_w_a, lru_b_a, lru_w_x, lru_b_x, lru_lam, cmlp_ln_g, cmlp_ln_b, cmlp_w_s, cmlp_b_s,
           mla_q_norm_g, mla_kv_norm_g, mla_w_q_up, mla_w_kv_up, w_branch, w_out, w_ff1, w_ff3, w_ff2,
           final_norm_g):
    bsz, n_lat, d = x.shape
    n_ctx = ctx.shape[1]
    depth = w_in.shape[0]
    cw = d // 2
    q_scale = (QK_NOPE + QK_ROPE) ** -0.5 * LOG2E
    row_in_head = jnp.arange(MLA_HEADS * VT_ROWS) % VT_ROWS
    vone = (row_in_head >= QK_NOPE).astype(F32)[:, None]

    rows = -(-(bsz + 1) // SUBLANES) * SUBLANES
    s_rows = jnp.concatenate([c, c_ctx[None, :], jnp.zeros((rows - bsz - 1, d), F32)], axis=0)
    mod = _modulation(s_rows, w_mod, b_mod)
    lat_row = lambda bi: bi
    ctx_row = lambda bi: bsz

    lat_tabs = _rope_tables(n_lat)
    ctx_tabs = _identity_tables(n_ctx)
    zero_state = jnp.zeros((bsz, 1, cw), F32)

    xc = ctx
    for l in range(depth):
        last = l == depth - 1
        w = _layer_weights(w_in[l], lru_w_a[l], lru_w_x[l], lru_b_a[l], lru_b_x[l], lru_lam[l],
                           mla_w_q_up[l], mla_w_kv_up[l], cmlp_w_s[l], w_branch[l], w_out[l],
                           w_ff1[l], w_ff3[l], w_ff2[l])
        mod3 = mod[l].reshape(rows, 1, 6 * d)
        cmlp = (cmlp_ln_g[l], cmlp_ln_b[l], cmlp_b_s[l])
        mla = (mla_q_norm_g[l], mla_kv_norm_g[l], vone)

        zc_lru, zc_conv, yc_c, qc, kc, vc = _inproj(xc, mod3, ctx_row, norm1_g[l], w, cmlp, mla, ctx_tabs,
                                                    q_scale, TM_PROJ)
        yc_b, hc_f, hc_b = _lru_mixer(zc_lru, zero_state, zero_state, lru_conv_w[l], lru_conv_b[l],
                                      w["lru_wg"], w["lru_bg"], w["lru_lam"], TM_MIX)

        z_lru, z_conv, y_c, q, k, v = _inproj(x, mod3, lat_row, norm1_g[l], w, cmlp, mla, lat_tabs,
                                              q_scale, TM_PROJ)
        y_b, _, _ = _lru_mixer(z_lru, hc_f, hc_b, lru_conv_w[l], lru_conv_b[l],
                               w["lru_wg"], w["lru_bg"], w["lru_lam"], TM_MIX)
        y_d = _attention(q, [(k, v), (kc, vc)], TQ_ATTN, TU_ATTN)
        x = _merge(x, mod3, lat_row, norm1_g[l], z_conv, conv_a_w[l], (y_b, y_c, y_d), w["w_gate"],
                   w["w_branch"], w["w_out"], TM_PROJ)
        x = _ffn(x, mod3, lat_row, norm2_g[l], w["w_ff1"], w["w_ff3"], w["w_ff2"], final_norm_g, last, TM_PROJ)

        if not last:
            yc_d = _attention(qc, [(kc, vc)], TQ_ATTN, TU_ATTN)
            xc = _merge(xc, mod3, ctx_row, norm1_g[l], zc_conv, conv_a_w[l], (yc_b, yc_c, yc_d), w["w_gate"],
                        w["w_branch"], w["w_out"], TM_PROJ)
            xc = _ffn(xc, mod3, ctx_row, norm2_g[l], w["w_ff1"], w["w_ff3"], w["w_ff2"], final_norm_g, False,
                      TM_PROJ)
    return x
```

```python
import functools

import jax
import jax.numpy as jnp
from jax import lax
from jax.experimental import pallas as pl
from jax.experimental.pallas import tpu as pltpu

F32 = jnp.float32
BF16 = jnp.bfloat16

EPS = 1e-6
GRID_W = 64
N_BRANCH = 4
LRU_HEADS = 8
LRU_C = 8.0
CMLP_GROUPS = 4
CHUNK = 128
MLA_HEADS = 8
QK_NOPE = 64
QK_ROPE = 32
Q_LORA = 384
KV_LORA = 256
ROPE_THETA = 10000.0
HEAD_PAD = 128
ONES_ROWS = 16
VT_ROWS = QK_NOPE + ONES_ROWS
LOG2E = 1.4426950408889634
SUBLANES = 8
VMEM_LIMIT = 56 * 1024 * 1024


def _params(sem, vmem=VMEM_LIMIT, flags=None):
    return pltpu.CompilerParams(dimension_semantics=sem, vmem_limit_bytes=vmem, flags=flags)


def _const_spec(shape):
    zeros = (0,) * len(shape)
    return pl.BlockSpec(shape, lambda *_: zeros, pipeline_mode=pl.Buffered(1))


def _layer_spec(stacked, layer):
    index = (layer,) + (0,) * (stacked.ndim - 1)
    return pl.BlockSpec((None,) + stacked.shape[1:], lambda *_: index, pipeline_mode=pl.Buffered(1))


def _gelu(x):
    return jax.nn.gelu(x)


def _norm_mod(x, g, shift, scale):
    y = x * lax.rsqrt(jnp.mean(x * x, axis=-1, keepdims=True) + EPS) * g
    return y * (1.0 + scale) + shift


def _dot(a, b):
    return jnp.dot(a, b, preferred_element_type=F32)


def _mod_kernel(s_ref, w_ref, b_ref, o_ref):
    s = s_ref[...]
    s = s * jax.nn.sigmoid(s)
    o_ref[0] = _dot(s.astype(BF16), w_ref[0].astype(BF16)) + b_ref[0]


def _modulation(s_rows, w_mod, b_mod):
    n_layer, d, d6 = w_mod.shape
    r = s_rows.shape[0]
    return pl.pallas_call(
        _mod_kernel,
        grid=(n_layer, d6 // d),
        in_specs=[
            pl.BlockSpec((r, d), lambda l, j: (0, 0)),
            pl.BlockSpec((1, d, d), lambda l, j: (l, 0, j)),
            pl.BlockSpec((1, 1, d), lambda l, j: (l, 0, j)),
        ],
        out_specs=pl.BlockSpec((1, r, d), lambda l, j: (l, 0, j)),
        out_shape=jax.ShapeDtypeStruct((n_layer, r, d6), F32),
        compiler_params=_params(("parallel", "parallel")),
        name="modulation",
    )(s_rows, w_mod, b_mod.reshape(n_layer, 1, d6))


def _inproj_kernel(x_ref, sh_ref, sc_ref, g_ref, w_lru, w_conv, w_cmlp, w_mla,
                   lng_ref, lnb_ref, ws_ref, bs_ref,
                   qg_ref, kvg_ref, wq_ref, wk_ref, wvt_ref, vone_ref, qc_ref, qs_ref, kcs_ref,
                   o_lru, o_conv, o_yc, q_ref, k_ref, vt_ref, *, q_scale):
    h = _norm_mod(x_ref[0], g_ref[...], sh_ref[0], sc_ref[0]).astype(BF16)
    zl = _dot(h, w_lru[...])
    zc = _dot(h, w_conv[...])
    c = zc.shape[1] // 3
    o_lru[0, :, :c] = zl[:, :c]
    o_lru[0, :, c:] = _gelu(zl[:, c:])
    o_conv[0, :, :c] = zc[:, :c]
    o_conv[0, :, c:] = zc[:, c:2 * c] * zc[:, 2 * c:]
    _cmlp_body(_dot(h, w_cmlp[...]), lng_ref, lnb_ref, ws_ref, bs_ref, o_yc)
    _mla_body(_dot(h, w_mla[...]), qg_ref, kvg_ref, wq_ref, wk_ref, wvt_ref, vone_ref,
              qc_ref, qs_ref, kcs_ref, q_ref, k_ref, vt_ref, q_scale)


def _inproj(x, mod3, row_of, g, w, layer, cmlp, mla, tabs, q_scale, tm):
    b, l, d = x.shape
    tm = min(tm, l)
    c = d // 2
    ln_g, ln_b, b_s = cmlp
    q_g, kv_g, vone = mla
    hp = w["wq"].shape[2]
    vrows = w["wvt"].shape[1]
    whole = lambda a: (a, _const_spec(a.shape))
    stacked = lambda name: (w[name], _layer_spec(w[name], layer))
    consts = [whole(g.reshape(1, d)), stacked("w_lru"), stacked("w_conv"), stacked("w_cmlp"), stacked("w_mla"),
              whole(ln_g.reshape(1, c)), whole(ln_b.reshape(1, c)), stacked("w_s"), whole(b_s.T),
              whole(q_g.reshape(1, Q_LORA)), whole(kv_g.reshape(1, KV_LORA)), stacked("wq"), stacked("wk"),
              stacked("wvt"), whole(vone)]
    tab = pl.BlockSpec((tm, HEAD_PAD), lambda bi, i: (i, 0))
    tok = lambda n: pl.BlockSpec((1, tm, n), lambda bi, i: (bi, i, 0))
    in_specs = ([tok(d),
                 pl.BlockSpec((1, 1, d), lambda bi, i: (row_of(bi), 0, 0)),
                 pl.BlockSpec((1, 1, d), lambda bi, i: (row_of(bi), 0, 1))]
                + [spec for _, spec in consts] + [tab, tab, tab])
    out_specs = [tok(2 * c), tok(2 * c), tok(c), tok(hp), tok(hp),
                 pl.BlockSpec((1, vrows, tm), lambda bi, i: (bi, 0, i))]
    out_shape = [jax.ShapeDtypeStruct((b, l, 2 * c), F32), jax.ShapeDtypeStruct((b, l, 2 * c), F32),
                 jax.ShapeDtypeStruct((b, l, c), BF16), jax.ShapeDtypeStruct((b, l, hp), BF16),
                 jax.ShapeDtypeStruct((b, l, hp), BF16), jax.ShapeDtypeStruct((b, vrows, l), BF16)]
    return pl.pallas_call(
        functools.partial(_inproj_kernel, q_scale=q_scale),
        grid=(b, l // tm),
        in_specs=in_specs,
        out_specs=out_specs,
        out_shape=out_shape,
        compiler_params=_params(("parallel", "parallel")),
        name="inproj",
    )(x, mod3, mod3, *[a for a, _ in consts], *tabs)


def _halo_specs(tm, l, c, lane_blk):
    nb = l // SUBLANES
    per = tm // SUBLANES
    prev = pl.BlockSpec((1, SUBLANES, c), lambda bi, i: (bi, jnp.maximum(i * per - 1, 0), lane_blk))
    nxt = pl.BlockSpec((1, SUBLANES, c), lambda bi, i: (bi, jnp.minimum((i + 1) * per, nb - 1), lane_blk))
    return prev, nxt


def _shift_down(p, first_row, k):
    row = lax.broadcasted_iota(jnp.int32, p.shape, 0)
    out = pltpu.roll(p, k, 0)
    for j in range(k):
        out = jnp.where(row == j, first_row[j:j + 1, :], out)
    return out


def _shift_up(p, last_row):
    tm = p.shape[0]
    row = lax.broadcasted_iota(jnp.int32, p.shape, 0)
    return jnp.where(row == tm - 1, last_row, pltpu.roll(p, tm - 1, 0))


def _lru_conv(x_ref, xp_ref, xn_ref, cw_ref, cb_ref, i, n_tiles):
    x = x_ref[0]
    prev = jnp.where(i > 0, xp_ref[0, SUBLANES - 2:, :], 0.0)
    nxt = jnp.where(i < n_tiles - 1, xn_ref[0, 0:1, :], 0.0)
    cw = cw_ref[...]
    x_m1 = _shift_down(x, prev[1:2], 1)
    x_m2 = _shift_down(x, prev, 2)
    return cw[0:1] * x_m2 + cw[1:2] * x_m1 + cw[2:3] * x + cw[3:4] * _shift_up(x, nxt) + cb_ref[...]


def _lru_coeffs(xl, wg_ref, bg_ref, lam_ref):
    c = xl.shape[1]
    gates = _dot(xl.astype(BF16), wg_ref[0]) + bg_ref[0]
    r = jax.nn.sigmoid(gates[:, :c])
    gi = jax.nn.sigmoid(gates[:, c:])
    lam = lam_ref[0]
    softplus_neg = jnp.maximum(-lam, 0.0) + jnp.log1p(jnp.exp(-jnp.abs(lam)))
    a = jnp.exp2((-LRU_C * LOG2E * softplus_neg) * r)
    y = 1.0 - a * a
    root = jnp.where(y > 0.0, y * lax.rsqrt(y), 0.0)
    return a, root * (gi * xl)


def _scan_tile(a, bb, a_scr, b_scr, h_scr, carry_ref, reverse):
    tm, c = a.shape
    groups = tm // SUBLANES
    acc_a = a.reshape(groups, SUBLANES, c)
    acc_b = bb.reshape(groups, SUBLANES, c)
    r8 = lax.broadcasted_iota(jnp.int32, acc_a.shape, 1)
    for k in (1, 2, 4):
        if reverse:
            keep = r8 < SUBLANES - k
            shift = SUBLANES - k
        else:
            keep = r8 >= k
            shift = k
        a_sh = jnp.where(keep, pltpu.roll(acc_a, shift, 1), 1.0)
        b_sh = jnp.where(keep, pltpu.roll(acc_b, shift, 1), 0.0)
        acc_b = acc_a * b_sh + acc_b
        acc_a = acc_a * a_sh
    a_scr[...] = acc_a.reshape(tm, c)
    b_scr[...] = acc_b.reshape(tm, c)

    def body(gi, h_in):
        g = groups - 1 - gi if reverse else gi
        rows = pl.ds(pl.multiple_of(g * SUBLANES, SUBLANES), SUBLANES)
        h8 = b_scr[rows, :] + a_scr[rows, :] * h_in
        h_scr[rows, :] = h8
        return h8[0:1, :] if reverse else h8[SUBLANES - 1:, :]

    carry_ref[...] = lax.fori_loop(0, groups, body, carry_ref[...], unroll=4)


def _lru_fwd_kernel(x_ref, xp_ref, xn_ref, h0_ref, cw_ref, cb_ref, wg_ref, bg_ref, lam_ref,
                    h_ref, xl_ref, hlast_ref, a_scr, b_scr, carry, *, n_tiles):
    i = pl.program_id(1)

    @pl.when(i == 0)
    def _():
        carry[...] = h0_ref[0]

    xl = _lru_conv(x_ref, xp_ref, xn_ref, cw_ref, cb_ref, i, n_tiles)
    xl_ref[0] = xl
    a, bb = _lru_coeffs(xl, wg_ref, bg_ref, lam_ref)
    _scan_tile(a, bb, a_scr, b_scr, h_ref.at[0], carry, reverse=False)
    hlast_ref[0] = carry[...]


def _lru_bwd_kernel(xl_ref, h0_ref, hf_ref, g_ref, wg_ref, bg_ref, lam_ref,
                    y_ref, hfirst_ref, a_scr, b_scr, h_scr, carry):
    i = pl.program_id(1)

    @pl.when(i == 0)
    def _():
        carry[...] = h0_ref[0]

    a, bb = _lru_coeffs(xl_ref[0], wg_ref, bg_ref, lam_ref)
    _scan_tile(a, bb, a_scr, b_scr, h_scr, carry, reverse=True)
    hfirst_ref[0] = carry[...]
    y_ref[0] = (g_ref[0] * (hf_ref[0] + h_scr[...])).astype(y_ref.dtype)


def _lru_mixer(z_lru, h0_f, h0_b, conv_w, conv_b, wg, layer, bg, lam, tm):
    b, l, c2 = z_lru.shape
    c = c2 // 2
    tm = min(tm, l)
    nt = l // tm
    state = jax.ShapeDtypeStruct((b, 1, c), F32)
    state_spec = pl.BlockSpec((1, 1, c), lambda bi, i: (bi, 0, 0))
    cp = _params(("parallel", "arbitrary"))
    gate = lambda d: [pl.BlockSpec((None, 1, c, c2), lambda bi, i: (layer, d, 0, 0)),
                      pl.BlockSpec((1, 1, c2), lambda bi, i: (d, 0, 0)),
                      pl.BlockSpec((1, 1, c), lambda bi, i: (d, 0, 0))]
    tile_f = pl.BlockSpec((1, tm, c), lambda bi, i: (bi, i, 0))

    prev_f, next_f = _halo_specs(tm, l, c, 0)
    hf, xl, hf_last = pl.pallas_call(
        functools.partial(_lru_fwd_kernel, n_tiles=nt),
        grid=(b, nt),
        in_specs=[tile_f, prev_f, next_f, state_spec, _const_spec(conv_w.shape), _const_spec((1, c))] + gate(0),
        out_specs=[tile_f, tile_f, state_spec],
        out_shape=[jax.ShapeDtypeStruct((b, l, c), F32), jax.ShapeDtypeStruct((b, l, c), F32), state],
        scratch_shapes=[pltpu.VMEM((tm, c), F32), pltpu.VMEM((tm, c), F32), pltpu.VMEM((1, c), F32)],
        compiler_params=cp,
        name="lru_fwd",
    )(z_lru, z_lru, z_lru, h0_f, conv_w, conv_b.reshape(1, c), wg, bg, lam)

    tile_b = lambda blk: pl.BlockSpec((1, tm, c), lambda bi, i: (bi, nt - 1 - i, blk))
    y, hb_first = pl.pallas_call(
        _lru_bwd_kernel,
        grid=(b, nt),
        in_specs=[tile_b(0), state_spec, tile_b(0), tile_b(1)] + gate(1),
        out_specs=[tile_b(0), state_spec],
        out_shape=[jax.ShapeDtypeStruct((b, l, c), BF16), state],
        scratch_shapes=[pltpu.VMEM((tm, c), F32), pltpu.VMEM((tm, c), F32), pltpu.VMEM((tm, c), F32),
                        pltpu.VMEM((1, c), F32)],
        compiler_params=cp,
        name="lru_bwd",
    )(xl, h0_b, hf, z_lru, wg, bg, lam)
    return y, hf_last, hb_first


def _cmlp_body(z, g_ref, b_ref, ws_ref, bs_ref, o_ref):
    tm, c = z.shape[0], z.shape[1] // 2
    gd = c // CMLP_GROUPS
    u = _gelu(z[:, :c])
    v = _gelu(z[:, c:])
    mu = jnp.mean(v, axis=-1, keepdims=True)
    var = jnp.mean(jnp.square(v - mu), axis=-1, keepdims=True)
    v = ((v - mu) * lax.rsqrt(var + EPS) * g_ref[...] + b_ref[...]).astype(BF16)
    bs = bs_ref[...]
    for ck in range(tm // CHUNK):
        rows = slice(ck * CHUNK, (ck + 1) * CHUNK)
        for g in range(CMLP_GROUPS):
            cols = slice(g * gd, (g + 1) * gd)
            mixed = _dot(ws_ref[g], v[rows, cols]) + bs[:, g:g + 1]
            o_ref[0, rows, cols] = (u[rows, cols] * mixed).astype(o_ref.dtype)


def _rms(x, g):
    return x * lax.rsqrt(jnp.mean(x * x, axis=-1, keepdims=True) + EPS) * g


def _mla_body(z, qg_ref, kvg_ref, wq_ref, wk_ref, wvt_ref, vone_ref,
              qc_ref, qs_ref, kcs_ref, q_ref, k_ref, vt_ref, q_scale):
    heads = q_ref.shape[2] // HEAD_PAD
    kvn = _rms(z[:, :KV_LORA], kvg_ref[...]).astype(BF16)
    qn = _rms(z[:, KV_LORA:KV_LORA + Q_LORA], qg_ref[...]).astype(BF16)
    t = z[:, KV_LORA + Q_LORA:] * kcs_ref[...]
    t = t + pltpu.roll(t, HEAD_PAD - QK_ROPE, 1)
    lane = lax.broadcasted_iota(jnp.int32, t.shape, 1)
    rope = jnp.where((lane >= QK_NOPE) & (lane < QK_NOPE + QK_ROPE), pltpu.roll(t, QK_NOPE, 1), 0.0)
    k = _dot(kvn, wk_ref[...]) + jnp.concatenate([rope] * heads, axis=1)
    k_ref[0] = k.astype(k_ref.dtype)
    v_t = lax.dot_general(wvt_ref[...], kvn, (((1,), (1,)), ((), ())), preferred_element_type=F32)
    vt_ref[0] = (v_t + vone_ref[...]).astype(vt_ref.dtype)
    qa = _dot(qn, wq_ref[...])
    qc = jnp.concatenate([qc_ref[...]] * heads, axis=1)
    qs = jnp.concatenate([qs_ref[...]] * heads, axis=1)
    q = qa * qc + pltpu.roll(qa, qa.shape[1] - QK_ROPE, 1) * qs
    q_ref[0] = (q * q_scale).astype(q_ref.dtype)


def _attn_kernel(q_ref, *refs, n_kv, tu, kb):
    kv_refs, o_ref = refs[:2 * n_kv], refs[2 * n_kv]
    s_buf, p_buf, mx_buf, acc_ref, ot_ref, qt_ref = refs[2 * n_kv + 1:]
    hv = o_ref.shape[2] // 2
    hr = kv_refs[1].shape[1] // 2
    tq = q_ref.shape[1]
    blocks = []
    off = 0
    for j in range(n_kv):
        k_ref, vt_ref = kv_refs[2 * j], kv_refs[2 * j + 1]
        lk = k_ref.shape[1]
        for s0 in range(0, lk, kb):
            blocks.append((k_ref, vt_ref, s0, off + s0, min(kb, lk - s0)))
        off += lk
    units = [(qi, hh) for qi in range(tq // tu) for hh in range(2)]
    n = len(units)

    qt_ref[...] = q_ref[0].T

    def stage_qk(u, bi):
        qi, hh = units[u]
        k_ref, _, s0, g0, size = blocks[bi]
        lanes = slice(hh * HEAD_PAD, (hh + 1) * HEAD_PAD)
        s = _dot(k_ref[0, s0:s0 + size, lanes], qt_ref[lanes, qi * tu:(qi + 1) * tu])
        s_buf[g0:g0 + size, :] = s
        m = jnp.max(s, axis=0, keepdims=True)
        mx_buf[u % 2] = m if bi == 0 else jnp.maximum(mx_buf[u % 2], m)

    def stage_exp(u, bi):
        _, _, _, g0, size = blocks[bi]
        p_buf[g0:g0 + size, :] = jnp.exp2(s_buf[g0:g0 + size, :] - mx_buf[u % 2]).astype(BF16)

    def stage_pv(u, bi):
        qi, hh = units[u]
        _, vt_ref, s0, g0, size = blocks[bi]
        pv = _dot(vt_ref[0, hh * hr:(hh + 1) * hr, s0:s0 + size], p_buf[g0:g0 + size, :])
        if bi > 0:
            pv = pv + acc_ref[...]
        if bi < len(blocks) - 1:
            acc_ref[...] = pv
        else:
            ot_ref[hh * hv:(hh + 1) * hv, qi * tu:(qi + 1) * tu] = pv[:hv] * (1.0 / pv[hv:hv + 1])

    for t in range(n + 2):
        for bi in range(len(blocks)):
            if 0 <= t - 2 < n:
                stage_pv(t - 2, bi)
            if 0 <= t - 1 < n:
                stage_exp(t - 1, bi)
            if t < n:
                stage_qk(t, bi)
    o_ref[0] = ot_ref[...].T.astype(o_ref.dtype)


def _attention(q, kvs, tq, tu):
    b, l, hp = q.shape
    heads = hp // HEAD_PAD
    tq = min(tq, l)
    tu = min(tu, tq)
    in_specs = [pl.BlockSpec((1, tq, 2 * HEAD_PAD), lambda bi, h, i: (bi, i, h))]
    args = [q]
    lk_total = 0
    for k, vt in kvs:
        lk = k.shape[1]
        lk_total += lk
        in_specs.append(pl.BlockSpec((1, lk, 2 * HEAD_PAD), lambda bi, h, i: (bi, 0, h)))
        in_specs.append(pl.BlockSpec((1, 2 * VT_ROWS, lk), lambda bi, h, i: (bi, h, 0)))
        args += [k, vt]
    return pl.pallas_call(
        functools.partial(_attn_kernel, n_kv=len(kvs), tu=tu, kb=KB_ATTN),
        grid=(b, heads // 2, l // tq),
        in_specs=in_specs,
        out_specs=pl.BlockSpec((1, tq, 2 * QK_NOPE), lambda bi, h, i: (bi, i, h)),
        out_shape=jax.ShapeDtypeStruct((b, l, heads * QK_NOPE), BF16),
        scratch_shapes=[pltpu.VMEM((lk_total, tu), F32), pltpu.VMEM((lk_total, tu), BF16),
                        pltpu.VMEM((2, 1, tu), F32), pltpu.VMEM((VT_ROWS, tu), F32),
                        pltpu.VMEM((2 * QK_NOPE, tq), F32),
                        pltpu.VMEM((2 * HEAD_PAD, tq), BF16)],
        compiler_params=_params(("parallel", "parallel", "parallel")),
        name="attention",
    )(*args)


def _merge_kernel(x_ref, sh_ref, sc_ref, gt_ref, g_ref, ab_ref, p_ref, pp_ref, pn_ref, cw_ref,
                  yb_ref, yc_ref, yd_ref, wg_ref, wb_ref, wo_ref, o_ref, *, n_tiles):
    i = pl.program_id(1)
    p = p_ref[0]
    p_prev = jnp.where(i > 0, pp_ref[0, SUBLANES - 1:, :], 0.0)
    p_next = jnp.where(i < n_tiles - 1, pn_ref[0, 0:1, :], 0.0)
    cw = cw_ref[...]
    conv = cw[0:1] * _shift_down(p, p_prev, 1) + cw[1:2] * p + cw[2:3] * _shift_up(p, p_next)
    y_a = (ab_ref[0] * conv).astype(BF16)

    x = x_ref[0]
    d = x.shape[1]
    h = _norm_mod(x, g_ref[...], sh_ref[0], sc_ref[0]).astype(BF16)
    merged = None
    for n, y in enumerate((y_a, yb_ref[0], yc_ref[0], yd_ref[0])):
        gate = jax.nn.sigmoid(_dot(h, wg_ref[:, n * d:(n + 1) * d]))
        term = gate * _dot(y, wb_ref[n])
        merged = term if merged is None else merged + term
    o_ref[0] = x + gt_ref[0] * _dot(merged.astype(BF16), wo_ref[...])


def _merge(x, mod3, row_of, g, z_conv, conv_w, ys, wg, wb, wo, layer, tm):
    b, l, d = x.shape
    tm = min(tm, l)
    c = ys[0].shape[2]
    tok = lambda n: pl.BlockSpec((1, tm, n), lambda bi, i: (bi, i, 0))
    modv = lambda j: pl.BlockSpec((1, 1, d), lambda bi, i: (row_of(bi), 0, j))
    prev, nxt = _halo_specs(tm, l, c, 1)
    return pl.pallas_call(
        functools.partial(_merge_kernel, n_tiles=l // tm),
        grid=(b, l // tm),
        in_specs=[tok(d), modv(0), modv(1), modv(2), _const_spec((1, d)),
                  tok(c), pl.BlockSpec((1, tm, c), lambda bi, i: (bi, i, 1)), prev, nxt, _const_spec(conv_w.shape),
                  tok(c), tok(c), tok(c),
                  _layer_spec(wg, layer), _layer_spec(wb, layer), _layer_spec(wo, layer)],
        out_specs=tok(d),
        out_shape=jax.ShapeDtypeStruct((b, l, d), F32),
        compiler_params=_params(("parallel", "parallel")),
        name="merge",
    )(x, mod3, mod3, mod3, g.reshape(1, d), z_conv, z_conv, z_conv, z_conv, conv_w, *ys, wg, wb, wo)


def _ffn_kernel(x_ref, sh_ref, sc_ref, gt_ref, g_ref, w1_ref, w3_ref, w2_ref, fg_ref, o_ref, *, final_norm):
    x = x_ref[0]
    h = _norm_mod(x, g_ref[...], sh_ref[0], sc_ref[0]).astype(BF16)
    a = _dot(h, w1_ref[...])
    u = (a * jax.nn.sigmoid(a) * _dot(h, w3_ref[...])).astype(BF16)
    y = x + gt_ref[0] * _dot(u, w2_ref[...])
    if final_norm:
        y = _rms(y, fg_ref[...])
    o_ref[0] = y


def _ffn(x, mod3, row_of, g, w1, w3, w2, layer, final_g, final_norm, tm):
    b, l, d = x.shape
    tm = min(tm, l)
    tok = pl.BlockSpec((1, tm, d), lambda bi, i: (bi, i, 0))
    modv = lambda j: pl.BlockSpec((1, 1, d), lambda bi, i: (row_of(bi), 0, j))
    return pl.pallas_call(
        functools.partial(_ffn_kernel, final_norm=final_norm),
        grid=(b, l // tm),
        in_specs=[tok, modv(3), modv(4), modv(5), _const_spec((1, d)),
                  _layer_spec(w1, layer), _layer_spec(w3, layer), _layer_spec(w2, layer), _const_spec((1, d))],
        out_specs=tok,
        out_shape=jax.ShapeDtypeStruct((b, l, d), F32),
        compiler_params=_params(("parallel", "parallel")),
        name="ffn",
    )(x, mod3, mod3, mod3, g.reshape(1, d), w1, w3, w2, final_g.reshape(1, d))


def _rotate_half_cols(w):
    q = QK_ROPE // 4
    return jnp.concatenate([-w[:, q:2 * q], w[:, :q], -w[:, 3 * q:], w[:, 2 * q:3 * q]], axis=1)


def _block_diag(w):
    h, hd, _ = w.shape
    eye = jnp.eye(h, dtype=w.dtype)
    return (eye[:, None, :, None] * w[:, :, None, :]).reshape(h * hd, h * hd)


def _layer_weights(w_in_l, lru_w_a, lru_w_x, lru_b_a, lru_b_x, lru_lam, w_q_up, w_kv_up, cmlp_w_s,
                   w_branch, w_out, w_ff1, w_ff3, w_ff2):
    d = w_in_l.shape[0]
    c = d // 2
    o = {}
    i_lru_x, i_kv, i_kr, i_lru_g, i_q = 0, c, c + KV_LORA, c + KV_LORA + QK_ROPE, 2 * c + KV_LORA + QK_ROPE
    i_ab = i_q + Q_LORA
    i_ac, i_ax, i_cu, i_cv, i_gate = i_ab + c, i_ab + 2 * c, i_ab + 3 * c, i_ab + 4 * c, i_ab + 5 * c
    col = lambda s, n: w_in_l[:, s:s + n]
    k_rope = col(i_kr, QK_ROPE)
    mla_pad = jnp.zeros((d, HEAD_PAD - 2 * QK_ROPE), F32)
    o["w_lru"] = jnp.concatenate([col(i_lru_x, c), col(i_lru_g, c)], axis=1).astype(BF16)
    o["w_mla"] = jnp.concatenate([col(i_kv, KV_LORA), col(i_q, Q_LORA), k_rope, _rotate_half_cols(k_rope),
                                  mla_pad], axis=1).astype(BF16)
    o["w_conv"] = w_in_l[:, i_ab:i_cu].astype(BF16)
    o["w_cmlp"] = w_in_l[:, i_cu:i_gate].astype(BF16)
    o["w_gate"] = w_in_l[:, i_gate:].astype(BF16)
    o["lru_wg"] = jnp.stack([jnp.concatenate([_block_diag(lru_w_a[dd]), _block_diag(lru_w_x[dd])], axis=1)
                             for dd in range(2)]).astype(BF16)
    o["lru_bg"] = jnp.concatenate([lru_b_a, lru_b_x], axis=1).reshape(2, 1, 2 * c)
    o["lru_lam"] = lru_lam.reshape(2, 1, c)
    wq = w_q_up.reshape(Q_LORA, MLA_HEADS, QK_NOPE + QK_ROPE)
    rot = jnp.stack([_rotate_half_cols(wq[:, h, QK_NOPE:]) for h in range(MLA_HEADS)], axis=1)
    o["wq"] = jnp.concatenate([wq, rot], axis=2).reshape(Q_LORA, MLA_HEADS * HEAD_PAD).astype(BF16)
    wkv = w_kv_up.reshape(KV_LORA, MLA_HEADS, 2 * QK_NOPE)
    o["wk"] = jnp.concatenate([wkv[:, :, :QK_NOPE], jnp.zeros((KV_LORA, MLA_HEADS, HEAD_PAD - QK_NOPE), F32)],
                              axis=2).reshape(KV_LORA, MLA_HEADS * HEAD_PAD).astype(BF16)
    wv = jnp.concatenate([wkv[:, :, QK_NOPE:], jnp.zeros((KV_LORA, MLA_HEADS, ONES_ROWS), F32)], axis=2)
    o["wvt"] = wv.reshape(KV_LORA, MLA_HEADS * VT_ROWS).T.astype(BF16)
    o["w_s"] = cmlp_w_s.astype(BF16)
    o["w_branch"] = w_branch.astype(BF16)
    o["w_out"] = w_out.astype(BF16)
    o["w_ff1"] = w_ff1.astype(BF16)
    o["w_ff3"] = w_ff3.astype(BF16)
    o["w_ff2"] = w_ff2.astype(BF16)
    return o


def _rope_tables(n_lat):
    t = jnp.arange(n_lat)
    n_freq = QK_ROPE // 4
    inv = ROPE_THETA ** (-jnp.arange(n_freq, dtype=F32) / n_freq)
    ang_r = (t // GRID_W).astype(F32)[:, None] * inv
    ang_c = (t % GRID_W).astype(F32)[:, None] * inv
    cos = jnp.concatenate([jnp.cos(ang_r)] * 2 + [jnp.cos(ang_c)] * 2, axis=1)
    sin = jnp.concatenate([jnp.sin(ang_r)] * 2 + [jnp.sin(ang_c)] * 2, axis=1)
    ones = jnp.ones((n_lat, QK_NOPE), F32)
    z32 = jnp.zeros((n_lat, HEAD_PAD - QK_NOPE - QK_ROPE), F32)
    q_cos = jnp.concatenate([ones, cos, z32], axis=1)
    q_sin = jnp.concatenate([jnp.zeros((n_lat, QK_NOPE), F32), sin, z32], axis=1)
    k_cs = jnp.concatenate([cos, sin, jnp.zeros((n_lat, HEAD_PAD - 2 * QK_ROPE), F32)], axis=1)
    return q_cos, q_sin, k_cs


def _identity_tables(n):
    q_cos = jnp.concatenate([jnp.ones((n, QK_NOPE + QK_ROPE), F32),
                             jnp.zeros((n, HEAD_PAD - QK_NOPE - QK_ROPE), F32)], axis=1)
    q_sin = jnp.zeros((n, HEAD_PAD), F32)
    k_cs = jnp.concatenate([jnp.ones((n, QK_ROPE), F32), jnp.zeros((n, HEAD_PAD - QK_ROPE), F32)], axis=1)
    return q_cos, q_sin, k_cs


TM_PROJ = 512
TM_MIX = 1024
TQ_ATTN = 2048
TU_ATTN = 512
KB_ATTN = 1024


def kernel(x, c, ctx, c_ctx, w_mod, b_mod, norm1_g, norm2_g, w_in, conv_a_w, lru_conv_w, lru_conv_b,
           lru_w_a, lru_b_a, lru_w_x, lru_b_x, lru_lam, cmlp_ln_g, cmlp_ln_b, cmlp_w_s, cmlp_b_s,
           mla_q_norm_g, mla_kv_norm_g, mla_w_q_up, mla_w_kv_up, w_branch, w_out, w_ff1, w_ff3, w_ff2,
           final_norm_g):
    bsz, n_lat, d = x.shape
    n_ctx = ctx.shape[1]
    depth = w_in.shape[0]
    cw = d // 2
    q_scale = (QK_NOPE + QK_ROPE) ** -0.5 * LOG2E
    row_in_head = jnp.arange(MLA_HEADS * VT_ROWS) % VT_ROWS
    vone = (row_in_head >= QK_NOPE).astype(F32)[:, None]

    rows = -(-(bsz + 1) // SUBLANES) * SUBLANES
    s_rows = jnp.concatenate([c, c_ctx[None, :], jnp.zeros((rows - bsz - 1, d), F32)], axis=0)
    mod = _modulation(s_rows, w_mod, b_mod)
    lat_row = lambda bi: bi
    ctx_row = lambda bi: bsz

    lat_tabs = _rope_tables(n_lat)
    ctx_tabs = _identity_tables(n_ctx)
    zero_state = jnp.zeros((bsz, 1, cw), F32)

    w = jax.vmap(_layer_weights)(w_in, lru_w_a, lru_w_x, lru_b_a, lru_b_x, lru_lam, mla_w_q_up, mla_w_kv_up,
                                 cmlp_w_s, w_branch, w_out, w_ff1, w_ff3, w_ff2)

    xc = ctx
    for l in range(depth):
        last = l == depth - 1
        mod3 = mod[l].reshape(rows, 1, 6 * d)
        cmlp = (cmlp_ln_g[l], cmlp_ln_b[l], cmlp_b_s[l])
        mla = (mla_q_norm_g[l], mla_kv_norm_g[l], vone)
        lru = (lru_conv_w[l], lru_conv_b[l], w["lru_wg"], l, w["lru_bg"][l], w["lru_lam"][l], TM_MIX)
        gate_w = (w["w_gate"], w["w_branch"], w["w_out"], l, TM_PROJ)
        ffn_w = (w["w_ff1"], w["w_ff3"], w["w_ff2"], l, final_norm_g)

        zc_lru, zc_conv, yc_c, qc, kc, vc = _inproj(xc, mod3, ctx_row, norm1_g[l], w, l, cmlp, mla, ctx_tabs,
                                                    q_scale, TM_PROJ)
        yc_b, hc_f, hc_b = _lru_mixer(zc_lru, zero_state, zero_state, *lru)

        z_lru, z_conv, y_c, q, k, v = _inproj(x, mod3, lat_row, norm1_g[l], w, l, cmlp, mla, lat_tabs,
                                              q_scale, TM_PROJ)
        y_b, _, _ = _lru_mixer(z_lru, hc_f, hc_b, *lru)
        y_d = _attention(q, [(k, v), (kc, vc)], TQ_ATTN, TU_ATTN)
        x = _merge(x, mod3, lat_row, norm1_g[l], z_conv, conv_a_w[l], (y_b, y_c, y_d), *gate_w)
        x = _ffn(x, mod3, lat_row, norm2_g[l], *ffn_w, last, TM_PROJ)

        if not last:
            yc_d = _attention(qc, [(kc, vc)], TQ_ATTN, TU_ATTN)
            xc = _merge(xc, mod3, ctx_row, norm1_g[l], zc_conv, conv_a_w[l], (yc_b, yc_c, yc_d), *gate_w)
            xc = _ffn(xc, mod3, ctx_row, norm2_g[l], *ffn_w, False, TM_PROJ)
    return x
```

```python
import functools

import jax
import jax.numpy as jnp
from jax import lax
from jax.experimental import pallas as pl
from jax.experimental.pallas import tpu as pltpu

F32 = jnp.float32
BF16 = jnp.bfloat16

EPS = 1e-6
GRID_W = 64
N_BRANCH = 4
LRU_HEADS = 8
LRU_C = 8.0
CMLP_GROUPS = 4
CHUNK = 128
MLA_HEADS = 8
QK_NOPE = 64
QK_ROPE = 32
Q_LORA = 384
KV_LORA = 256
ROPE_THETA = 10000.0
HEAD_PAD = 128
ONES_ROWS = 16
VT_ROWS = QK_NOPE + ONES_ROWS
LOG2E = 1.4426950408889634
SUBLANES = 8
VMEM_LIMIT = 56 * 1024 * 1024


def _params(sem, vmem=VMEM_LIMIT, flags=None):
    return pltpu.CompilerParams(dimension_semantics=sem, vmem_limit_bytes=vmem, flags=flags)


def _const_spec(shape):
    zeros = (0,) * len(shape)
    return pl.BlockSpec(shape, lambda *_: zeros, pipeline_mode=pl.Buffered(1))


def _layer_spec(stacked, layer):
    index = (layer,) + (0,) * (stacked.ndim - 1)
    return pl.BlockSpec((None,) + stacked.shape[1:], lambda *_: index, pipeline_mode=pl.Buffered(1))


def _gelu(x):
    return jax.nn.gelu(x)


def _norm_mod(x, g, shift, scale):
    gain = g * (1.0 + scale)
    return x * lax.rsqrt(jnp.mean(x * x, axis=-1, keepdims=True) + EPS) * gain + shift


def _dot(a, b):
    return jnp.dot(a, b, preferred_element_type=F32)


def _mod_kernel(s_ref, w_ref, b_ref, o_ref):
    s = s_ref[...]
    s = s * jax.nn.sigmoid(s)
    o_ref[0] = _dot(s.astype(BF16), w_ref[0].astype(BF16)) + b_ref[0]


def _modulation(s_rows, w_mod, b_mod):
    n_layer, d, d6 = w_mod.shape
    r = s_rows.shape[0]
    return pl.pallas_call(
        _mod_kernel,
        grid=(n_layer, d6 // d),
        in_specs=[
            pl.BlockSpec((r, d), lambda l, j: (0, 0)),
            pl.BlockSpec((1, d, d), lambda l, j: (l, 0, j)),
            pl.BlockSpec((1, 1, d), lambda l, j: (l, 0, j)),
        ],
        out_specs=pl.BlockSpec((1, r, d), lambda l, j: (l, 0, j)),
        out_shape=jax.ShapeDtypeStruct((n_layer, r, d6), F32),
        compiler_params=_params(("parallel", "parallel")),
        name="modulation",
    )(s_rows, w_mod, b_mod.reshape(n_layer, 1, d6))


def _inproj_kernel(x_ref, sh_ref, sc_ref, g_ref, w_lru, w_conv, w_cmlp, w_mla,
                   lng_ref, lnb_ref, ws_ref, bs_ref,
                   qg_ref, kvg_ref, wq_ref, wk_ref, wvt_ref, vone_ref, qc_ref, qs_ref, kcs_ref,
                   o_h, o_lru, o_conv, o_yc, q_ref, k_ref, vt_ref, *, q_scale):
    h = _norm_mod(x_ref[0], g_ref[...], sh_ref[0], sc_ref[0]).astype(BF16)
    o_h[0] = h
    zl = _dot(h, w_lru[...])
    zc = _dot(h, w_conv[...])
    c = zc.shape[1] // 3
    o_lru[0, :, :c] = zl[:, :c]
    o_lru[0, :, c:] = _gelu(zl[:, c:])
    o_conv[0, :, :c] = zc[:, :c]
    o_conv[0, :, c:] = zc[:, c:2 * c] * zc[:, 2 * c:]
    _cmlp_body(_dot(h, w_cmlp[...]), lng_ref, lnb_ref, ws_ref, bs_ref, o_yc)
    _mla_body(_dot(h, w_mla[...]), qg_ref, kvg_ref, wq_ref, wk_ref, wvt_ref, vone_ref,
              qc_ref, qs_ref, kcs_ref, q_ref, k_ref, vt_ref, q_scale)


def _inproj(x, mod3, row_of, g, w, layer, cmlp, mla, tabs, q_scale, tm):
    b, l, d = x.shape
    tm = min(tm, l)
    c = d // 2
    ln_g, ln_b, b_s = cmlp
    q_g, kv_g, vone = mla
    hp = w["wq"].shape[2]
    vrows = w["wvt"].shape[1]
    whole = lambda a: (a, _const_spec(a.shape))
    stacked = lambda name: (w[name], _layer_spec(w[name], layer))
    consts = [whole(g.reshape(1, d)), stacked("w_lru"), stacked("w_conv"), stacked("w_cmlp"), stacked("w_mla"),
              whole(ln_g.reshape(1, c)), whole(ln_b.reshape(1, c)), stacked("w_s"), whole(b_s.T),
              whole(q_g.reshape(1, Q_LORA)), whole(kv_g.reshape(1, KV_LORA)), stacked("wq"), stacked("wk"),
              stacked("wvt"), whole(vone)]
    tab = pl.BlockSpec((tm, HEAD_PAD), lambda bi, i: (i, 0))
    tok = lambda n: pl.BlockSpec((1, tm, n), lambda bi, i: (bi, i, 0))
    in_specs = ([tok(d),
                 pl.BlockSpec((1, 1, d), lambda bi, i: (row_of(bi), 0, 0)),
                 pl.BlockSpec((1, 1, d), lambda bi, i: (row_of(bi), 0, 1))]
                + [spec for _, spec in consts] + [tab, tab, tab])
    out_specs = [tok(d), tok(2 * c), tok(2 * c), tok(c), tok(hp), tok(hp),
                 pl.BlockSpec((1, vrows, tm), lambda bi, i: (bi, 0, i))]
    out_shape = [jax.ShapeDtypeStruct((b, l, d), BF16),
                 jax.ShapeDtypeStruct((b, l, 2 * c), F32), jax.ShapeDtypeStruct((b, l, 2 * c), F32),
                 jax.ShapeDtypeStruct((b, l, c), BF16), jax.ShapeDtypeStruct((b, l, hp), BF16),
                 jax.ShapeDtypeStruct((b, l, hp), BF16), jax.ShapeDtypeStruct((b, vrows, l), BF16)]
    return pl.pallas_call(
        functools.partial(_inproj_kernel, q_scale=q_scale),
        grid=(b, l // tm),
        in_specs=in_specs,
        out_specs=out_specs,
        out_shape=out_shape,
        compiler_params=_params(("parallel", "parallel")),
        name="inproj",
    )(x, mod3, mod3, *[a for a, _ in consts], *tabs)


def _halo_specs(tm, l, c, lane_blk):
    nb = l // SUBLANES
    per = tm // SUBLANES
    prev = pl.BlockSpec((1, SUBLANES, c), lambda bi, i: (bi, jnp.maximum(i * per - 1, 0), lane_blk))
    nxt = pl.BlockSpec((1, SUBLANES, c), lambda bi, i: (bi, jnp.minimum((i + 1) * per, nb - 1), lane_blk))
    return prev, nxt


def _shift_down(p, first_row, k):
    row = lax.broadcasted_iota(jnp.int32, p.shape, 0)
    out = pltpu.roll(p, k, 0)
    for j in range(k):
        out = jnp.where(row == j, first_row[j:j + 1, :], out)
    return out


def _shift_up(p, last_row):
    tm = p.shape[0]
    row = lax.broadcasted_iota(jnp.int32, p.shape, 0)
    return jnp.where(row == tm - 1, last_row, pltpu.roll(p, tm - 1, 0))


def _lru_conv(x_ref, xp_ref, xn_ref, cw_ref, cb_ref, i, n_tiles):
    x = x_ref[0]
    prev = jnp.where(i > 0, xp_ref[0, SUBLANES - 2:, :], 0.0)
    nxt = jnp.where(i < n_tiles - 1, xn_ref[0, 0:1, :], 0.0)
    cw = cw_ref[...]
    x_m1 = _shift_down(x, prev[1:2], 1)
    x_m2 = _shift_down(x, prev, 2)
    return cw[0:1] * x_m2 + cw[1:2] * x_m1 + cw[2:3] * x + cw[3:4] * _shift_up(x, nxt) + cb_ref[...]


def _lru_coeffs(xl, wg_ref, bg_ref, lam_ref):
    c = xl.shape[1]
    gates = _dot(xl.astype(BF16), wg_ref[0]) + bg_ref[0]
    r = jax.nn.sigmoid(gates[:, :c])
    gi = jax.nn.sigmoid(gates[:, c:])
    lam = lam_ref[0]
    softplus_neg = jnp.maximum(-lam, 0.0) + jnp.log1p(jnp.exp(-jnp.abs(lam)))
    a = jnp.exp2((-LRU_C * LOG2E * softplus_neg) * r)
    y = 1.0 - a * a
    root = jnp.where(y > 0.0, y * lax.rsqrt(y), 0.0)
    return a, root * (gi * xl)


def _scan_tile(a, bb, a_scr, b_scr, h_scr, carry_ref, reverse):
    tm, c = a.shape
    groups = tm // SUBLANES
    acc_a = a.reshape(groups, SUBLANES, c)
    acc_b = bb.reshape(groups, SUBLANES, c)
    r8 = lax.broadcasted_iota(jnp.int32, acc_a.shape, 1)
    for k in (1, 2, 4):
        if reverse:
            keep = r8 < SUBLANES - k
            shift = SUBLANES - k
        else:
            keep = r8 >= k
            shift = k
        a_sh = jnp.where(keep, pltpu.roll(acc_a, shift, 1), 1.0)
        b_sh = jnp.where(keep, pltpu.roll(acc_b, shift, 1), 0.0)
        acc_b = acc_a * b_sh + acc_b
        acc_a = acc_a * a_sh
    a_scr[...] = acc_a.reshape(tm, c)
    b_scr[...] = acc_b.reshape(tm, c)

    def body(gi, h_in):
        g = groups - 1 - gi if reverse else gi
        rows = pl.ds(pl.multiple_of(g * SUBLANES, SUBLANES), SUBLANES)
        h8 = b_scr[rows, :] + a_scr[rows, :] * h_in
        h_scr[rows, :] = h8
        return h8[0:1, :] if reverse else h8[SUBLANES - 1:, :]

    carry_ref[...] = lax.fori_loop(0, groups, body, carry_ref[...], unroll=4)


def _lru_fwd_kernel(x_ref, xp_ref, xn_ref, h0_ref, cw_ref, cb_ref, wg_ref, bg_ref, lam_ref,
                    h_ref, xl_ref, hlast_ref, a_scr, b_scr, carry, *, n_tiles):
    i = pl.program_id(1)

    @pl.when(i == 0)
    def _():
        carry[...] = h0_ref[0]

    xl = _lru_conv(x_ref, xp_ref, xn_ref, cw_ref, cb_ref, i, n_tiles)
    xl_ref[0] = xl
    a, bb = _lru_coeffs(xl, wg_ref, bg_ref, lam_ref)
    _scan_tile(a, bb, a_scr, b_scr, h_ref.at[0], carry, reverse=False)
    hlast_ref[0] = carry[...]


def _lru_bwd_kernel(xl_ref, h0_ref, hf_ref, g_ref, wg_ref, bg_ref, lam_ref,
                    y_ref, hfirst_ref, a_scr, b_scr, h_scr, carry):
    i = pl.program_id(1)

    @pl.when(i == 0)
    def _():
        carry[...] = h0_ref[0]

    a, bb = _lru_coeffs(xl_ref[0], wg_ref, bg_ref, lam_ref)
    _scan_tile(a, bb, a_scr, b_scr, h_scr, carry, reverse=True)
    hfirst_ref[0] = carry[...]
    y_ref[0] = (g_ref[0] * (hf_ref[0] + h_scr[...])).astype(y_ref.dtype)


def _lru_mixer(z_lru, h0_f, h0_b, conv_w, conv_b, wg, layer, bg, lam, tm):
    b, l, c2 = z_lru.shape
    c = c2 // 2
    tm = min(tm, l)
    nt = l // tm
    state = jax.ShapeDtypeStruct((b, 1, c), F32)
    state_spec = pl.BlockSpec((1, 1, c), lambda bi, i: (bi, 0, 0))
    cp = _params(("parallel", "arbitrary"))
    gate = lambda d: [pl.BlockSpec((None, 1, c, c2), lambda bi, i: (layer, d, 0, 0)),
                      pl.BlockSpec((1, 1, c2), lambda bi, i: (d, 0, 0)),
                      pl.BlockSpec((1, 1, c), lambda bi, i: (d, 0, 0))]
    tile_f = pl.BlockSpec((1, tm, c), lambda bi, i: (bi, i, 0))

    prev_f, next_f = _halo_specs(tm, l, c, 0)
    hf, xl, hf_last = pl.pallas_call(
        functools.partial(_lru_fwd_kernel, n_tiles=nt),
        grid=(b, nt),
        in_specs=[tile_f, prev_f, next_f, state_spec, _const_spec(conv_w.shape), _const_spec((1, c))] + gate(0),
        out_specs=[tile_f, tile_f, state_spec],
        out_shape=[jax.ShapeDtypeStruct((b, l, c), F32), jax.ShapeDtypeStruct((b, l, c), F32), state],
        scratch_shapes=[pltpu.VMEM((tm, c), F32), pltpu.VMEM((tm, c), F32), pltpu.VMEM((1, c), F32)],
        compiler_params=cp,
        name="lru_fwd",
    )(z_lru, z_lru, z_lru, h0_f, conv_w, conv_b.reshape(1, c), wg, bg, lam)

    tile_b = lambda blk: pl.BlockSpec((1, tm, c), lambda bi, i: (bi, nt - 1 - i, blk))
    y, hb_first = pl.pallas_call(
        _lru_bwd_kernel,
        grid=(b, nt),
        in_specs=[tile_b(0), state_spec, tile_b(0), tile_b(1)] + gate(1),
        out_specs=[tile_b(0), state_spec],
        out_shape=[jax.ShapeDtypeStruct((b, l, c), BF16), state],
        scratch_shapes=[pltpu.VMEM((tm, c), F32), pltpu.VMEM((tm, c), F32), pltpu.VMEM((tm, c), F32),
                        pltpu.VMEM((1, c), F32)],
        compiler_params=cp,
        name="lru_bwd",
    )(xl, h0_b, hf, z_lru, wg, bg, lam)
    return y, hf_last, hb_first


def _cmlp_body(z, g_ref, b_ref, ws_ref, bs_ref, o_ref):
    tm, c = z.shape[0], z.shape[1] // 2
    gd = c // CMLP_GROUPS
    u = _gelu(z[:, :c])
    v = _gelu(z[:, c:])
    mu = jnp.mean(v, axis=-1, keepdims=True)
    var = jnp.mean(jnp.square(v - mu), axis=-1, keepdims=True)
    v = ((v - mu) * lax.rsqrt(var + EPS) * g_ref[...] + b_ref[...]).astype(BF16)
    bs = bs_ref[...]
    for ck in range(tm // CHUNK):
        rows = slice(ck * CHUNK, (ck + 1) * CHUNK)
        for g in range(CMLP_GROUPS):
            cols = slice(g * gd, (g + 1) * gd)
            mixed = _dot(ws_ref[g], v[rows, cols]) + bs[:, g:g + 1]
            o_ref[0, rows, cols] = (u[rows, cols] * mixed).astype(o_ref.dtype)


def _rms(x, g):
    return x * lax.rsqrt(jnp.mean(x * x, axis=-1, keepdims=True) + EPS) * g


def _mla_body(z, qg_ref, kvg_ref, wq_ref, wk_ref, wvt_ref, vone_ref,
              qc_ref, qs_ref, kcs_ref, q_ref, k_ref, vt_ref, q_scale):
    heads = q_ref.shape[2] // HEAD_PAD
    kvn = _rms(z[:, :KV_LORA], kvg_ref[...]).astype(BF16)
    qn = _rms(z[:, KV_LORA:KV_LORA + Q_LORA], qg_ref[...]).astype(BF16)
    t = z[:, KV_LORA + Q_LORA:] * kcs_ref[...]
    t = t + pltpu.roll(t, HEAD_PAD - QK_ROPE, 1)
    lane = lax.broadcasted_iota(jnp.int32, t.shape, 1)
    rope = jnp.where((lane >= QK_NOPE) & (lane < QK_NOPE + QK_ROPE), pltpu.roll(t, QK_NOPE, 1), 0.0)
    k = _dot(kvn, wk_ref[...]) + jnp.concatenate([rope] * heads, axis=1)
    k_ref[0] = k.astype(k_ref.dtype)
    v_t = lax.dot_general(wvt_ref[...], kvn, (((1,), (1,)), ((), ())), preferred_element_type=F32)
    vt_ref[0] = (v_t + vone_ref[...]).astype(vt_ref.dtype)
    qa = _dot(qn, wq_ref[...])
    qc = jnp.concatenate([qc_ref[...]] * heads, axis=1)
    qs = jnp.concatenate([qs_ref[...]] * heads, axis=1)
    q = qa * qc + pltpu.roll(qa, qa.shape[1] - QK_ROPE, 1) * qs
    q_ref[0] = (q * q_scale).astype(q_ref.dtype)


def _attn_kernel(q_ref, *refs, n_kv, tu, kb):
    kv_refs, o_ref = refs[:2 * n_kv], refs[2 * n_kv]
    s_buf, p_buf, mx_buf, acc_ref, ot_ref, qt_ref = refs[2 * n_kv + 1:]
    hv = o_ref.shape[2] // 2
    hr = kv_refs[1].shape[1] // 2
    tq = q_ref.shape[1]
    blocks = []
    off = 0
    for j in range(n_kv):
        k_ref, vt_ref = kv_refs[2 * j], kv_refs[2 * j + 1]
        lk = k_ref.shape[1]
        for s0 in range(0, lk, kb):
            blocks.append((k_ref, vt_ref, s0, off + s0, min(kb, lk - s0)))
        off += lk
    units = [(qi, hh) for qi in range(tq // tu) for hh in range(2)]
    n = len(units)

    qt_ref[...] = q_ref[0].T

    def stage_qk(u, bi):
        qi, hh = units[u]
        k_ref, _, s0, g0, size = blocks[bi]
        lanes = slice(hh * HEAD_PAD, (hh + 1) * HEAD_PAD)
        s = _dot(k_ref[0, s0:s0 + size, lanes], qt_ref[lanes, qi * tu:(qi + 1) * tu])
        s_buf[g0:g0 + size, :] = s
        m = jnp.max(s, axis=0, keepdims=True)
        mx_buf[u % 2] = m if bi == 0 else jnp.maximum(mx_buf[u % 2], m)

    def stage_exp(u, bi):
        _, _, _, g0, size = blocks[bi]
        p_buf[g0:g0 + size, :] = jnp.exp2(s_buf[g0:g0 + size, :] - mx_buf[u % 2]).astype(BF16)

    def stage_pv(u, bi):
        qi, hh = units[u]
        _, vt_ref, s0, g0, size = blocks[bi]
        pv = _dot(vt_ref[0, hh * hr:(hh + 1) * hr, s0:s0 + size], p_buf[g0:g0 + size, :])
        if bi > 0:
            pv = pv + acc_ref[...]
        if bi < len(blocks) - 1:
            acc_ref[...] = pv
        else:
            ot_ref[hh * hv:(hh + 1) * hv, qi * tu:(qi + 1) * tu] = pv[:hv] * (1.0 / pv[hv:hv + 1])

    for t in range(n + 2):
        for bi in range(len(blocks)):
            if 0 <= t - 2 < n:
                stage_pv(t - 2, bi)
            if 0 <= t - 1 < n:
                stage_exp(t - 1, bi)
            if t < n:
                stage_qk(t, bi)
    o_ref[0] = ot_ref[...].T.astype(o_ref.dtype)


def _attention(q, kvs, tq, tu):
    b, l, hp = q.shape
    heads = hp // HEAD_PAD
    tq = min(tq, l)
    tu = min(tu, tq)
    in_specs = [pl.BlockSpec((1, tq, 2 * HEAD_PAD), lambda bi, h, i: (bi, i, h))]
    args = [q]
    lk_total = 0
    for k, vt in kvs:
        lk = k.shape[1]
        lk_total += lk
        in_specs.append(pl.BlockSpec((1, lk, 2 * HEAD_PAD), lambda bi, h, i: (bi, 0, h)))
        in_specs.append(pl.BlockSpec((1, 2 * VT_ROWS, lk), lambda bi, h, i: (bi, h, 0)))
        args += [k, vt]
    return pl.pallas_call(
        functools.partial(_attn_kernel, n_kv=len(kvs), tu=tu, kb=KB_ATTN),
        grid=(b, heads // 2, l // tq),
        in_specs=in_specs,
        out_specs=pl.BlockSpec((1, tq, 2 * QK_NOPE), lambda bi, h, i: (bi, i, h)),
        out_shape=jax.ShapeDtypeStruct((b, l, heads * QK_NOPE), BF16),
        scratch_shapes=[pltpu.VMEM((lk_total, tu), F32), pltpu.VMEM((lk_total, tu), BF16),
                        pltpu.VMEM((2, 1, tu), F32), pltpu.VMEM((VT_ROWS, tu), F32),
                        pltpu.VMEM((2 * QK_NOPE, tq), F32),
                        pltpu.VMEM((2 * HEAD_PAD, tq), BF16)],
        compiler_params=_params(("parallel", "parallel", "parallel")),
        name="attention",
    )(*args)


def _merge_kernel(x_ref, h_ref, gt_ref, ab_ref, p_ref, pp_ref, pn_ref, cw_ref,
                  yb_ref, yc_ref, yd_ref, wg_ref, wb_ref, wo_ref, o_ref, *, n_tiles):
    i = pl.program_id(1)
    p = p_ref[0]
    p_prev = jnp.where(i > 0, pp_ref[0, SUBLANES - 1:, :], 0.0)
    p_next = jnp.where(i < n_tiles - 1, pn_ref[0, 0:1, :], 0.0)
    cw = cw_ref[...]
    conv = cw[0:1] * _shift_down(p, p_prev, 1) + cw[1:2] * p + cw[2:3] * _shift_up(p, p_next)
    y_a = (ab_ref[0] * conv).astype(BF16)

    x = x_ref[0]
    d = x.shape[1]
    h = h_ref[0]
    merged = None
    for n, y in enumerate((y_a, yb_ref[0], yc_ref[0], yd_ref[0])):
        gate = jax.nn.sigmoid(_dot(h, wg_ref[:, n * d:(n + 1) * d]))
        term = gate * _dot(y, wb_ref[n])
        merged = term if merged is None else merged + term
    o_ref[0] = x + gt_ref[0] * _dot(merged.astype(BF16), wo_ref[...])


def _merge(x, h, mod3, row_of, z_conv, conv_w, ys, wg, wb, wo, layer, tm):
    b, l, d = x.shape
    tm = min(tm, l)
    c = ys[0].shape[2]
    tok = lambda n: pl.BlockSpec((1, tm, n), lambda bi, i: (bi, i, 0))
    modv = lambda j: pl.BlockSpec((1, 1, d), lambda bi, i: (row_of(bi), 0, j))
    prev, nxt = _halo_specs(tm, l, c, 1)
    return pl.pallas_call(
        functools.partial(_merge_kernel, n_tiles=l // tm),
        grid=(b, l // tm),
        in_specs=[tok(d), tok(d), modv(2),
                  tok(c), pl.BlockSpec((1, tm, c), lambda bi, i: (bi, i, 1)), prev, nxt, _const_spec(conv_w.shape),
                  tok(c), tok(c), tok(c),
                  _layer_spec(wg, layer), _layer_spec(wb, layer), _layer_spec(wo, layer)],
        out_specs=tok(d),
        out_shape=jax.ShapeDtypeStruct((b, l, d), F32),
        compiler_params=_params(("parallel", "parallel")),
        name="merge",
    )(x, h, mod3, z_conv, z_conv, z_conv, z_conv, conv_w, *ys, wg, wb, wo)


def _ffn_kernel(x_ref, sh_ref, sc_ref, gt_ref, g_ref, w1_ref, w3_ref, w2_ref, fg_ref, o_ref, *, final_norm):
    x = x_ref[0]
    h = _norm_mod(x, g_ref[...], sh_ref[0], sc_ref[0]).astype(BF16)
    a = _dot(h, w1_ref[...])
    u = (a * jax.nn.sigmoid(a) * _dot(h, w3_ref[...])).astype(BF16)
    y = x + gt_ref[0] * _dot(u, w2_ref[...])
    if final_norm:
        y = _rms(y, fg_ref[...])
    o_ref[0] = y


def _ffn(x, mod3, row_of, g, w1, w3, w2, layer, final_g, final_norm, tm):
    b, l, d = x.shape
    tm = min(tm, l)
    tok = pl.BlockSpec((1, tm, d), lambda bi, i: (bi, i, 0))
    modv = lambda j: pl.BlockSpec((1, 1, d), lambda bi, i: (row_of(bi), 0, j))
    return pl.pallas_call(
        functools.partial(_ffn_kernel, final_norm=final_norm),
        grid=(b, l // tm),
        in_specs=[tok, modv(3), modv(4), modv(5), _const_spec((1, d)),
                  _layer_spec(w1, layer), _layer_spec(w3, layer), _layer_spec(w2, layer), _const_spec((1, d))],
        out_specs=tok,
        out_shape=jax.ShapeDtypeStruct((b, l, d), F32),
        compiler_params=_params(("parallel", "parallel")),
        name="ffn",
    )(x, mod3, mod3, mod3, g.reshape(1, d), w1, w3, w2, final_g.reshape(1, d))


def _rotate_half_cols(w):
    q = QK_ROPE // 4
    return jnp.concatenate([-w[:, q:2 * q], w[:, :q], -w[:, 3 * q:], w[:, 2 * q:3 * q]], axis=1)


def _block_diag(w):
    h, hd, _ = w.shape
    eye = jnp.eye(h, dtype=w.dtype)
    return (eye[:, None, :, None] * w[:, :, None, :]).reshape(h * hd, h * hd)


def _layer_weights(w_in_l, lru_w_a, lru_w_x, lru_b_a, lru_b_x, lru_lam, w_q_up, w_kv_up, cmlp_w_s,
                   w_branch, w_out, w_ff1, w_ff3, w_ff2):
    d = w_in_l.shape[0]
    c = d // 2
    o = {}
    i_lru_x, i_kv, i_kr, i_lru_g, i_q = 0, c, c + KV_LORA, c + KV_LORA + QK_ROPE, 2 * c + KV_LORA + QK_ROPE
    i_ab = i_q + Q_LORA
    i_ac, i_ax, i_cu, i_cv, i_gate = i_ab + c, i_ab + 2 * c, i_ab + 3 * c, i_ab + 4 * c, i_ab + 5 * c
    col = lambda s, n: w_in_l[:, s:s + n]
    k_rope = col(i_kr, QK_ROPE)
    mla_pad = jnp.zeros((d, HEAD_PAD - 2 * QK_ROPE), F32)
    o["w_lru"] = jnp.concatenate([col(i_lru_x, c), col(i_lru_g, c)], axis=1).astype(BF16)
    o["w_mla"] = jnp.concatenate([col(i_kv, KV_LORA), col(i_q, Q_LORA), k_rope, _rotate_half_cols(k_rope),
                                  mla_pad], axis=1).astype(BF16)
    o["w_conv"] = w_in_l[:, i_ab:i_cu].astype(BF16)
    o["w_cmlp"] = w_in_l[:, i_cu:i_gate].astype(BF16)
    o["w_gate"] = w_in_l[:, i_gate:].astype(BF16)
    o["lru_wg"] = jnp.stack([jnp.concatenate([_block_diag(lru_w_a[dd]), _block_diag(lru_w_x[dd])], axis=1)
                             for dd in range(2)]).astype(BF16)
    o["lru_bg"] = jnp.concatenate([lru_b_a, lru_b_x], axis=1).reshape(2, 1, 2 * c)
    o["lru_lam"] = lru_lam.reshape(2, 1, c)
    wq = w_q_up.reshape(Q_LORA, MLA_HEADS, QK_NOPE + QK_ROPE)
    rot = jnp.stack([_rotate_half_cols(wq[:, h, QK_NOPE:]) for h in range(MLA_HEADS)], axis=1)
    o["wq"] = jnp.concatenate([wq, rot], axis=2).reshape(Q_LORA, MLA_HEADS * HEAD_PAD).astype(BF16)
    wkv = w_kv_up.reshape(KV_LORA, MLA_HEADS, 2 * QK_NOPE)
    o["wk"] = jnp.concatenate([wkv[:, :, :QK_NOPE], jnp.zeros((KV_LORA, MLA_HEADS, HEAD_PAD - QK_NOPE), F32)],
                              axis=2).reshape(KV_LORA, MLA_HEADS * HEAD_PAD).astype(BF16)
    wv = jnp.concatenate([wkv[:, :, QK_NOPE:], jnp.zeros((KV_LORA, MLA_HEADS, ONES_ROWS), F32)], axis=2)
    o["wvt"] = wv.reshape(KV_LORA, MLA_HEADS * VT_ROWS).T.astype(BF16)
    o["w_s"] = cmlp_w_s.astype(BF16)
    o["w_branch"] = w_branch.astype(BF16)
    o["w_out"] = w_out.astype(BF16)
    o["w_ff1"] = w_ff1.astype(BF16)
    o["w_ff3"] = w_ff3.astype(BF16)
    o["w_ff2"] = w_ff2.astype(BF16)
    return o


def _rope_tables(n_lat):
    t = jnp.arange(n_lat)
    n_freq = QK_ROPE // 4
    inv = ROPE_THETA ** (-jnp.arange(n_freq, dtype=F32) / n_freq)
    ang_r = (t // GRID_W).astype(F32)[:, None] * inv
    ang_c = (t % GRID_W).astype(F32)[:, None] * inv
    cos = jnp.concatenate([jnp.cos(ang_r)] * 2 + [jnp.cos(ang_c)] * 2, axis=1)
    sin = jnp.concatenate([jnp.sin(ang_r)] * 2 + [jnp.sin(ang_c)] * 2, axis=1)
    ones = jnp.ones((n_lat, QK_NOPE), F32)
    z32 = jnp.zeros((n_lat, HEAD_PAD - QK_NOPE - QK_ROPE), F32)
    q_cos = jnp.concatenate([ones, cos, z32], axis=1)
    q_sin = jnp.concatenate([jnp.zeros((n_lat, QK_NOPE), F32), sin, z32], axis=1)
    k_cs = jnp.concatenate([cos, sin, jnp.zeros((n_lat, HEAD_PAD - 2 * QK_ROPE), F32)], axis=1)
    return q_cos, q_sin, k_cs


def _identity_tables(n):
    q_cos = jnp.concatenate([jnp.ones((n, QK_NOPE + QK_ROPE), F32),
                             jnp.zeros((n, HEAD_PAD - QK_NOPE - QK_ROPE), F32)], axis=1)
    q_sin = jnp.zeros((n, HEAD_PAD), F32)
    k_cs = jnp.concatenate([jnp.ones((n, QK_ROPE), F32), jnp.zeros((n, HEAD_PAD - QK_ROPE), F32)], axis=1)
    return q_cos, q_sin, k_cs


TM_PROJ = 512
TM_MIX = 1024
TQ_ATTN = 2048
TU_ATTN = 512
KB_ATTN = 1024


def kernel(x, c, ctx, c_ctx, w_mod, b_mod, norm1_g, norm2_g, w_in, conv_a_w, lru_conv_w, lru_conv_b,
           lru_w_a, lru_b_a, lru_w_x, lru_b_x, lru_lam, cmlp_ln_g, cmlp_ln_b, cmlp_w_s, cmlp_b_s,
           mla_q_norm_g, mla_kv_norm_g, mla_w_q_up, mla_w_kv_up, w_branch, w_out, w_ff1, w_ff3, w_ff2,
           final_norm_g):
    bsz, n_lat, d = x.shape
    n_ctx = ctx.shape[1]
    depth = w_in.shape[0]
    cw = d // 2
    q_scale = (QK_NOPE + QK_ROPE) ** -0.5 * LOG2E
    row_in_head = jnp.arange(MLA_HEADS * VT_ROWS) % VT_ROWS
    vone = (row_in_head >= QK_NOPE).astype(F32)[:, None]

    rows = -(-(bsz + 1) // SUBLANES) * SUBLANES
    s_rows = jnp.concatenate([c, c_ctx[None, :], jnp.zeros((rows - bsz - 1, d), F32)], axis=0)
    mod = _modulation(s_rows, w_mod, b_mod)
    lat_row = lambda bi: bi
    ctx_row = lambda bi: bsz

    lat_tabs = _rope_tables(n_lat)
    ctx_tabs = _identity_tables(n_ctx)
    zero_state = jnp.zeros((bsz, 1, cw), F32)

    w = jax.vmap(_layer_weights)(w_in, lru_w_a, lru_w_x, lru_b_a, lru_b_x, lru_lam, mla_w_q_up, mla_w_kv_up,
                                 cmlp_w_s, w_branch, w_out, w_ff1, w_ff3, w_ff2)

    xc = ctx
    for l in range(depth):
        last = l == depth - 1
        mod3 = mod[l].reshape(rows, 1, 6 * d)
        cmlp = (cmlp_ln_g[l], cmlp_ln_b[l], cmlp_b_s[l])
        mla = (mla_q_norm_g[l], mla_kv_norm_g[l], vone)
        lru = (lru_conv_w[l], lru_conv_b[l], w["lru_wg"], l, w["lru_bg"][l], w["lru_lam"][l], TM_MIX)
        gate_w = (w["w_gate"], w["w_branch"], w["w_out"], l, TM_PROJ)
        ffn_w = (w["w_ff1"], w["w_ff3"], w["w_ff2"], l, final_norm_g)

        hc, zc_lru, zc_conv, yc_c, qc, kc, vc = _inproj(xc, mod3, ctx_row, norm1_g[l], w, l, cmlp, mla, ctx_tabs,
                                                        q_scale, TM_PROJ)
        yc_b, hc_f, hc_b = _lru_mixer(zc_lru, zero_state, zero_state, *lru)

        h, z_lru, z_conv, y_c, q, k, v = _inproj(x, mod3, lat_row, norm1_g[l], w, l, cmlp, mla, lat_tabs,
                                                 q_scale, TM_PROJ)
        y_b, _, _ = _lru_mixer(z_lru, hc_f, hc_b, *lru)
        y_d = _attention(q, [(k, v), (kc, vc)], TQ_ATTN, TU_ATTN)
        x = _merge(x, h, mod3, lat_row, z_conv, conv_a_w[l], (y_b, y_c, y_d), *gate_w)
        x = _ffn(x, mod3, lat_row, norm2_g[l], *ffn_w, last, TM_PROJ)

        if not last:
            yc_d = _attention(qc, [(kc, vc)], TQ_ATTN, TU_ATTN)
            xc = _merge(xc, hc, mod3, ctx_row, zc_conv, conv_a_w[l], (yc_b, yc_c, yc_d), *gate_w)
            xc = _ffn(xc, mod3, ctx_row, norm2_g[l], *ffn_w, False, TM_PROJ)
    return x
```

```python
import functools

import jax
import jax.numpy as jnp
from jax import lax
from jax.experimental import pallas as pl
from jax.experimental.pallas import tpu as pltpu

F32 = jnp.float32
BF16 = jnp.bfloat16

EPS = 1e-6
GRID_W = 64
N_BRANCH = 4
LRU_HEADS = 8
LRU_C = 8.0
CMLP_GROUPS = 4
CHUNK = 128
MLA_HEADS = 8
QK_NOPE = 64
QK_ROPE = 32
Q_LORA = 384
KV_LORA = 256
ROPE_THETA = 10000.0
HEAD_PAD = 128
ONES_ROWS = 16
VT_ROWS = QK_NOPE + ONES_ROWS
LOG2E = 1.4426950408889634
SUBLANES = 8
VMEM_LIMIT = 56 * 1024 * 1024


def _params(sem, vmem=VMEM_LIMIT, flags=None):
    return pltpu.CompilerParams(dimension_semantics=sem, vmem_limit_bytes=vmem, flags=flags)


def _const_spec(shape):
    zeros = (0,) * len(shape)
    return pl.BlockSpec(shape, lambda *_: zeros, pipeline_mode=pl.Buffered(1))


def _layer_spec(stacked, layer):
    index = (layer,) + (0,) * (stacked.ndim - 1)
    return pl.BlockSpec((None,) + stacked.shape[1:], lambda *_: index, pipeline_mode=pl.Buffered(1))


def _gelu(x):
    return jax.nn.gelu(x)


def _norm_mod(x, g, shift, scale):
    gain = g * (1.0 + scale)
    return x * lax.rsqrt(jnp.mean(x * x, axis=-1, keepdims=True) + EPS) * gain + shift


def _dot(a, b):
    return jnp.dot(a, b, preferred_element_type=F32)


def _mod_kernel(s_ref, w_ref, b_ref, o_ref):
    s = s_ref[...]
    s = s * jax.nn.sigmoid(s)
    o_ref[0] = _dot(s.astype(BF16), w_ref[0].astype(BF16)) + b_ref[0]


def _modulation(s_rows, w_mod, b_mod):
    n_layer, d, d6 = w_mod.shape
    r = s_rows.shape[0]
    return pl.pallas_call(
        _mod_kernel,
        grid=(n_layer, d6 // d),
        in_specs=[
            pl.BlockSpec((r, d), lambda l, j: (0, 0)),
            pl.BlockSpec((1, d, d), lambda l, j: (l, 0, j)),
            pl.BlockSpec((1, 1, d), lambda l, j: (l, 0, j)),
        ],
        out_specs=pl.BlockSpec((1, r, d), lambda l, j: (l, 0, j)),
        out_shape=jax.ShapeDtypeStruct((n_layer, r, d6), F32),
        compiler_params=_params(("parallel", "parallel")),
        name="modulation",
    )(s_rows, w_mod, b_mod.reshape(n_layer, 1, d6))


def _inproj_kernel(x_ref, sh_ref, sc_ref, g_ref, w_lru, w_conv, w_cmlp, w_mla,
                   lng_ref, lnb_ref, ws_ref, bs_ref,
                   qg_ref, kvg_ref, wq_ref, wk_ref, wvt_ref, vone_ref, qc_ref, qs_ref, kcs_ref,
                   o_h, o_lru, o_conv, o_yc, q_ref, k_ref, vt_ref, *, q_scale):
    h = _norm_mod(x_ref[0], g_ref[...], sh_ref[0], sc_ref[0]).astype(BF16)
    o_h[0] = h
    zl = _dot(h, w_lru[...])
    zc = _dot(h, w_conv[...])
    c = zc.shape[1] // 3
    o_lru[0, :, :c] = zl[:, :c]
    o_lru[0, :, c:] = _gelu(zl[:, c:])
    o_conv[0, :, :c] = zc[:, :c]
    o_conv[0, :, c:] = zc[:, c:2 * c] * zc[:, 2 * c:]
    _cmlp_body(_dot(h, w_cmlp[...]), lng_ref, lnb_ref, ws_ref, bs_ref, o_yc)
    _mla_body(_dot(h, w_mla[...]), qg_ref, kvg_ref, wq_ref, wk_ref, wvt_ref, vone_ref,
              qc_ref, qs_ref, kcs_ref, q_ref, k_ref, vt_ref, q_scale)


def _inproj(x, mod3, row_of, g, w, layer, cmlp, mla, tabs, q_scale, tm):
    b, l, d = x.shape
    tm = min(tm, l)
    c = d // 2
    ln_g, ln_b, b_s = cmlp
    q_g, kv_g, vone = mla
    hp = w["wq"].shape[2]
    vrows = w["wvt"].shape[1]
    whole = lambda a: (a, _const_spec(a.shape))
    stacked = lambda name: (w[name], _layer_spec(w[name], layer))
    consts = [whole(g.reshape(1, d)), stacked("w_lru"), stacked("w_conv"), stacked("w_cmlp"), stacked("w_mla"),
              whole(ln_g.reshape(1, c)), whole(ln_b.reshape(1, c)), stacked("w_s"), whole(b_s.T),
              whole(q_g.reshape(1, Q_LORA)), whole(kv_g.reshape(1, KV_LORA)), stacked("wq"), stacked("wk"),
              stacked("wvt"), whole(vone)]
    tab = pl.BlockSpec((tm, HEAD_PAD), lambda bi, i: (i, 0))
    tok = lambda n: pl.BlockSpec((1, tm, n), lambda bi, i: (bi, i, 0))
    in_specs = ([tok(d),
                 pl.BlockSpec((1, 1, d), lambda bi, i: (row_of(bi), 0, 0)),
                 pl.BlockSpec((1, 1, d), lambda bi, i: (row_of(bi), 0, 1))]
                + [spec for _, spec in consts] + [tab, tab, tab])
    out_specs = [tok(d), tok(2 * c), tok(2 * c), tok(c), tok(hp), tok(hp),
                 pl.BlockSpec((1, vrows, tm), lambda bi, i: (bi, 0, i))]
    out_shape = [jax.ShapeDtypeStruct((b, l, d), BF16),
                 jax.ShapeDtypeStruct((b, l, 2 * c), F32), jax.ShapeDtypeStruct((b, l, 2 * c), F32),
                 jax.ShapeDtypeStruct((b, l, c), BF16), jax.ShapeDtypeStruct((b, l, hp), BF16),
                 jax.ShapeDtypeStruct((b, l, hp), BF16), jax.ShapeDtypeStruct((b, vrows, l), BF16)]
    return pl.pallas_call(
        functools.partial(_inproj_kernel, q_scale=q_scale),
        grid=(b, l // tm),
        in_specs=in_specs,
        out_specs=out_specs,
        out_shape=out_shape,
        compiler_params=_params(("parallel", "parallel")),
        name="inproj",
    )(x, mod3, mod3, *[a for a, _ in consts], *tabs)


def _halo_specs(tm, l, c, lane_blk):
    nb = l // SUBLANES
    per = tm // SUBLANES
    prev = pl.BlockSpec((1, SUBLANES, c), lambda bi, i: (bi, jnp.maximum(i * per - 1, 0), lane_blk))
    nxt = pl.BlockSpec((1, SUBLANES, c), lambda bi, i: (bi, jnp.minimum((i + 1) * per, nb - 1), lane_blk))
    return prev, nxt


def _shift_down(p, first_row, k):
    row = lax.broadcasted_iota(jnp.int32, p.shape, 0)
    out = pltpu.roll(p, k, 0)
    for j in range(k):
        out = jnp.where(row == j, first_row[j:j + 1, :], out)
    return out


def _shift_up(p, last_row):
    tm = p.shape[0]
    row = lax.broadcasted_iota(jnp.int32, p.shape, 0)
    return jnp.where(row == tm - 1, last_row, pltpu.roll(p, tm - 1, 0))


def _lru_conv(x_ref, xp_ref, xn_ref, cw_ref, cb_ref, i, n_tiles):
    x = x_ref[0]
    prev = jnp.where(i > 0, xp_ref[0, SUBLANES - 2:, :], 0.0)
    nxt = jnp.where(i < n_tiles - 1, xn_ref[0, 0:1, :], 0.0)
    cw = cw_ref[...]
    x_m1 = _shift_down(x, prev[1:2], 1)
    x_m2 = _shift_down(x, prev, 2)
    return cw[0:1] * x_m2 + cw[1:2] * x_m1 + cw[2:3] * x + cw[3:4] * _shift_up(x, nxt) + cb_ref[...]


def _lru_coeffs(xl, wg_ref, bg_ref, lam_ref):
    c = xl.shape[1]
    gates = _dot(xl.astype(BF16), wg_ref[0]) + bg_ref[0]
    r = jax.nn.sigmoid(gates[:, :c])
    gi = jax.nn.sigmoid(gates[:, c:])
    lam = lam_ref[0]
    softplus_neg = jnp.maximum(-lam, 0.0) + jnp.log1p(jnp.exp(-jnp.abs(lam)))
    a = jnp.exp2((-LRU_C * LOG2E * softplus_neg) * r)
    y = 1.0 - a * a
    root = jnp.where(y > 0.0, y * lax.rsqrt(y), 0.0)
    return a, root * (gi * xl)


def _scan_tile(a, bb, a_scr, b_scr, h_scr, carry_ref, reverse):
    tm, c = a.shape
    groups = tm // SUBLANES
    acc_a = a.reshape(groups, SUBLANES, c)
    acc_b = bb.reshape(groups, SUBLANES, c)
    r8 = lax.broadcasted_iota(jnp.int32, acc_a.shape, 1)
    for k in (1, 2, 4):
        if reverse:
            keep = r8 < SUBLANES - k
            shift = SUBLANES - k
        else:
            keep = r8 >= k
            shift = k
        a_sh = jnp.where(keep, pltpu.roll(acc_a, shift, 1), 1.0)
        b_sh = jnp.where(keep, pltpu.roll(acc_b, shift, 1), 0.0)
        acc_b = acc_a * b_sh + acc_b
        acc_a = acc_a * a_sh
    a_scr[...] = acc_a.reshape(tm, c)
    b_scr[...] = acc_b.reshape(tm, c)

    def body(gi, h_in):
        g = groups - 1 - gi if reverse else gi
        rows = pl.ds(pl.multiple_of(g * SUBLANES, SUBLANES), SUBLANES)
        h8 = b_scr[rows, :] + a_scr[rows, :] * h_in
        h_scr[rows, :] = h8
        return h8[0:1, :] if reverse else h8[SUBLANES - 1:, :]

    carry_ref[...] = lax.fori_loop(0, groups, body, carry_ref[...], unroll=4)


def _lru_fwd_kernel(x_ref, xp_ref, xn_ref, h0_ref, cw_ref, cb_ref, wg_ref, bg_ref, lam_ref,
                    h_ref, xl_ref, hlast_ref, a_scr, b_scr, carry, *, n_tiles):
    i = pl.program_id(1)

    @pl.when(i == 0)
    def _():
        carry[...] = h0_ref[0]

    xl = _lru_conv(x_ref, xp_ref, xn_ref, cw_ref, cb_ref, i, n_tiles)
    xl_ref[0] = xl
    a, bb = _lru_coeffs(xl, wg_ref, bg_ref, lam_ref)
    _scan_tile(a, bb, a_scr, b_scr, h_ref.at[0], carry, reverse=False)
    hlast_ref[0] = carry[...]


def _lru_bwd_kernel(xl_ref, h0_ref, hf_ref, g_ref, wg_ref, bg_ref, lam_ref,
                    y_ref, hfirst_ref, a_scr, b_scr, h_scr, carry):
    i = pl.program_id(1)

    @pl.when(i == 0)
    def _():
        carry[...] = h0_ref[0]

    a, bb = _lru_coeffs(xl_ref[0], wg_ref, bg_ref, lam_ref)
    _scan_tile(a, bb, a_scr, b_scr, h_scr, carry, reverse=True)
    hfirst_ref[0] = carry[...]
    y_ref[0] = (g_ref[0] * (hf_ref[0] + h_scr[...])).astype(y_ref.dtype)


def _lru_mixer(z_lru, h0_f, h0_b, conv_w, conv_b, wg, layer, bg, lam, tm):
    b, l, c2 = z_lru.shape
    c = c2 // 2
    tm = min(tm, l)
    nt = l // tm
    state = jax.ShapeDtypeStruct((b, 1, c), F32)
    state_spec = pl.BlockSpec((1, 1, c), lambda bi, i: (bi, 0, 0))
    cp = _params(("parallel", "arbitrary"))
    gate = lambda d: [pl.BlockSpec((None, 1, c, c2), lambda bi, i: (layer, d, 0, 0)),
                      pl.BlockSpec((1, 1, c2), lambda bi, i: (d, 0, 0)),
                      pl.BlockSpec((1, 1, c), lambda bi, i: (d, 0, 0))]
    tile_f = pl.BlockSpec((1, tm, c), lambda bi, i: (bi, i, 0))

    prev_f, next_f = _halo_specs(tm, l, c, 0)
    hf, xl, hf_last = pl.pallas_call(
        functools.partial(_lru_fwd_kernel, n_tiles=nt),
        grid=(b, nt),
        in_specs=[tile_f, prev_f, next_f, state_spec, _const_spec(conv_w.shape), _const_spec((1, c))] + gate(0),
        out_specs=[tile_f, tile_f, state_spec],
        out_shape=[jax.ShapeDtypeStruct((b, l, c), F32), jax.ShapeDtypeStruct((b, l, c), F32), state],
        scratch_shapes=[pltpu.VMEM((tm, c), F32), pltpu.VMEM((tm, c), F32), pltpu.VMEM((1, c), F32)],
        compiler_params=cp,
        name="lru_fwd",
    )(z_lru, z_lru, z_lru, h0_f, conv_w, conv_b.reshape(1, c), wg, bg, lam)

    tile_b = lambda blk: pl.BlockSpec((1, tm, c), lambda bi, i: (bi, nt - 1 - i, blk))
    y, hb_first = pl.pallas_call(
        _lru_bwd_kernel,
        grid=(b, nt),
        in_specs=[tile_b(0), state_spec, tile_b(0), tile_b(1)] + gate(1),
        out_specs=[tile_b(0), state_spec],
        out_shape=[jax.ShapeDtypeStruct((b, l, c), BF16), state],
        scratch_shapes=[pltpu.VMEM((tm, c), F32), pltpu.VMEM((tm, c), F32), pltpu.VMEM((tm, c), F32),
                        pltpu.VMEM((1, c), F32)],
        compiler_params=cp,
        name="lru_bwd",
    )(xl, h0_b, hf, z_lru, wg, bg, lam)
    return y, hf_last, hb_first


def _cmlp_body(z, g_ref, b_ref, ws_ref, bs_ref, o_ref):
    tm, c = z.shape[0], z.shape[1] // 2
    gd = c // CMLP_GROUPS
    u = _gelu(z[:, :c])
    v = _gelu(z[:, c:])
    mu = jnp.mean(v, axis=-1, keepdims=True)
    var = jnp.mean(jnp.square(v - mu), axis=-1, keepdims=True)
    v = ((v - mu) * lax.rsqrt(var + EPS) * g_ref[...] + b_ref[...]).astype(BF16)
    bs = bs_ref[...]
    for ck in range(tm // CHUNK):
        rows = slice(ck * CHUNK, (ck + 1) * CHUNK)
        for g in range(CMLP_GROUPS):
            cols = slice(g * gd, (g + 1) * gd)
            mixed = _dot(ws_ref[g], v[rows, cols]) + bs[:, g:g + 1]
            o_ref[0, rows, cols] = (u[rows, cols] * mixed).astype(o_ref.dtype)


def _rms(x, g):
    return x * lax.rsqrt(jnp.mean(x * x, axis=-1, keepdims=True) + EPS) * g


def _mla_body(z, qg_ref, kvg_ref, wq_ref, wk_ref, wvt_ref, vone_ref,
              qc_ref, qs_ref, kcs_ref, q_ref, k_ref, vt_ref, q_scale):
    heads = q_ref.shape[2] // HEAD_PAD
    kvn = _rms(z[:, :KV_LORA], kvg_ref[...]).astype(BF16)
    qn = _rms(z[:, KV_LORA:KV_LORA + Q_LORA], qg_ref[...]).astype(BF16)
    t = z[:, KV_LORA + Q_LORA:] * kcs_ref[...]
    t = t + pltpu.roll(t, HEAD_PAD - QK_ROPE, 1)
    lane = lax.broadcasted_iota(jnp.int32, t.shape, 1)
    rope = jnp.where((lane >= QK_NOPE) & (lane < QK_NOPE + QK_ROPE), pltpu.roll(t, QK_NOPE, 1), 0.0)
    k = _dot(kvn, wk_ref[...]) + jnp.concatenate([rope] * heads, axis=1)
    k_ref[0] = k.astype(k_ref.dtype)
    v_t = lax.dot_general(wvt_ref[...], kvn, (((1,), (1,)), ((), ())), preferred_element_type=F32)
    vt_ref[0] = (v_t + vone_ref[...]).astype(vt_ref.dtype)
    qa = _dot(qn, wq_ref[...])
    qc = jnp.concatenate([qc_ref[...]] * heads, axis=1)
    qs = jnp.concatenate([qs_ref[...]] * heads, axis=1)
    q = qa * qc + pltpu.roll(qa, qa.shape[1] - QK_ROPE, 1) * qs
    q_ref[0] = (q * q_scale).astype(q_ref.dtype)


def _attn_kernel(q_ref, *refs, n_kv, tu, kb):
    kv_refs, o_ref = refs[:2 * n_kv], refs[2 * n_kv]
    s_buf, p_buf, mx_buf, acc_ref, ot_ref, qt_ref = refs[2 * n_kv + 1:]
    hv = o_ref.shape[2] // 2
    hr = kv_refs[1].shape[1] // 2
    tq = q_ref.shape[1]
    blocks = []
    off = 0
    for j in range(n_kv):
        k_ref, vt_ref = kv_refs[2 * j], kv_refs[2 * j + 1]
        lk = k_ref.shape[1]
        for s0 in range(0, lk, kb):
            blocks.append((k_ref, vt_ref, s0, off + s0, min(kb, lk - s0)))
        off += lk
    nq = tq // tu

    for qi in range(nq):
        qt_ref[qi] = q_ref[0, qi * tu:(qi + 1) * tu, :].T

    def stage_qk(qi, hh, bi):
        k_ref, _, s0, g0, size = blocks[bi]
        lanes = slice(hh * HEAD_PAD, (hh + 1) * HEAD_PAD)
        s = _dot(k_ref[0, s0:s0 + size, lanes], qt_ref[qi, lanes, :])
        s_buf[g0:g0 + size, :] = s
        m = jnp.max(s, axis=0, keepdims=True)
        mx_buf[hh] = m if bi == 0 else jnp.maximum(mx_buf[hh], m)

    def stage_exp(hh, bi):
        _, _, _, g0, size = blocks[bi]
        p_buf[g0:g0 + size, :] = jnp.exp2(s_buf[g0:g0 + size, :] - mx_buf[hh]).astype(BF16)

    def stage_pv(hh, bi):
        _, vt_ref, s0, g0, size = blocks[bi]
        pv = _dot(vt_ref[0, hh * hr:(hh + 1) * hr, s0:s0 + size], p_buf[g0:g0 + size, :])
        if bi > 0:
            pv = pv + acc_ref[...]
        if bi < len(blocks) - 1:
            acc_ref[...] = pv
        else:
            ot_ref[hh * hv:(hh + 1) * hv, :] = pv[:hv] * (1.0 / pv[hv:hv + 1])

    def slot(pv_head, exp_head, qk):
        for bi in range(len(blocks)):
            if pv_head is not None:
                stage_pv(pv_head, bi)
            if exp_head is not None:
                stage_exp(exp_head, bi)
            if qk is not None:
                stage_qk(qk[0], qk[1], bi)

    def write_tile(qi):
        rows = pl.ds(pl.multiple_of(qi * tu, tu), tu)
        o_ref[0, rows, :] = ot_ref[...].T.astype(o_ref.dtype)

    slot(None, None, (0, 0))
    slot(None, 0, (0, 1))

    def body(qi, carry):
        slot(0, 1, (qi, 0))
        slot(1, 0, (qi, 1))
        write_tile(qi - 1)
        return carry

    lax.fori_loop(1, nq, body, 0)
    slot(0, 1, None)
    slot(1, None, None)
    write_tile(nq - 1)


def _attention(q, kvs, tq, tu):
    b, l, hp = q.shape
    heads = hp // HEAD_PAD
    tq = min(tq, l)
    tu = min(tu, tq)
    in_specs = [pl.BlockSpec((1, tq, 2 * HEAD_PAD), lambda bi, h, i: (bi, i, h))]
    args = [q]
    lk_total = 0
    for k, vt in kvs:
        lk = k.shape[1]
        lk_total += lk
        in_specs.append(pl.BlockSpec((1, lk, 2 * HEAD_PAD), lambda bi, h, i: (bi, 0, h)))
        in_specs.append(pl.BlockSpec((1, 2 * VT_ROWS, lk), lambda bi, h, i: (bi, h, 0)))
        args += [k, vt]
    return pl.pallas_call(
        functools.partial(_attn_kernel, n_kv=len(kvs), tu=tu, kb=KB_ATTN),
        grid=(b, heads // 2, l // tq),
        in_specs=in_specs,
        out_specs=pl.BlockSpec((1, tq, 2 * QK_NOPE), lambda bi, h, i: (bi, i, h)),
        out_shape=jax.ShapeDtypeStruct((b, l, heads * QK_NOPE), BF16),
        scratch_shapes=[pltpu.VMEM((lk_total, tu), F32), pltpu.VMEM((lk_total, tu), BF16),
                        pltpu.VMEM((2, 1, tu), F32), pltpu.VMEM((VT_ROWS, tu), F32),
                        pltpu.VMEM((2 * QK_NOPE, tu), F32),
                        pltpu.VMEM((tq // tu, 2 * HEAD_PAD, tu), BF16)],
        compiler_params=_params(("parallel", "parallel", "parallel")),
        name="attention",
    )(*args)


def _merge_kernel(x_ref, h_ref, gt_ref, ab_ref, p_ref, pp_ref, pn_ref, cw_ref,
                  yb_ref, yc_ref, yd_ref, wg_ref, wb_ref, wo_ref, o_ref, *, n_tiles):
    i = pl.program_id(1)
    p = p_ref[0]
    p_prev = jnp.where(i > 0, pp_ref[0, SUBLANES - 1:, :], 0.0)
    p_next = jnp.where(i < n_tiles - 1, pn_ref[0, 0:1, :], 0.0)
    cw = cw_ref[...]
    conv = cw[0:1] * _shift_down(p, p_prev, 1) + cw[1:2] * p + cw[2:3] * _shift_up(p, p_next)
    y_a = (ab_ref[0] * conv).astype(BF16)

    x = x_ref[0]
    d = x.shape[1]
    h = h_ref[0]
    merged = None
    for n, y in enumerate((y_a, yb_ref[0], yc_ref[0], yd_ref[0])):
        gate = jax.nn.sigmoid(_dot(h, wg_ref[:, n * d:(n + 1) * d]))
        term = gate * _dot(y, wb_ref[n])
        merged = term if merged is None else merged + term
    o_ref[0] = x + gt_ref[0] * _dot(merged.astype(BF16), wo_ref[...])


def _merge(x, h, mod3, row_of, z_conv, conv_w, ys, wg, wb, wo, layer, tm):
    b, l, d = x.shape
    tm = min(tm, l)
    c = ys[0].shape[2]
    tok = lambda n: pl.BlockSpec((1, tm, n), lambda bi, i: (bi, i, 0))
    modv = lambda j: pl.BlockSpec((1, 1, d), lambda bi, i: (row_of(bi), 0, j))
    prev, nxt = _halo_specs(tm, l, c, 1)
    return pl.pallas_call(
        functools.partial(_merge_kernel, n_tiles=l // tm),
        grid=(b, l // tm),
        in_specs=[tok(d), tok(d), modv(2),
                  tok(c), pl.BlockSpec((1, tm, c), lambda bi, i: (bi, i, 1)), prev, nxt, _const_spec(conv_w.shape),
                  tok(c), tok(c), tok(c),
                  _layer_spec(wg, layer), _layer_spec(wb, layer), _layer_spec(wo, layer)],
        out_specs=tok(d),
        out_shape=jax.ShapeDtypeStruct((b, l, d), F32),
        compiler_params=_params(("parallel", "parallel")),
        name="merge",
    )(x, h, mod3, z_conv, z_conv, z_conv, z_conv, conv_w, *ys, wg, wb, wo)


def _ffn_kernel(x_ref, sh_ref, sc_ref, gt_ref, g_ref, w1_ref, w3_ref, w2_ref, fg_ref, o_ref, *, final_norm):
    x = x_ref[0]
    h = _norm_mod(x, g_ref[...], sh_ref[0], sc_ref[0]).astype(BF16)
    a = _dot(h, w1_ref[...])
    u = (a * jax.nn.sigmoid(a) * _dot(h, w3_ref[...])).astype(BF16)
    y = x + gt_ref[0] * _dot(u, w2_ref[...])
    if final_norm:
        y = _rms(y, fg_ref[...])
    o_ref[0] = y


def _ffn(x, mod3, row_of, g, w1, w3, w2, layer, final_g, final_norm, tm):
    b, l, d = x.shape
    tm = min(tm, l)
    tok = pl.BlockSpec((1, tm, d), lambda bi, i: (bi, i, 0))
    modv = lambda j: pl.BlockSpec((1, 1, d), lambda bi, i: (row_of(bi), 0, j))
    return pl.pallas_call(
        functools.partial(_ffn_kernel, final_norm=final_norm),
        grid=(b, l // tm),
        in_specs=[tok, modv(3), modv(4), modv(5), _const_spec((1, d)),
                  _layer_spec(w1, layer), _layer_spec(w3, layer), _layer_spec(w2, layer), _const_spec((1, d))],
        out_specs=tok,
        out_shape=jax.ShapeDtypeStruct((b, l, d), F32),
        compiler_params=_params(("parallel", "parallel")),
        name="ffn",
    )(x, mod3, mod3, mod3, g.reshape(1, d), w1, w3, w2, final_g.reshape(1, d))


def _rotate_half_cols(w):
    q = QK_ROPE // 4
    return jnp.concatenate([-w[:, q:2 * q], w[:, :q], -w[:, 3 * q:], w[:, 2 * q:3 * q]], axis=1)


def _block_diag(w):
    h, hd, _ = w.shape
    eye = jnp.eye(h, dtype=w.dtype)
    return (eye[:, None, :, None] * w[:, :, None, :]).reshape(h * hd, h * hd)


def _layer_weights(w_in_l, lru_w_a, lru_w_x, lru_b_a, lru_b_x, lru_lam, w_q_up, w_kv_up, cmlp_w_s,
                   w_branch, w_out, w_ff1, w_ff3, w_ff2):
    d = w_in_l.shape[0]
    c = d // 2
    o = {}
    i_lru_x, i_kv, i_kr, i_lru_g, i_q = 0, c, c + KV_LORA, c + KV_LORA + QK_ROPE, 2 * c + KV_LORA + QK_ROPE
    i_ab = i_q + Q_LORA
    i_ac, i_ax, i_cu, i_cv, i_gate = i_ab + c, i_ab + 2 * c, i_ab + 3 * c, i_ab + 4 * c, i_ab + 5 * c
    col = lambda s, n: w_in_l[:, s:s + n]
    k_rope = col(i_kr, QK_ROPE)
    mla_pad = jnp.zeros((d, HEAD_PAD - 2 * QK_ROPE), F32)
    o["w_lru"] = jnp.concatenate([col(i_lru_x, c), col(i_lru_g, c)], axis=1).astype(BF16)
    o["w_mla"] = jnp.concatenate([col(i_kv, KV_LORA), col(i_q, Q_LORA), k_rope, _rotate_half_cols(k_rope),
                                  mla_pad], axis=1).astype(BF16)
    o["w_conv"] = w_in_l[:, i_ab:i_cu].astype(BF16)
    o["w_cmlp"] = w_in_l[:, i_cu:i_gate].astype(BF16)
    o["w_gate"] = w_in_l[:, i_gate:].astype(BF16)
    o["lru_wg"] = jnp.stack([jnp.concatenate([_block_diag(lru_w_a[dd]), _block_diag(lru_w_x[dd])], axis=1)
                             for dd in range(2)]).astype(BF16)
    o["lru_bg"] = jnp.concatenate([lru_b_a, lru_b_x], axis=1).reshape(2, 1, 2 * c)
    o["lru_lam"] = lru_lam.reshape(2, 1, c)
    wq = w_q_up.reshape(Q_LORA, MLA_HEADS, QK_NOPE + QK_ROPE)
    rot = jnp.stack([_rotate_half_cols(wq[:, h, QK_NOPE:]) for h in range(MLA_HEADS)], axis=1)
    o["wq"] = jnp.concatenate([wq, rot], axis=2).reshape(Q_LORA, MLA_HEADS * HEAD_PAD).astype(BF16)
    wkv = w_kv_up.reshape(KV_LORA, MLA_HEADS, 2 * QK_NOPE)
    o["wk"] = jnp.concatenate([wkv[:, :, :QK_NOPE], jnp.zeros((KV_LORA, MLA_HEADS, HEAD_PAD - QK_NOPE), F32)],
                              axis=2).reshape(KV_LORA, MLA_HEADS * HEAD_PAD).astype(BF16)
    wv = jnp.concatenate([wkv[:, :, QK_NOPE:], jnp.zeros((KV_LORA, MLA_HEADS, ONES_ROWS), F32)], axis=2)
    o["wvt"] = wv.reshape(KV_LORA, MLA_HEADS * VT_ROWS).T.astype(BF16)
    o["w_s"] = cmlp_w_s.astype(BF16)
    o["w_branch"] = w_branch.astype(BF16)
    o["w_out"] = w_out.astype(BF16)
    o["w_ff1"] = w_ff1.astype(BF16)
    o["w_ff3"] = w_ff3.astype(BF16)
    o["w_ff2"] = w_ff2.astype(BF16)
    return o


def _rope_tables(n_lat):
    t = jnp.arange(n_lat)
    n_freq = QK_ROPE // 4
    inv = ROPE_THETA ** (-jnp.arange(n_freq, dtype=F32) / n_freq)
    ang_r = (t // GRID_W).astype(F32)[:, None] * inv
    ang_c = (t % GRID_W).astype(F32)[:, None] * inv
    cos = jnp.concatenate([jnp.cos(ang_r)] * 2 + [jnp.cos(ang_c)] * 2, axis=1)
    sin = jnp.concatenate([jnp.sin(ang_r)] * 2 + [jnp.sin(ang_c)] * 2, axis=1)
    ones = jnp.ones((n_lat, QK_NOPE), F32)
    z32 = jnp.zeros((n_lat, HEAD_PAD - QK_NOPE - QK_ROPE), F32)
    q_cos = jnp.concatenate([ones, cos, z32], axis=1)
    q_sin = jnp.concatenate([jnp.zeros((n_lat, QK_NOPE), F32), sin, z32], axis=1)
    k_cs = jnp.concatenate([cos, sin, jnp.zeros((n_lat, HEAD_PAD - 2 * QK_ROPE), F32)], axis=1)
    return q_cos, q_sin, k_cs


def _identity_tables(n):
    q_cos = jnp.concatenate([jnp.ones((n, QK_NOPE + QK_ROPE), F32),
                             jnp.zeros((n, HEAD_PAD - QK_NOPE - QK_ROPE), F32)], axis=1)
    q_sin = jnp.zeros((n, HEAD_PAD), F32)
    k_cs = jnp.concatenate([jnp.ones((n, QK_ROPE), F32), jnp.zeros((n, HEAD_PAD - QK_ROPE), F32)], axis=1)
    return q_cos, q_sin, k_cs


TM_PROJ = 512
TM_MIX = 1024
TQ_ATTN = 4096
TU_ATTN = 512
KB_ATTN = 1024


def kernel(x, c, ctx, c_ctx, w_mod, b_mod, norm1_g, norm2_g, w_in, conv_a_w, lru_conv_w, lru_conv_b,
           lru_w_a, lru_b_a, lru_w_x, lru_b_x, lru_lam, cmlp_ln_g, cmlp_ln_b, cmlp_w_s, cmlp_b_s,
           mla_q_norm_g, mla_kv_norm_g, mla_w_q_up, mla_w_kv_up, w_branch, w_out, w_ff1, w_ff3, w_ff2,
           final_norm_g):
    bsz, n_lat, d = x.shape
    n_ctx = ctx.shape[1]
    depth = w_in.shape[0]
    cw = d // 2
    q_scale = (QK_NOPE + QK_ROPE) ** -0.5 * LOG2E
    row_in_head = jnp.arange(MLA_HEADS * VT_ROWS) % VT_ROWS
    vone = (row_in_head >= QK_NOPE).astype(F32)[:, None]

    rows = -(-(bsz + 1) // SUBLANES) * SUBLANES
    s_rows = jnp.concatenate([c, c_ctx[None, :], jnp.zeros((rows - bsz - 1, d), F32)], axis=0)
    mod = _modulation(s_rows, w_mod, b_mod)
    lat_row = lambda bi: bi
    ctx_row = lambda bi: bsz

    lat_tabs = _rope_tables(n_lat)
    ctx_tabs = _identity_tables(n_ctx)
    zero_state = jnp.zeros((bsz, 1, cw), F32)

    w = jax.vmap(_layer_weights)(w_in, lru_w_a, lru_w_x, lru_b_a, lru_b_x, lru_lam, mla_w_q_up, mla_w_kv_up,
                                 cmlp_w_s, w_branch, w_out, w_ff1, w_ff3, w_ff2)

    xc = ctx
    for l in range(depth):
        last = l == depth - 1
        mod3 = mod[l].reshape(rows, 1, 6 * d)
        cmlp = (cmlp_ln_g[l], cmlp_ln_b[l], cmlp_b_s[l])
        mla = (mla_q_norm_g[l], mla_kv_norm_g[l], vone)
        lru = (lru_conv_w[l], lru_conv_b[l], w["lru_wg"], l, w["lru_bg"][l], w["lru_lam"][l], TM_MIX)
        gate_w = (w["w_gate"], w["w_branch"], w["w_out"], l, TM_PROJ)
        ffn_w = (w["w_ff1"], w["w_ff3"], w["w_ff2"], l, final_norm_g)

        hc, zc_lru, zc_conv, yc_c, qc, kc, vc = _inproj(xc, mod3, ctx_row, norm1_g[l], w, l, cmlp, mla, ctx_tabs,
                                                        q_scale, TM_PROJ)
        yc_b, hc_f, hc_b = _lru_mixer(zc_lru, zero_state, zero_state, *lru)

        h, z_lru, z_conv, y_c, q, k, v = _inproj(x, mod3, lat_row, norm1_g[l], w, l, cmlp, mla, lat_tabs,
                                                 q_scale, TM_PROJ)
        y_b, _, _ = _lru_mixer(z_lru, hc_f, hc_b, *lru)
        y_d = _attention(q, [(k, v), (kc, vc)], TQ_ATTN, TU_ATTN)
        x = _merge(x, h, mod3, lat_row, z_conv, conv_a_w[l], (y_b, y_c, y_d), *gate_w)
        x = _ffn(x, mod3, lat_row, norm2_g[l], *ffn_w, last, TM_PROJ)

        if not last:
            yc_d = _attention(qc, [(kc, vc)], TQ_ATTN, TU_ATTN)
            xc = _merge(xc, hc, mod3, ctx_row, zc_conv, conv_a_w[l], (yc_b, yc_c, yc_d), *gate_w)
            xc = _ffn(xc, mod3, ctx_row, norm2_g[l], *ffn_w, False, TM_PROJ)
    return x
```

```python
import functools

import jax
import jax.numpy as jnp
from jax import lax
from jax.experimental import pallas as pl
from jax.experimental.pallas import tpu as pltpu

F32 = jnp.float32
BF16 = jnp.bfloat16

EPS = 1e-6
GRID_W = 64
N_BRANCH = 4
LRU_HEADS = 8
LRU_C = 8.0
CMLP_GROUPS = 4
CHUNK = 128
MLA_HEADS = 8
QK_NOPE = 64
QK_ROPE = 32
Q_LORA = 384
KV_LORA = 256
ROPE_THETA = 10000.0
HEAD_PAD = 128
ONES_ROWS = 16
VT_ROWS = QK_NOPE + ONES_ROWS
LOG2E = 1.4426950408889634
SUBLANES = 8
VMEM_LIMIT = 56 * 1024 * 1024


def _params(sem, vmem=VMEM_LIMIT, flags=None):
    return pltpu.CompilerParams(dimension_semantics=sem, vmem_limit_bytes=vmem, flags=flags)


def _const_spec(shape):
    zeros = (0,) * len(shape)
    return pl.BlockSpec(shape, lambda *_: zeros, pipeline_mode=pl.Buffered(1))


def _layer_spec(stacked, layer):
    index = (layer,) + (0,) * (stacked.ndim - 1)
    return pl.BlockSpec((None,) + stacked.shape[1:], lambda *_: index, pipeline_mode=pl.Buffered(1))


def _gelu(x):
    return jax.nn.gelu(x)


def _norm_mod(x, g, shift, scale):
    gain = g * (1.0 + scale)
    return x * lax.rsqrt(jnp.mean(x * x, axis=-1, keepdims=True) + EPS) * gain + shift


def _dot(a, b):
    return jnp.dot(a, b, preferred_element_type=F32)


def _mod_kernel(s_ref, w_ref, b_ref, o_ref):
    s = s_ref[...]
    s = s * jax.nn.sigmoid(s)
    o_ref[0] = _dot(s.astype(BF16), w_ref[0].astype(BF16)) + b_ref[0]


def _modulation(s_rows, w_mod, b_mod):
    n_layer, d, d6 = w_mod.shape
    r = s_rows.shape[0]
    return pl.pallas_call(
        _mod_kernel,
        grid=(n_layer, d6 // d),
        in_specs=[
            pl.BlockSpec((r, d), lambda l, j: (0, 0)),
            pl.BlockSpec((1, d, d), lambda l, j: (l, 0, j)),
            pl.BlockSpec((1, 1, d), lambda l, j: (l, 0, j)),
        ],
        out_specs=pl.BlockSpec((1, r, d), lambda l, j: (l, 0, j)),
        out_shape=jax.ShapeDtypeStruct((n_layer, r, d6), F32),
        compiler_params=_params(("parallel", "parallel")),
        name="modulation",
    )(s_rows, w_mod, b_mod.reshape(n_layer, 1, d6))


def _inproj_kernel(x_ref, sh_ref, sc_ref, g_ref, w_lru, w_conv, w_cmlp, w_mla,
                   lng_ref, lnb_ref, ws_ref, bs_ref,
                   qg_ref, kvg_ref, wq_ref, wk_ref, wvt_ref, vone_ref, qc_ref, qs_ref, kcs_ref,
                   o_h, o_lru, o_conv, o_yc, q_ref, k_ref, vt_ref, *, q_scale):
    h = _norm_mod(x_ref[0], g_ref[...], sh_ref[0], sc_ref[0]).astype(BF16)
    o_h[0] = h
    zl = _dot(h, w_lru[...])
    zc = _dot(h, w_conv[...])
    c = zc.shape[1] // 3
    o_lru[0, :, :c] = zl[:, :c]
    o_lru[0, :, c:] = _gelu(zl[:, c:])
    o_conv[0, :, :c] = zc[:, :c]
    o_conv[0, :, c:] = zc[:, c:2 * c] * zc[:, 2 * c:]
    _cmlp_body(_dot(h, w_cmlp[...]), lng_ref, lnb_ref, ws_ref, bs_ref, o_yc)
    _mla_body(_dot(h, w_mla[...]), qg_ref, kvg_ref, wq_ref, wk_ref, wvt_ref, vone_ref,
              qc_ref, qs_ref, kcs_ref, q_ref, k_ref, vt_ref, q_scale)


def _inproj(x, mod3, row_of, g, w, layer, cmlp, mla, tabs, q_scale, tm):
    b, l, d = x.shape
    tm = min(tm, l)
    c = d // 2
    ln_g, ln_b, b_s = cmlp
    q_g, kv_g, vone = mla
    hp = w["wq"].shape[2]
    vrows = w["wvt"].shape[1]
    whole = lambda a: (a, _const_spec(a.shape))
    stacked = lambda name: (w[name], _layer_spec(w[name], layer))
    consts = [whole(g.reshape(1, d)), stacked("w_lru"), stacked("w_conv"), stacked("w_cmlp"), stacked("w_mla"),
              whole(ln_g.reshape(1, c)), whole(ln_b.reshape(1, c)), stacked("w_s"), whole(b_s.T),
              whole(q_g.reshape(1, Q_LORA)), whole(kv_g.reshape(1, KV_LORA)), stacked("wq"), stacked("wk"),
              stacked("wvt"), whole(vone)]
    tab = pl.BlockSpec((tm, HEAD_PAD), lambda bi, i: (i, 0))
    tok = lambda n: pl.BlockSpec((1, tm, n), lambda bi, i: (bi, i, 0))
    in_specs = ([tok(d),
                 pl.BlockSpec((1, 1, d), lambda bi, i: (row_of(bi), 0, 0)),
                 pl.BlockSpec((1, 1, d), lambda bi, i: (row_of(bi), 0, 1))]
                + [spec for _, spec in consts] + [tab, tab, tab])
    out_specs = [tok(d), tok(2 * c), tok(2 * c), tok(c), tok(hp), tok(hp),
                 pl.BlockSpec((1, vrows, tm), lambda bi, i: (bi, 0, i))]
    out_shape = [jax.ShapeDtypeStruct((b, l, d), BF16),
                 jax.ShapeDtypeStruct((b, l, 2 * c), F32), jax.ShapeDtypeStruct((b, l, 2 * c), F32),
                 jax.ShapeDtypeStruct((b, l, c), BF16), jax.ShapeDtypeStruct((b, l, hp), BF16),
                 jax.ShapeDtypeStruct((b, l, hp), BF16), jax.ShapeDtypeStruct((b, vrows, l), BF16)]
    return pl.pallas_call(
        functools.partial(_inproj_kernel, q_scale=q_scale),
        grid=(b, l // tm),
        in_specs=in_specs,
        out_specs=out_specs,
        out_shape=out_shape,
        compiler_params=_params(("parallel", "parallel")),
        name="inproj",
    )(x, mod3, mod3, *[a for a, _ in consts], *tabs)


def _halo_specs(tm, l, c, lane_blk):
    nb = l // SUBLANES
    per = tm // SUBLANES
    prev = pl.BlockSpec((1, SUBLANES, c), lambda bi, i: (bi, jnp.maximum(i * per - 1, 0), lane_blk))
    nxt = pl.BlockSpec((1, SUBLANES, c), lambda bi, i: (bi, jnp.minimum((i + 1) * per, nb - 1), lane_blk))
    return prev, nxt


def _shift_down(p, first_row, k):
    row = lax.broadcasted_iota(jnp.int32, p.shape, 0)
    out = pltpu.roll(p, k, 0)
    for j in range(k):
        out = jnp.where(row == j, first_row[j:j + 1, :], out)
    return out


def _shift_up(p, last_row):
    tm = p.shape[0]
    row = lax.broadcasted_iota(jnp.int32, p.shape, 0)
    return jnp.where(row == tm - 1, last_row, pltpu.roll(p, tm - 1, 0))


def _lru_conv(x_ref, xp_ref, xn_ref, cw_ref, cb_ref, i, n_tiles):
    x = x_ref[0]
    prev = jnp.where(i > 0, xp_ref[0, SUBLANES - 2:, :], 0.0)
    nxt = jnp.where(i < n_tiles - 1, xn_ref[0, 0:1, :], 0.0)
    cw = cw_ref[...]
    x_m1 = _shift_down(x, prev[1:2], 1)
    x_m2 = _shift_down(x, prev, 2)
    return cw[0:1] * x_m2 + cw[1:2] * x_m1 + cw[2:3] * x + cw[3:4] * _shift_up(x, nxt) + cb_ref[...]


def _lru_coeffs(xl, wg_ref, bg_ref, lam_ref):
    c = xl.shape[1]
    gates = _dot(xl.astype(BF16), wg_ref[0]) + bg_ref[0]
    r = jax.nn.sigmoid(gates[:, :c])
    gi = jax.nn.sigmoid(gates[:, c:])
    lam = lam_ref[0]
    softplus_neg = jnp.maximum(-lam, 0.0) + jnp.log1p(jnp.exp(-jnp.abs(lam)))
    a = jnp.exp2((-LRU_C * LOG2E * softplus_neg) * r)
    y = 1.0 - a * a
    root = jnp.where(y > 0.0, y * lax.rsqrt(y), 0.0)
    return a, root * (gi * xl)


def _scan_tile(a, bb, a_scr, b_scr, h_scr, carry_ref, reverse):
    tm, c = a.shape
    groups = tm // SUBLANES
    acc_a = a.reshape(groups, SUBLANES, c)
    acc_b = bb.reshape(groups, SUBLANES, c)
    r8 = lax.broadcasted_iota(jnp.int32, acc_a.shape, 1)
    for k in (1, 2, 4):
        if reverse:
            keep = r8 < SUBLANES - k
            shift = SUBLANES - k
        else:
            keep = r8 >= k
            shift = k
        a_sh = jnp.where(keep, pltpu.roll(acc_a, shift, 1), 1.0)
        b_sh = jnp.where(keep, pltpu.roll(acc_b, shift, 1), 0.0)
        acc_b = acc_a * b_sh + acc_b
        acc_a = acc_a * a_sh
    a_scr[...] = acc_a.reshape(tm, c)
    b_scr[...] = acc_b.reshape(tm, c)

    def body(gi, h_in):
        g = groups - 1 - gi if reverse else gi
        rows = pl.ds(pl.multiple_of(g * SUBLANES, SUBLANES), SUBLANES)
        h8 = b_scr[rows, :] + a_scr[rows, :] * h_in
        h_scr[rows, :] = h8
        return h8[0:1, :] if reverse else h8[SUBLANES - 1:, :]

    carry_ref[...] = lax.fori_loop(0, groups, body, carry_ref[...], unroll=4)


def _lru_fwd_kernel(x_ref, xp_ref, xn_ref, h0_ref, cw_ref, cb_ref, wg_ref, bg_ref, lam_ref,
                    h_ref, xl_ref, hlast_ref, a_scr, b_scr, carry, *, n_tiles):
    i = pl.program_id(1)

    @pl.when(i == 0)
    def _():
        carry[...] = h0_ref[0]

    xl = _lru_conv(x_ref, xp_ref, xn_ref, cw_ref, cb_ref, i, n_tiles)
    xl_ref[0] = xl
    a, bb = _lru_coeffs(xl, wg_ref, bg_ref, lam_ref)
    _scan_tile(a, bb, a_scr, b_scr, h_ref.at[0], carry, reverse=False)
    hlast_ref[0] = carry[...]


def _lru_bwd_kernel(xl_ref, h0_ref, hf_ref, g_ref, wg_ref, bg_ref, lam_ref,
                    y_ref, hfirst_ref, a_scr, b_scr, h_scr, carry):
    i = pl.program_id(1)

    @pl.when(i == 0)
    def _():
        carry[...] = h0_ref[0]

    a, bb = _lru_coeffs(xl_ref[0], wg_ref, bg_ref, lam_ref)
    _scan_tile(a, bb, a_scr, b_scr, h_scr, carry, reverse=True)
    hfirst_ref[0] = carry[...]
    y_ref[0] = (g_ref[0] * (hf_ref[0] + h_scr[...])).astype(y_ref.dtype)


def _lru_mixer(z_lru, h0_f, h0_b, conv_w, conv_b, wg, layer, bg, lam, tm):
    b, l, c2 = z_lru.shape
    c = c2 // 2
    tm = min(tm, l)
    nt = l // tm
    state = jax.ShapeDtypeStruct((b, 1, c), F32)
    state_spec = pl.BlockSpec((1, 1, c), lambda bi, i: (bi, 0, 0))
    cp = _params(("parallel", "arbitrary"))
    gate = lambda d: [pl.BlockSpec((None, 1, c, c2), lambda bi, i: (layer, d, 0, 0)),
                      pl.BlockSpec((1, 1, c2), lambda bi, i: (d, 0, 0)),
                      pl.BlockSpec((1, 1, c), lambda bi, i: (d, 0, 0))]
    tile_f = pl.BlockSpec((1, tm, c), lambda bi, i: (bi, i, 0))

    prev_f, next_f = _halo_specs(tm, l, c, 0)
    hf, xl, hf_last = pl.pallas_call(
        functools.partial(_lru_fwd_kernel, n_tiles=nt),
        grid=(b, nt),
        in_specs=[tile_f, prev_f, next_f, state_spec, _const_spec(conv_w.shape), _const_spec((1, c))] + gate(0),
        out_specs=[tile_f, tile_f, state_spec],
        out_shape=[jax.ShapeDtypeStruct((b, l, c), F32), jax.ShapeDtypeStruct((b, l, c), F32), state],
        scratch_shapes=[pltpu.VMEM((tm, c), F32), pltpu.VMEM((tm, c), F32), pltpu.VMEM((1, c), F32)],
        compiler_params=cp,
        name="lru_fwd",
    )(z_lru, z_lru, z_lru, h0_f, conv_w, conv_b.reshape(1, c), wg, bg, lam)

    tile_b = lambda blk: pl.BlockSpec((1, tm, c), lambda bi, i: (bi, nt - 1 - i, blk))
    y, hb_first = pl.pallas_call(
        _lru_bwd_kernel,
        grid=(b, nt),
        in_specs=[tile_b(0), state_spec, tile_b(0), tile_b(1)] + gate(1),
        out_specs=[tile_b(0), state_spec],
        out_shape=[jax.ShapeDtypeStruct((b, l, c), BF16), state],
        scratch_shapes=[pltpu.VMEM((tm, c), F32), pltpu.VMEM((tm, c), F32), pltpu.VMEM((tm, c), F32),
                        pltpu.VMEM((1, c), F32)],
        compiler_params=cp,
        name="lru_bwd",
    )(xl, h0_b, hf, z_lru, wg, bg, lam)
    return y, hf_last, hb_first


def _cmlp_body(z, g_ref, b_ref, ws_ref, bs_ref, o_ref):
    tm, c = z.shape[0], z.shape[1] // 2
    gd = c // CMLP_GROUPS
    u = _gelu(z[:, :c])
    v = _gelu(z[:, c:])
    mu = jnp.mean(v, axis=-1, keepdims=True)
    var = jnp.mean(jnp.square(v - mu), axis=-1, keepdims=True)
    v = ((v - mu) * lax.rsqrt(var + EPS) * g_ref[...] + b_ref[...]).astype(BF16)
    bs = bs_ref[...]
    for ck in range(tm // CHUNK):
        rows = slice(ck * CHUNK, (ck + 1) * CHUNK)
        for g in range(CMLP_GROUPS):
            cols = slice(g * gd, (g + 1) * gd)
            mixed = _dot(ws_ref[g], v[rows, cols]) + bs[:, g:g + 1]
            o_ref[0, rows, cols] = (u[rows, cols] * mixed).astype(o_ref.dtype)


def _rms(x, g):
    return x * lax.rsqrt(jnp.mean(x * x, axis=-1, keepdims=True) + EPS) * g


def _mla_body(z, qg_ref, kvg_ref, wq_ref, wk_ref, wvt_ref, vone_ref,
              qc_ref, qs_ref, kcs_ref, q_ref, k_ref, vt_ref, q_scale):
    heads = q_ref.shape[2] // HEAD_PAD
    kvn = _rms(z[:, :KV_LORA], kvg_ref[...]).astype(BF16)
    qn = _rms(z[:, KV_LORA:KV_LORA + Q_LORA], qg_ref[...]).astype(BF16)
    t = z[:, KV_LORA + Q_LORA:] * kcs_ref[...]
    t = t + pltpu.roll(t, HEAD_PAD - QK_ROPE, 1)
    lane = lax.broadcasted_iota(jnp.int32, t.shape, 1)
    rope = jnp.where((lane >= QK_NOPE) & (lane < QK_NOPE + QK_ROPE), pltpu.roll(t, QK_NOPE, 1), 0.0)
    k = _dot(kvn, wk_ref[...]) + jnp.concatenate([rope] * heads, axis=1)
    k_ref[0] = k.astype(k_ref.dtype)
    v_t = lax.dot_general(wvt_ref[...], kvn, (((1,), (1,)), ((), ())), preferred_element_type=F32)
    vt_ref[0] = (v_t + vone_ref[...]).astype(vt_ref.dtype)
    qa = _dot(qn, wq_ref[...])
    qc = jnp.concatenate([qc_ref[...]] * heads, axis=1)
    qs = jnp.concatenate([qs_ref[...]] * heads, axis=1)
    q = qa * qc + pltpu.roll(qa, qa.shape[1] - QK_ROPE, 1) * qs
    q_ref[0] = (q * q_scale).astype(q_ref.dtype)


def _attn_kernel(q_ref, *refs, n_kv, tu, kb):
    kv_refs, o_ref = refs[:2 * n_kv], refs[2 * n_kv]
    s_buf, p_buf, mx_buf, acc_ref, ot_ref, qt_ref = refs[2 * n_kv + 1:]
    hv = o_ref.shape[2] // 2
    hr = kv_refs[1].shape[1] // 2
    tq = q_ref.shape[1]
    blocks = []
    off = 0
    for j in range(n_kv):
        k_ref, vt_ref = kv_refs[2 * j], kv_refs[2 * j + 1]
        lk = k_ref.shape[1]
        for s0 in range(0, lk, kb):
            blocks.append((k_ref, vt_ref, s0, off + s0, min(kb, lk - s0)))
        off += lk
    units = [(qi, hh) for qi in range(tq // tu) for hh in range(2)]
    n = len(units)

    qt_ref[...] = q_ref[0].T

    def stage_qk(u, bi):
        qi, hh = units[u]
        k_ref, _, s0, g0, size = blocks[bi]
        lanes = slice(hh * HEAD_PAD, (hh + 1) * HEAD_PAD)
        s = _dot(k_ref[0, s0:s0 + size, lanes], qt_ref[lanes, qi * tu:(qi + 1) * tu])
        s_buf[g0:g0 + size, :] = s
        m = jnp.max(s, axis=0, keepdims=True)
        mx_buf[u % 2] = m if bi == 0 else jnp.maximum(mx_buf[u % 2], m)

    def stage_exp(u, bi):
        _, _, _, g0, size = blocks[bi]
        p_buf[g0:g0 + size, :] = jnp.exp2(s_buf[g0:g0 + size, :] - mx_buf[u % 2]).astype(BF16)

    def stage_pv(u, bi):
        qi, hh = units[u]
        _, vt_ref, s0, g0, size = blocks[bi]
        pv = _dot(vt_ref[0, hh * hr:(hh + 1) * hr, s0:s0 + size], p_buf[g0:g0 + size, :])
        if bi > 0:
            pv = pv + acc_ref[...]
        if bi < len(blocks) - 1:
            acc_ref[...] = pv
        else:
            ot_ref[hh * hv:(hh + 1) * hv, qi * tu:(qi + 1) * tu] = pv[:hv] * (1.0 / pv[hv:hv + 1])

    for t in range(n + 2):
        for bi in range(len(blocks)):
            if 0 <= t - 2 < n:
                stage_pv(t - 2, bi)
            if 0 <= t - 1 < n:
                stage_exp(t - 1, bi)
            if t < n:
                stage_qk(t, bi)
    o_ref[0] = ot_ref[...].T.astype(o_ref.dtype)


def _attention(q, kvs, tq, tu):
    b, l, hp = q.shape
    heads = hp // HEAD_PAD
    tq = min(tq, l)
    tu = min(tu, tq)
    in_specs = [pl.BlockSpec((1, tq, 2 * HEAD_PAD), lambda bi, h, i: (bi, i, h))]
    args = [q]
    lk_total = 0
    for k, vt in kvs:
        lk = k.shape[1]
        lk_total += lk
        in_specs.append(pl.BlockSpec((1, lk, 2 * HEAD_PAD), lambda bi, h, i: (bi, 0, h)))
        in_specs.append(pl.BlockSpec((1, 2 * VT_ROWS, lk), lambda bi, h, i: (bi, h, 0)))
        args += [k, vt]
    return pl.pallas_call(
        functools.partial(_attn_kernel, n_kv=len(kvs), tu=tu, kb=KB_ATTN),
        grid=(b, heads // 2, l // tq),
        in_specs=in_specs,
        out_specs=pl.BlockSpec((1, tq, 2 * QK_NOPE), lambda bi, h, i: (bi, i, h)),
        out_shape=jax.ShapeDtypeStruct((b, l, heads * QK_NOPE), BF16),
        scratch_shapes=[pltpu.VMEM((lk_total, tu), F32), pltpu.VMEM((lk_total, tu), BF16),
                        pltpu.VMEM((2, 1, tu), F32), pltpu.VMEM((VT_ROWS, tu), F32),
                        pltpu.VMEM((2 * QK_NOPE, tq), F32),
                        pltpu.VMEM((2 * HEAD_PAD, tq), BF16)],
        compiler_params=_params(("parallel", "parallel", "parallel")),
        name="attention",
    )(*args)


def _merge_kernel(x_ref, h_ref, gt_ref, ab_ref, p_ref, pp_ref, pn_ref, cw_ref,
                  yb_ref, yc_ref, yd_ref, wg_ref, wb_ref, wo_ref, o_ref, *, n_tiles):
    i = pl.program_id(1)
    p = p_ref[0]
    p_prev = jnp.where(i > 0, pp_ref[0, SUBLANES - 1:, :], 0.0)
    p_next = jnp.where(i < n_tiles - 1, pn_ref[0, 0:1, :], 0.0)
    cw = cw_ref[...]
    conv = cw[0:1] * _shift_down(p, p_prev, 1) + cw[1:2] * p + cw[2:3] * _shift_up(p, p_next)
    y_a = (ab_ref[0] * conv).astype(BF16)

    x = x_ref[0]
    d = x.shape[1]
    h = h_ref[0]
    merged = None
    for n, y in enumerate((y_a, yb_ref[0], yc_ref[0], yd_ref[0])):
        gate = jax.nn.sigmoid(_dot(h, wg_ref[:, n * d:(n + 1) * d]))
        term = gate * _dot(y, wb_ref[n])
        merged = term if merged is None else merged + term
    o_ref[0] = x + gt_ref[0] * _dot(merged.astype(BF16), wo_ref[...])


def _merge(x, h, mod3, row_of, z_conv, conv_w, ys, wg, wb, wo, layer, tm):
    b, l, d = x.shape
    tm = min(tm, l)
    c = ys[0].shape[2]
    tok = lambda n: pl.BlockSpec((1, tm, n), lambda bi, i: (bi, i, 0))
    modv = lambda j: pl.BlockSpec((1, 1, d), lambda bi, i: (row_of(bi), 0, j))
    prev, nxt = _halo_specs(tm, l, c, 1)
    return pl.pallas_call(
        functools.partial(_merge_kernel, n_tiles=l // tm),
        grid=(b, l // tm),
        in_specs=[tok(d), tok(d), modv(2),
                  tok(c), pl.BlockSpec((1, tm, c), lambda bi, i: (bi, i, 1)), prev, nxt, _const_spec(conv_w.shape),
                  tok(c), tok(c), tok(c),
                  _layer_spec(wg, layer), _layer_spec(wb, layer), _layer_spec(wo, layer)],
        out_specs=tok(d),
        out_shape=jax.ShapeDtypeStruct((b, l, d), F32),
        compiler_params=_params(("parallel", "parallel")),
        name="merge",
    )(x, h, mod3, z_conv, z_conv, z_conv, z_conv, conv_w, *ys, wg, wb, wo)


def _ffn_kernel(x_ref, sh_ref, sc_ref, gt_ref, g_ref, w1_ref, w3_ref, w2_ref, fg_ref, o_ref, *, final_norm):
    x = x_ref[0]
    h = _norm_mod(x, g_ref[...], sh_ref[0], sc_ref[0]).astype(BF16)
    a = _dot(h, w1_ref[...])
    u = (a * jax.nn.sigmoid(a) * _dot(h, w3_ref[...])).astype(BF16)
    y = x + gt_ref[0] * _dot(u, w2_ref[...])
    if final_norm:
        y = _rms(y, fg_ref[...])
    o_ref[0] = y


def _ffn(x, mod3, row_of, g, w1, w3, w2, layer, final_g, final_norm, tm):
    b, l, d = x.shape
    tm = min(tm, l)
    tok = pl.BlockSpec((1, tm, d), lambda bi, i: (bi, i, 0))
    modv = lambda j: pl.BlockSpec((1, 1, d), lambda bi, i: (row_of(bi), 0, j))
    return pl.pallas_call(
        functools.partial(_ffn_kernel, final_norm=final_norm),
        grid=(b, l // tm),
        in_specs=[tok, modv(3), modv(4), modv(5), _const_spec((1, d)),
                  _layer_spec(w1, layer), _layer_spec(w3, layer), _layer_spec(w2, layer), _const_spec((1, d))],
        out_specs=tok,
        out_shape=jax.ShapeDtypeStruct((b, l, d), F32),
        compiler_params=_params(("parallel", "parallel")),
        name="ffn",
    )(x, mod3, mod3, mod3, g.reshape(1, d), w1, w3, w2, final_g.reshape(1, d))


def _rotate_half_cols(w):
    q = QK_ROPE // 4
    return jnp.concatenate([-w[:, q:2 * q], w[:, :q], -w[:, 3 * q:], w[:, 2 * q:3 * q]], axis=1)


def _block_diag(w):
    h, hd, _ = w.shape
    eye = jnp.eye(h, dtype=w.dtype)
    return (eye[:, None, :, None] * w[:, :, None, :]).reshape(h * hd, h * hd)


def _layer_weights(w_in_l, lru_w_a, lru_w_x, lru_b_a, lru_b_x, lru_lam, w_q_up, w_kv_up, cmlp_w_s,
                   w_branch, w_out, w_ff1, w_ff3, w_ff2):
    d = w_in_l.shape[0]
    c = d // 2
    o = {}
    i_lru_x, i_kv, i_kr, i_lru_g, i_q = 0, c, c + KV_LORA, c + KV_LORA + QK_ROPE, 2 * c + KV_LORA + QK_ROPE
    i_ab = i_q + Q_LORA
    i_ac, i_ax, i_cu, i_cv, i_gate = i_ab + c, i_ab + 2 * c, i_ab + 3 * c, i_ab + 4 * c, i_ab + 5 * c
    col = lambda s, n: w_in_l[:, s:s + n]
    k_rope = col(i_kr, QK_ROPE)
    mla_pad = jnp.zeros((d, HEAD_PAD - 2 * QK_ROPE), F32)
    o["w_lru"] = jnp.concatenate([col(i_lru_x, c), col(i_lru_g, c)], axis=1).astype(BF16)
    o["w_mla"] = jnp.concatenate([col(i_kv, KV_LORA), col(i_q, Q_LORA), k_rope, _rotate_half_cols(k_rope),
                                  mla_pad], axis=1).astype(BF16)
    o["w_conv"] = w_in_l[:, i_ab:i_cu].astype(BF16)
    o["w_cmlp"] = w_in_l[:, i_cu:i_gate].astype(BF16)
    o["w_gate"] = w_in_l[:, i_gate:].astype(BF16)
    o["lru_wg"] = jnp.stack([jnp.concatenate([_block_diag(lru_w_a[dd]), _block_diag(lru_w_x[dd])], axis=1)
                             for dd in range(2)]).astype(BF16)
    o["lru_bg"] = jnp.concatenate([lru_b_a, lru_b_x], axis=1).reshape(2, 1, 2 * c)
    o["lru_lam"] = lru_lam.reshape(2, 1, c)
    wq = w_q_up.reshape(Q_LORA, MLA_HEADS, QK_NOPE + QK_ROPE)
    rot = jnp.stack([_rotate_half_cols(wq[:, h, QK_NOPE:]) for h in range(MLA_HEADS)], axis=1)
    o["wq"] = jnp.concatenate([wq, rot], axis=2).reshape(Q_LORA, MLA_HEADS * HEAD_PAD).astype(BF16)
    wkv = w_kv_up.reshape(KV_LORA, MLA_HEADS, 2 * QK_NOPE)
    o["wk"] = jnp.concatenate([wkv[:, :, :QK_NOPE], jnp.zeros((KV_LORA, MLA_HEADS, HEAD_PAD - QK_NOPE), F32)],
                              axis=2).reshape(KV_LORA, MLA_HEADS * HEAD_PAD).astype(BF16)
    wv = jnp.concatenate([wkv[:, :, QK_NOPE:], jnp.zeros((KV_LORA, MLA_HEADS, ONES_ROWS), F32)], axis=2)
    o["wvt"] = wv.reshape(KV_LORA, MLA_HEADS * VT_ROWS).T.astype(BF16)
    o["w_s"] = cmlp_w_s.astype(BF16)
    o["w_branch"] = w_branch.astype(BF16)
    o["w_out"] = w_out.astype(BF16)
    o["w_ff1"] = w_ff1.astype(BF16)
    o["w_ff3"] = w_ff3.astype(BF16)
    o["w_ff2"] = w_ff2.astype(BF16)
    return o


def _rope_tables(n_lat):
    t = jnp.arange(n_lat)
    n_freq = QK_ROPE // 4
    inv = ROPE_THETA ** (-jnp.arange(n_freq, dtype=F32) / n_freq)
    ang_r = (t // GRID_W).astype(F32)[:, None] * inv
    ang_c = (t % GRID_W).astype(F32)[:, None] * inv
    cos = jnp.concatenate([jnp.cos(ang_r)] * 2 + [jnp.cos(ang_c)] * 2, axis=1)
    sin = jnp.concatenate([jnp.sin(ang_r)] * 2 + [jnp.sin(ang_c)] * 2, axis=1)
    ones = jnp.ones((n_lat, QK_NOPE), F32)
    z32 = jnp.zeros((n_lat, HEAD_PAD - QK_NOPE - QK_ROPE), F32)
    q_cos = jnp.concatenate([ones, cos, z32], axis=1)
    q_sin = jnp.concatenate([jnp.zeros((n_lat, QK_NOPE), F32), sin, z32], axis=1)
    k_cs = jnp.concatenate([cos, sin, jnp.zeros((n_lat, HEAD_PAD - 2 * QK_ROPE), F32)], axis=1)
    return q_cos, q_sin, k_cs


def _identity_tables(n):
    q_cos = jnp.concatenate([jnp.ones((n, QK_NOPE + QK_ROPE), F32),
                             jnp.zeros((n, HEAD_PAD - QK_NOPE - QK_ROPE), F32)], axis=1)
    q_sin = jnp.zeros((n, HEAD_PAD), F32)
    k_cs = jnp.concatenate([jnp.ones((n, QK_ROPE), F32), jnp.zeros((n, HEAD_PAD - QK_ROPE), F32)], axis=1)
    return q_cos, q_sin, k_cs


TM_PROJ = 512
TM_MIX = 1024
TQ_ATTN = 4096
TU_ATTN = 512
KB_ATTN = 1024


def kernel(x, c, ctx, c_ctx, w_mod, b_mod, norm1_g, norm2_g, w_in, conv_a_w, lru_conv_w, lru_conv_b,
           lru_w_a, lru_b_a, lru_w_x, lru_b_x, lru_lam, cmlp_ln_g, cmlp_ln_b, cmlp_w_s, cmlp_b_s,
           mla_q_norm_g, mla_kv_norm_g, mla_w_q_up, mla_w_kv_up, w_branch, w_out, w_ff1, w_ff3, w_ff2,
           final_norm_g):
    bsz, n_lat, d = x.shape
    n_ctx = ctx.shape[1]
    depth = w_in.shape[0]
    cw = d // 2
    q_scale = (QK_NOPE + QK_ROPE) ** -0.5 * LOG2E
    row_in_head = jnp.arange(MLA_HEADS * VT_ROWS) % VT_ROWS
    vone = (row_in_head >= QK_NOPE).astype(F32)[:, None]

    rows = -(-(bsz + 1) // SUBLANES) * SUBLANES
    s_rows = jnp.concatenate([c, c_ctx[None, :], jnp.zeros((rows - bsz - 1, d), F32)], axis=0)
    mod = _modulation(s_rows, w_mod, b_mod)
    lat_row = lambda bi: bi
    ctx_row = lambda bi: bsz

    lat_tabs = _rope_tables(n_lat)
    ctx_tabs = _identity_tables(n_ctx)
    zero_state = jnp.zeros((bsz, 1, cw), F32)

    w = jax.vmap(_layer_weights)(w_in, lru_w_a, lru_w_x, lru_b_a, lru_b_x, lru_lam, mla_w_q_up, mla_w_kv_up,
                                 cmlp_w_s, w_branch, w_out, w_ff1, w_ff3, w_ff2)

    xc = ctx
    for l in range(depth):
        last = l == depth - 1
        mod3 = mod[l].reshape(rows, 1, 6 * d)
        cmlp = (cmlp_ln_g[l], cmlp_ln_b[l], cmlp_b_s[l])
        mla = (mla_q_norm_g[l], mla_kv_norm_g[l], vone)
        lru = (lru_conv_w[l], lru_conv_b[l], w["lru_wg"], l, w["lru_bg"][l], w["lru_lam"][l], TM_MIX)
        gate_w = (w["w_gate"], w["w_branch"], w["w_out"], l, TM_PROJ)
        ffn_w = (w["w_ff1"], w["w_ff3"], w["w_ff2"], l, final_norm_g)

        hc, zc_lru, zc_conv, yc_c, qc, kc, vc = _inproj(xc, mod3, ctx_row, norm1_g[l], w, l, cmlp, mla, ctx_tabs,
                                                        q_scale, TM_PROJ)
        yc_b, hc_f, hc_b = _lru_mixer(zc_lru, zero_state, zero_state, *lru)

        h, z_lru, z_conv, y_c, q, k, v = _inproj(x, mod3, lat_row, norm1_g[l], w, l, cmlp, mla, lat_tabs,
                                                 q_scale, TM_PROJ)
        y_b, _, _ = _lru_mixer(z_lru, hc_f, hc_b, *lru)
        y_d = _attention(q, [(k, v), (kc, vc)], TQ_ATTN, TU_ATTN)
        x = _merge(x, h, mod3, lat_row, z_conv, conv_a_w[l], (y_b, y_c, y_d), *gate_w)
        x = _ffn(x, mod3, lat_row, norm2_g[l], *ffn_w, last, TM_PROJ)

        if not last:
            yc_d = _attention(qc, [(kc, vc)], TQ_ATTN, TU_ATTN)
            xc = _merge(xc, hc, mod3, ctx_row, zc_conv, conv_a_w[l], (yc_b, yc_c, yc_d), *gate_w)
            xc = _ffn(xc, mod3, ctx_row, norm2_g[l], *ffn_w, False, TM_PROJ)
    return x
```

```python
import functools

import jax
import jax.numpy as jnp
from jax import lax
from jax.experimental import pallas as pl
from jax.experimental.pallas import tpu as pltpu

F32 = jnp.float32
BF16 = jnp.bfloat16

EPS = 1e-6
GRID_W = 64
N_BRANCH = 4
LRU_HEADS = 8
LRU_C = 8.0
CMLP_GROUPS = 4
CHUNK = 128
MLA_HEADS = 8
QK_NOPE = 64
QK_ROPE = 32
Q_LORA = 384
KV_LORA = 256
ROPE_THETA = 10000.0
HEAD_PAD = 128
ONES_ROWS = 16
VT_ROWS = QK_NOPE + ONES_ROWS
LOG2E = 1.4426950408889634
SUBLANES = 8
VMEM_LIMIT = 56 * 1024 * 1024


def _params(sem, vmem=VMEM_LIMIT, flags=None):
    return pltpu.CompilerParams(dimension_semantics=sem, vmem_limit_bytes=vmem, flags=flags)


def _const_spec(shape):
    zeros = (0,) * len(shape)
    return pl.BlockSpec(shape, lambda *_: zeros, pipeline_mode=pl.Buffered(1))


def _layer_spec(stacked, layer):
    index = (layer,) + (0,) * (stacked.ndim - 1)
    return pl.BlockSpec((None,) + stacked.shape[1:], lambda *_: index, pipeline_mode=pl.Buffered(1))


def _gelu(x):
    return jax.nn.gelu(x)


def _norm_mod(x, g, shift, scale):
    gain = g * (1.0 + scale)
    return x * lax.rsqrt(jnp.mean(x * x, axis=-1, keepdims=True) + EPS) * gain + shift


def _dot(a, b):
    return jnp.dot(a, b, preferred_element_type=F32)


def _mod_kernel(s_ref, w_ref, b_ref, o_ref):
    s = s_ref[...]
    s = s * jax.nn.sigmoid(s)
    o_ref[0] = _dot(s.astype(BF16), w_ref[0].astype(BF16)) + b_ref[0]


def _modulation(s_rows, w_mod, b_mod):
    n_layer, d, d6 = w_mod.shape
    r = s_rows.shape[0]
    return pl.pallas_call(
        _mod_kernel,
        grid=(n_layer, d6 // d),
        in_specs=[
            pl.BlockSpec((r, d), lambda l, j: (0, 0)),
            pl.BlockSpec((1, d, d), lambda l, j: (l, 0, j)),
            pl.BlockSpec((1, 1, d), lambda l, j: (l, 0, j)),
        ],
        out_specs=pl.BlockSpec((1, r, d), lambda l, j: (l, 0, j)),
        out_shape=jax.ShapeDtypeStruct((n_layer, r, d6), F32),
        compiler_params=_params(("parallel", "parallel")),
        name="modulation",
    )(s_rows, w_mod, b_mod.reshape(n_layer, 1, d6))


def _inproj_kernel(x_ref, sh_ref, sc_ref, g_ref, w_lru, w_conv, w_cmlp, w_mla,
                   lng_ref, lnb_ref, ws_ref, bs_ref,
                   qg_ref, kvg_ref, wq_ref, wk_ref, wvt_ref, vone_ref, qc_ref, qs_ref, kcs_ref,
                   o_h, o_lru, o_conv, o_yc, q_ref, k_ref, vt_ref, *, q_scale):
    h = _norm_mod(x_ref[0], g_ref[...], sh_ref[0], sc_ref[0]).astype(BF16)
    o_h[0] = h
    zl = _dot(h, w_lru[...])
    zc = _dot(h, w_conv[...])
    c = zc.shape[1] // 3
    o_lru[0, :, :c] = zl[:, :c]
    o_lru[0, :, c:] = _gelu(zl[:, c:])
    o_conv[0, :, :c] = zc[:, :c]
    o_conv[0, :, c:] = zc[:, c:2 * c] * zc[:, 2 * c:]
    _cmlp_body(_dot(h, w_cmlp[...]), lng_ref, lnb_ref, ws_ref, bs_ref, o_yc)
    _mla_body(_dot(h, w_mla[...]), qg_ref, kvg_ref, wq_ref, wk_ref, wvt_ref, vone_ref,
              qc_ref, qs_ref, kcs_ref, q_ref, k_ref, vt_ref, q_scale)


def _inproj(x, mod3, row_of, g, w, layer, cmlp, mla, tabs, q_scale, tm):
    b, l, d = x.shape
    tm = min(tm, l)
    c = d // 2
    ln_g, ln_b, b_s = cmlp
    q_g, kv_g, vone = mla
    hp = w["wq"].shape[2]
    vrows = w["wvt"].shape[1]
    whole = lambda a: (a, _const_spec(a.shape))
    stacked = lambda name: (w[name], _layer_spec(w[name], layer))
    consts = [whole(g.reshape(1, d)), stacked("w_lru"), stacked("w_conv"), stacked("w_cmlp"), stacked("w_mla"),
              whole(ln_g.reshape(1, c)), whole(ln_b.reshape(1, c)), stacked("w_s"), whole(b_s.T),
              whole(q_g.reshape(1, Q_LORA)), whole(kv_g.reshape(1, KV_LORA)), stacked("wq"), stacked("wk"),
              stacked("wvt"), whole(vone)]
    tab = pl.BlockSpec((tm, HEAD_PAD), lambda bi, i: (i, 0))
    tok = lambda n: pl.BlockSpec((1, tm, n), lambda bi, i: (bi, i, 0))
    in_specs = ([tok(d),
                 pl.BlockSpec((1, 1, d), lambda bi, i: (row_of(bi), 0, 0)),
                 pl.BlockSpec((1, 1, d), lambda bi, i: (row_of(bi), 0, 1))]
                + [spec for _, spec in consts] + [tab, tab, tab])
    out_specs = [tok(d), tok(2 * c), tok(2 * c), tok(c), tok(hp), tok(hp),
                 pl.BlockSpec((1, vrows, tm), lambda bi, i: (bi, 0, i))]
    out_shape = [jax.ShapeDtypeStruct((b, l, d), BF16),
                 jax.ShapeDtypeStruct((b, l, 2 * c), F32), jax.ShapeDtypeStruct((b, l, 2 * c), F32),
                 jax.ShapeDtypeStruct((b, l, c), BF16), jax.ShapeDtypeStruct((b, l, hp), BF16),
                 jax.ShapeDtypeStruct((b, l, hp), BF16), jax.ShapeDtypeStruct((b, vrows, l), BF16)]
    return pl.pallas_call(
        functools.partial(_inproj_kernel, q_scale=q_scale),
        grid=(b, l // tm),
        in_specs=in_specs,
        out_specs=out_specs,
        out_shape=out_shape,
        compiler_params=_params(("parallel", "parallel")),
        name="inproj",
    )(x, mod3, mod3, *[a for a, _ in consts], *tabs)


def _halo_specs(tm, l, c, lane_blk):
    nb = l // SUBLANES
    per = tm // SUBLANES
    prev = pl.BlockSpec((1, SUBLANES, c), lambda bi, i: (bi, jnp.maximum(i * per - 1, 0), lane_blk))
    nxt = pl.BlockSpec((1, SUBLANES, c), lambda bi, i: (bi, jnp.minimum((i + 1) * per, nb - 1), lane_blk))
    return prev, nxt


def _shift_down(p, first_row, k):
    row = lax.broadcasted_iota(jnp.int32, p.shape, 0)
    out = pltpu.roll(p, k, 0)
    for j in range(k):
        out = jnp.where(row == j, first_row[j:j + 1, :], out)
    return out


def _shift_up(p, last_row):
    tm = p.shape[0]
    row = lax.broadcasted_iota(jnp.int32, p.shape, 0)
    return jnp.where(row == tm - 1, last_row, pltpu.roll(p, tm - 1, 0))


def _lru_conv(x_ref, xp_ref, xn_ref, cw_ref, cb_ref, i, n_tiles):
    x = x_ref[0]
    prev = jnp.where(i > 0, xp_ref[0, SUBLANES - 2:, :], 0.0)
    nxt = jnp.where(i < n_tiles - 1, xn_ref[0, 0:1, :], 0.0)
    cw = cw_ref[...]
    x_m1 = _shift_down(x, prev[1:2], 1)
    x_m2 = _shift_down(x, prev, 2)
    return cw[0:1] * x_m2 + cw[1:2] * x_m1 + cw[2:3] * x + cw[3:4] * _shift_up(x, nxt) + cb_ref[...]


def _lru_coeffs(xl, wg_ref, bg_ref, lam_ref):
    c = xl.shape[1]
    gates = _dot(xl.astype(BF16), wg_ref[0]) + bg_ref[0]
    r = jax.nn.sigmoid(gates[:, :c])
    gi = jax.nn.sigmoid(gates[:, c:])
    lam = lam_ref[0]
    softplus_neg = jnp.maximum(-lam, 0.0) + jnp.log1p(jnp.exp(-jnp.abs(lam)))
    a = jnp.exp2((-LRU_C * LOG2E * softplus_neg) * r)
    y = 1.0 - a * a
    root = jnp.where(y > 0.0, y * lax.rsqrt(y), 0.0)
    return a, root * (gi * xl)


def _scan_tile(a, bb, a_scr, b_scr, h_scr, carry_ref, reverse):
    tm, c = a.shape
    groups = tm // SUBLANES
    acc_a = a.reshape(groups, SUBLANES, c)
    acc_b = bb.reshape(groups, SUBLANES, c)
    r8 = lax.broadcasted_iota(jnp.int32, acc_a.shape, 1)
    for k in (1, 2, 4):
        if reverse:
            keep = r8 < SUBLANES - k
            shift = SUBLANES - k
        else:
            keep = r8 >= k
            shift = k
        a_sh = jnp.where(keep, pltpu.roll(acc_a, shift, 1), 1.0)
        b_sh = jnp.where(keep, pltpu.roll(acc_b, shift, 1), 0.0)
        acc_b = acc_a * b_sh + acc_b
        acc_a = acc_a * a_sh
    a_scr[...] = acc_a.reshape(tm, c)
    b_scr[...] = acc_b.reshape(tm, c)

    def body(gi, h_in):
        g = groups - 1 - gi if reverse else gi
        rows = pl.ds(pl.multiple_of(g * SUBLANES, SUBLANES), SUBLANES)
        h8 = b_scr[rows, :] + a_scr[rows, :] * h_in
        h_scr[rows, :] = h8
        return h8[0:1, :] if reverse else h8[SUBLANES - 1:, :]

    carry_ref[...] = lax.fori_loop(0, groups, body, carry_ref[...], unroll=4)


def _lru_fwd_kernel(x_ref, xp_ref, xn_ref, h0_ref, cw_ref, cb_ref, wg_ref, bg_ref, lam_ref,
                    h_ref, xl_ref, hlast_ref, a_scr, b_scr, carry, *, n_tiles):
    i = pl.program_id(1)

    @pl.when(i == 0)
    def _():
        carry[...] = h0_ref[0]

    xl = _lru_conv(x_ref, xp_ref, xn_ref, cw_ref, cb_ref, i, n_tiles)
    xl_ref[0] = xl
    a, bb = _lru_coeffs(xl, wg_ref, bg_ref, lam_ref)
    _scan_tile(a, bb, a_scr, b_scr, h_ref.at[0], carry, reverse=False)
    hlast_ref[0] = carry[...]


def _lru_bwd_kernel(xl_ref, h0_ref, hf_ref, g_ref, wg_ref, bg_ref, lam_ref,
                    y_ref, hfirst_ref, a_scr, b_scr, h_scr, carry):
    i = pl.program_id(1)

    @pl.when(i == 0)
    def _():
        carry[...] = h0_ref[0]

    a, bb = _lru_coeffs(xl_ref[0], wg_ref, bg_ref, lam_ref)
    _scan_tile(a, bb, a_scr, b_scr, h_scr, carry, reverse=True)
    hfirst_ref[0] = carry[...]
    y_ref[0] = (g_ref[0] * (hf_ref[0] + h_scr[...])).astype(y_ref.dtype)


def _lru_mixer(z_lru, h0_f, h0_b, conv_w, conv_b, wg, layer, bg, lam, tm):
    b, l, c2 = z_lru.shape
    c = c2 // 2
    tm = min(tm, l)
    nt = l // tm
    state = jax.ShapeDtypeStruct((b, 1, c), F32)
    state_spec = pl.BlockSpec((1, 1, c), lambda bi, i: (bi, 0, 0))
    cp = _params(("parallel", "arbitrary"))
    gate = lambda d: [pl.BlockSpec((None, 1, c, c2), lambda bi, i: (layer, d, 0, 0)),
                      pl.BlockSpec((1, 1, c2), lambda bi, i: (d, 0, 0)),
                      pl.BlockSpec((1, 1, c), lambda bi, i: (d, 0, 0))]
    tile_f = pl.BlockSpec((1, tm, c), lambda bi, i: (bi, i, 0))

    prev_f, next_f = _halo_specs(tm, l, c, 0)
    hf, xl, hf_last = pl.pallas_call(
        functools.partial(_lru_fwd_kernel, n_tiles=nt),
        grid=(b, nt),
        in_specs=[tile_f, prev_f, next_f, state_spec, _const_spec(conv_w.shape), _const_spec((1, c))] + gate(0),
        out_specs=[tile_f, tile_f, state_spec],
        out_shape=[jax.ShapeDtypeStruct((b, l, c), F32), jax.ShapeDtypeStruct((b, l, c), F32), state],
        scratch_shapes=[pltpu.VMEM((tm, c), F32), pltpu.VMEM((tm, c), F32), pltpu.VMEM((1, c), F32)],
        compiler_params=cp,
        name="lru_fwd",
    )(z_lru, z_lru, z_lru, h0_f, conv_w, conv_b.reshape(1, c), wg, bg, lam)

    tile_b = lambda blk: pl.BlockSpec((1, tm, c), lambda bi, i: (bi, nt - 1 - i, blk))
    y, hb_first = pl.pallas_call(
        _lru_bwd_kernel,
        grid=(b, nt),
        in_specs=[tile_b(0), state_spec, tile_b(0), tile_b(1)] + gate(1),
        out_specs=[tile_b(0), state_spec],
        out_shape=[jax.ShapeDtypeStruct((b, l, c), BF16), state],
        scratch_shapes=[pltpu.VMEM((tm, c), F32), pltpu.VMEM((tm, c), F32), pltpu.VMEM((tm, c), F32),
                        pltpu.VMEM((1, c), F32)],
        compiler_params=cp,
        name="lru_bwd",
    )(xl, h0_b, hf, z_lru, wg, bg, lam)
    return y, hf_last, hb_first


def _cmlp_body(z, g_ref, b_ref, ws_ref, bs_ref, o_ref):
    tm, c = z.shape[0], z.shape[1] // 2
    gd = c // CMLP_GROUPS
    u = _gelu(z[:, :c])
    v = _gelu(z[:, c:])
    mu = jnp.mean(v, axis=-1, keepdims=True)
    var = jnp.mean(jnp.square(v - mu), axis=-1, keepdims=True)
    v = ((v - mu) * lax.rsqrt(var + EPS) * g_ref[...] + b_ref[...]).astype(BF16)
    bs = bs_ref[...]
    for ck in range(tm // CHUNK):
        rows = slice(ck * CHUNK, (ck + 1) * CHUNK)
        for g in range(CMLP_GROUPS):
            cols = slice(g * gd, (g + 1) * gd)
            mixed = _dot(ws_ref[g], v[rows, cols]) + bs[:, g:g + 1]
            o_ref[0, rows, cols] = (u[rows, cols] * mixed).astype(o_ref.dtype)


def _rms(x, g):
    return x * lax.rsqrt(jnp.mean(x * x, axis=-1, keepdims=True) + EPS) * g


def _mla_body(z, qg_ref, kvg_ref, wq_ref, wk_ref, wvt_ref, vone_ref,
              qc_ref, qs_ref, kcs_ref, q_ref, k_ref, vt_ref, q_scale):
    heads = q_ref.shape[2] // HEAD_PAD
    kvn = _rms(z[:, :KV_LORA], kvg_ref[...]).astype(BF16)
    qn = _rms(z[:, KV_LORA:KV_LORA + Q_LORA], qg_ref[...]).astype(BF16)
    t = z[:, KV_LORA + Q_LORA:] * kcs_ref[...]
    t = t + pltpu.roll(t, HEAD_PAD - QK_ROPE, 1)
    lane = lax.broadcasted_iota(jnp.int32, t.shape, 1)
    rope = jnp.where((lane >= QK_NOPE) & (lane < QK_NOPE + QK_ROPE), pltpu.roll(t, QK_NOPE, 1), 0.0)
    k = _dot(kvn, wk_ref[...]) + jnp.concatenate([rope] * heads, axis=1)
    k_ref[0] = k.astype(k_ref.dtype)
    v_t = lax.dot_general(wvt_ref[...], kvn, (((1,), (1,)), ((), ())), preferred_element_type=F32)
    vt_ref[0] = (v_t + vone_ref[...]).astype(vt_ref.dtype)
    qa = _dot(qn, wq_ref[...])
    qc = jnp.concatenate([qc_ref[...]] * heads, axis=1)
    qs = jnp.concatenate([qs_ref[...]] * heads, axis=1)
    q = qa * qc + pltpu.roll(qa, qa.shape[1] - QK_ROPE, 1) * qs
    q_ref[0] = (q * q_scale).astype(q_ref.dtype)


def _attn_kernel(q_ref, *refs, n_kv, tu, kb):
    kv_refs, o_ref = refs[:2 * n_kv], refs[2 * n_kv]
    s_buf, p_buf, mx_buf, acc_ref, ot_ref, qt_ref = refs[2 * n_kv + 1:]
    hv = o_ref.shape[2] // 2
    hr = kv_refs[1].shape[1] // 2
    tq = q_ref.shape[1]
    blocks = []
    off = 0
    for j in range(n_kv):
        k_ref, vt_ref = kv_refs[2 * j], kv_refs[2 * j + 1]
        lk = k_ref.shape[1]
        for s0 in range(0, lk, kb):
            blocks.append((k_ref, vt_ref, s0, off + s0, min(kb, lk - s0)))
        off += lk
    units = [(qi, hh) for qi in range(tq // tu) for hh in range(2)]
    n = len(units)

    qt_ref[...] = q_ref[0].T

    def stage_qk(u, bi):
        qi, hh = units[u]
        k_ref, _, s0, g0, size = blocks[bi]
        lanes = slice(hh * HEAD_PAD, (hh + 1) * HEAD_PAD)
        s = _dot(k_ref[0, s0:s0 + size, lanes], qt_ref[lanes, qi * tu:(qi + 1) * tu])
        s_buf[g0:g0 + size, :] = s
        m = jnp.max(s, axis=0, keepdims=True)
        mx_buf[u % 2] = m if bi == 0 else jnp.maximum(mx_buf[u % 2], m)

    def stage_exp(u, bi):
        _, _, _, g0, size = blocks[bi]
        p_buf[g0:g0 + size, :] = jnp.exp2(s_buf[g0:g0 + size, :] - mx_buf[u % 2]).astype(BF16)

    def stage_pv(u, bi):
        qi, hh = units[u]
        _, vt_ref, s0, g0, size = blocks[bi]
        pv = _dot(vt_ref[0, hh * hr:(hh + 1) * hr, s0:s0 + size], p_buf[g0:g0 + size, :])
        if bi > 0:
            pv = pv + acc_ref[...]
        if bi < len(blocks) - 1:
            acc_ref[...] = pv
        else:
            ot_ref[hh * hv:(hh + 1) * hv, qi * tu:(qi + 1) * tu] = pv[:hv] * (1.0 / pv[hv:hv + 1])

    for t in range(n + 2):
        for bi in range(len(blocks)):
            if 0 <= t - 2 < n:
                stage_pv(t - 2, bi)
            if 0 <= t - 1 < n:
                stage_exp(t - 1, bi)
            if t < n:
                stage_qk(t, bi)
    o_ref[0] = ot_ref[...].T.astype(o_ref.dtype)


def _attention(q, kvs, tq, tu):
    b, l, hp = q.shape
    heads = hp // HEAD_PAD
    tq = min(tq, l)
    tu = min(tu, tq)
    in_specs = [pl.BlockSpec((1, tq, 2 * HEAD_PAD), lambda bi, h, i: (bi, i, h))]
    args = [q]
    lk_total = 0
    for k, vt in kvs:
        lk = k.shape[1]
        lk_total += lk
        in_specs.append(pl.BlockSpec((1, lk, 2 * HEAD_PAD), lambda bi, h, i: (bi, 0, h)))
        in_specs.append(pl.BlockSpec((1, 2 * VT_ROWS, lk), lambda bi, h, i: (bi, h, 0)))
        args += [k, vt]
    return pl.pallas_call(
        functools.partial(_attn_kernel, n_kv=len(kvs), tu=tu, kb=KB_ATTN),
        grid=(b, heads // 2, l // tq),
        in_specs=in_specs,
        out_specs=pl.BlockSpec((1, tq, 2 * QK_NOPE), lambda bi, h, i: (bi, i, h)),
        out_shape=jax.ShapeDtypeStruct((b, l, heads * QK_NOPE), BF16),
        scratch_shapes=[pltpu.VMEM((lk_total, tu), F32), pltpu.VMEM((lk_total, tu), BF16),
                        pltpu.VMEM((2, 1, tu), F32), pltpu.VMEM((VT_ROWS, tu), F32),
                        pltpu.VMEM((2 * QK_NOPE, tq), F32),
                        pltpu.VMEM((2 * HEAD_PAD, tq), BF16)],
        compiler_params=_params(("parallel", "parallel", "parallel")),
        name="attention",
    )(*args)


def _merge_kernel(x_ref, h_ref, gt_ref, ab_ref, p_ref, pp_ref, pn_ref, cw_ref,
                  yb_ref, yc_ref, yd_ref, wg_ref, wb_ref, wo_ref, o_ref, *, n_tiles):
    i = pl.program_id(1)
    p = p_ref[0]
    p_prev = jnp.where(i > 0, pp_ref[0, SUBLANES - 1:, :], 0.0)
    p_next = jnp.where(i < n_tiles - 1, pn_ref[0, 0:1, :], 0.0)
    cw = cw_ref[...]
    conv = cw[0:1] * _shift_down(p, p_prev, 1) + cw[1:2] * p + cw[2:3] * _shift_up(p, p_next)
    y_a = (ab_ref[0] * conv).astype(BF16)

    x = x_ref[0]
    d = x.shape[1]
    h = h_ref[0]
    merged = None
    for n, y in enumerate((y_a, yb_ref[0], yc_ref[0], yd_ref[0])):
        gate = jax.nn.sigmoid(_dot(h, wg_ref[:, n * d:(n + 1) * d]))
        term = gate * _dot(y, wb_ref[n])
        merged = term if merged is None else merged + term
    o_ref[0] = x + gt_ref[0] * _dot(merged.astype(BF16), wo_ref[...])


def _merge(x, h, mod3, row_of, z_conv, conv_w, ys, wg, wb, wo, layer, tm):
    b, l, d = x.shape
    tm = min(tm, l)
    c = ys[0].shape[2]
    tok = lambda n: pl.BlockSpec((1, tm, n), lambda bi, i: (bi, i, 0))
    modv = lambda j: pl.BlockSpec((1, 1, d), lambda bi, i: (row_of(bi), 0, j))
    prev, nxt = _halo_specs(tm, l, c, 1)
    return pl.pallas_call(
        functools.partial(_merge_kernel, n_tiles=l // tm),
        grid=(b, l // tm),
        in_specs=[tok(d), tok(d), modv(2),
                  tok(c), pl.BlockSpec((1, tm, c), lambda bi, i: (bi, i, 1)), prev, nxt, _const_spec(conv_w.shape),
                  tok(c), tok(c), tok(c),
                  _layer_spec(wg, layer), _layer_spec(wb, layer), _layer_spec(wo, layer)],
        out_specs=tok(d),
        out_shape=jax.ShapeDtypeStruct((b, l, d), F32),
        compiler_params=_params(("parallel", "parallel")),
        name="merge",
    )(x, h, mod3, z_conv, z_conv, z_conv, z_conv, conv_w, *ys, wg, wb, wo)


def _ffn_kernel(x_ref, sh_ref, sc_ref, gt_ref, g_ref, w1_ref, w3_ref, w2_ref, fg_ref, o_ref, *, final_norm):
    x = x_ref[0]
    h = _norm_mod(x, g_ref[...], sh_ref[0], sc_ref[0]).astype(BF16)
    a = _dot(h, w1_ref[...])
    u = (a * jax.nn.sigmoid(a) * _dot(h, w3_ref[...])).astype(BF16)
    y = x + gt_ref[0] * _dot(u, w2_ref[...])
    if final_norm:
        y = _rms(y, fg_ref[...])
    o_ref[0] = y


def _ffn(x, mod3, row_of, g, w1, w3, w2, layer, final_g, final_norm, tm):
    b, l, d = x.shape
    tm = min(tm, l)
    tok = pl.BlockSpec((1, tm, d), lambda bi, i: (bi, i, 0))
    modv = lambda j: pl.BlockSpec((1, 1, d), lambda bi, i: (row_of(bi), 0, j))
    return pl.pallas_call(
        functools.partial(_ffn_kernel, final_norm=final_norm),
        grid=(b, l // tm),
        in_specs=[tok, modv(3), modv(4), modv(5), _const_spec((1, d)),
                  _layer_spec(w1, layer), _layer_spec(w3, layer), _layer_spec(w2, layer), _const_spec((1, d))],
        out_specs=tok,
        out_shape=jax.ShapeDtypeStruct((b, l, d), F32),
        compiler_params=_params(("parallel", "parallel")),
        name="ffn",
    )(x, mod3, mod3, mod3, g.reshape(1, d), w1, w3, w2, final_g.reshape(1, d))


def _rotate_half_cols(w):
    q = QK_ROPE // 4
    return jnp.concatenate([-w[:, q:2 * q], w[:, :q], -w[:, 3 * q:], w[:, 2 * q:3 * q]], axis=1)


def _block_diag(w):
    h, hd, _ = w.shape
    eye = jnp.eye(h, dtype=w.dtype)
    return (eye[:, None, :, None] * w[:, :, None, :]).reshape(h * hd, h * hd)


def _layer_weights(w_in_l, lru_w_a, lru_w_x, lru_b_a, lru_b_x, lru_lam, w_q_up, w_kv_up, cmlp_w_s,
                   w_branch, w_out, w_ff1, w_ff3, w_ff2):
    d = w_in_l.shape[0]
    c = d // 2
    o = {}
    i_lru_x, i_kv, i_kr, i_lru_g, i_q = 0, c, c + KV_LORA, c + KV_LORA + QK_ROPE, 2 * c + KV_LORA + QK_ROPE
    i_ab = i_q + Q_LORA
    i_ac, i_ax, i_cu, i_cv, i_gate = i_ab + c, i_ab + 2 * c, i_ab + 3 * c, i_ab + 4 * c, i_ab + 5 * c
    col = lambda s, n: w_in_l[:, s:s + n]
    k_rope = col(i_kr, QK_ROPE)
    mla_pad = jnp.zeros((d, HEAD_PAD - 2 * QK_ROPE), F32)
    o["w_lru"] = jnp.concatenate([col(i_lru_x, c), col(i_lru_g, c)], axis=1).astype(BF16)
    o["w_mla"] = jnp.concatenate([col(i_kv, KV_LORA), col(i_q, Q_LORA), k_rope, _rotate_half_cols(k_rope),
                                  mla_pad], axis=1).astype(BF16)
    o["w_conv"] = w_in_l[:, i_ab:i_cu].astype(BF16)
    o["w_cmlp"] = w_in_l[:, i_cu:i_gate].astype(BF16)
    o["w_gate"] = w_in_l[:, i_gate:].astype(BF16)
    o["lru_wg"] = jnp.stack([jnp.concatenate([_block_diag(lru_w_a[dd]), _block_diag(lru_w_x[dd])], axis=1)
                             for dd in range(2)]).astype(BF16)
    o["lru_bg"] = jnp.concatenate([lru_b_a, lru_b_x], axis=1).reshape(2, 1, 2 * c)
    o["lru_lam"] = lru_lam.reshape(2, 1, c)
    wq = w_q_up.reshape(Q_LORA, MLA_HEADS, QK_NOPE + QK_ROPE)
    rot = jnp.stack([_rotate_half_cols(wq[:, h, QK_NOPE:]) for h in range(MLA_HEADS)], axis=1)
    o["wq"] = jnp.concatenate([wq, rot], axis=2).reshape(Q_LORA, MLA_HEADS * HEAD_PAD).astype(BF16)
    wkv = w_kv_up.reshape(KV_LORA, MLA_HEADS, 2 * QK_NOPE)
    o["wk"] = jnp.concatenate([wkv[:, :, :QK_NOPE], jnp.zeros((KV_LORA, MLA_HEADS, HEAD_PAD - QK_NOPE), F32)],
                              axis=2).reshape(KV_LORA, MLA_HEADS * HEAD_PAD).astype(BF16)
    wv = jnp.concatenate([wkv[:, :, QK_NOPE:], jnp.zeros((KV_LORA, MLA_HEADS, ONES_ROWS), F32)], axis=2)
    o["wvt"] = wv.reshape(KV_LORA, MLA_HEADS * VT_ROWS).T.astype(BF16)
    o["w_s"] = cmlp_w_s.astype(BF16)
    o["w_branch"] = w_branch.astype(BF16)
    o["w_out"] = w_out.astype(BF16)
    o["w_ff1"] = w_ff1.astype(BF16)
    o["w_ff3"] = w_ff3.astype(BF16)
    o["w_ff2"] = w_ff2.astype(BF16)
    return o


def _rope_tables(n_lat):
    t = jnp.arange(n_lat)
    n_freq = QK_ROPE // 4
    inv = ROPE_THETA ** (-jnp.arange(n_freq, dtype=F32) / n_freq)
    ang_r = (t // GRID_W).astype(F32)[:, None] * inv
    ang_c = (t % GRID_W).astype(F32)[:, None] * inv
    cos = jnp.concatenate([jnp.cos(ang_r)] * 2 + [jnp.cos(ang_c)] * 2, axis=1)
    sin = jnp.concatenate([jnp.sin(ang_r)] * 2 + [jnp.sin(ang_c)] * 2, axis=1)
    ones = jnp.ones((n_lat, QK_NOPE), F32)
    z32 = jnp.zeros((n_lat, HEAD_PAD - QK_NOPE - QK_ROPE), F32)
    q_cos = jnp.concatenate([ones, cos, z32], axis=1)
    q_sin = jnp.concatenate([jnp.zeros((n_lat, QK_NOPE), F32), sin, z32], axis=1)
    k_cs = jnp.concatenate([cos, sin, jnp.zeros((n_lat, HEAD_PAD - 2 * QK_ROPE), F32)], axis=1)
    return q_cos, q_sin, k_cs


def _identity_tables(n):
    q_cos = jnp.concatenate([jnp.ones((n, QK_NOPE + QK_ROPE), F32),
                             jnp.zeros((n, HEAD_PAD - QK_NOPE - QK_ROPE), F32)], axis=1)
    q_sin = jnp.zeros((n, HEAD_PAD), F32)
    k_cs = jnp.concatenate([jnp.ones((n, QK_ROPE), F32), jnp.zeros((n, HEAD_PAD - QK_ROPE), F32)], axis=1)
    return q_cos, q_sin, k_cs


TM_PROJ = 512
TM_MIX = 1024
TQ_ATTN = 2048
TU_ATTN = 512
KB_ATTN = 2048


def kernel(x, c, ctx, c_ctx, w_mod, b_mod, norm1_g, norm2_g, w_in, conv_a_w, lru_conv_w, lru_conv_b,
           lru_w_a, lru_b_a, lru_w_x, lru_b_x, lru_lam, cmlp_ln_g, cmlp_ln_b, cmlp_w_s, cmlp_b_s,
           mla_q_norm_g, mla_kv_norm_g, mla_w_q_up, mla_w_kv_up, w_branch, w_out, w_ff1, w_ff3, w_ff2,
           final_norm_g):
    bsz, n_lat, d = x.shape
    n_ctx = ctx.shape[1]
    depth = w_in.shape[0]
    cw = d // 2
    q_scale = (QK_NOPE + QK_ROPE) ** -0.5 * LOG2E
    row_in_head = jnp.arange(MLA_HEADS * VT_ROWS) % VT_ROWS
    vone = (row_in_head >= QK_NOPE).astype(F32)[:, None]

    rows = -(-(bsz + 1) // SUBLANES) * SUBLANES
    s_rows = jnp.concatenate([c, c_ctx[None, :], jnp.zeros((rows - bsz - 1, d), F32)], axis=0)
    mod = _modulation(s_rows, w_mod, b_mod)
    lat_row = lambda bi: bi
    ctx_row = lambda bi: bsz

    lat_tabs = _rope_tables(n_lat)
    ctx_tabs = _identity_tables(n_ctx)
    zero_state = jnp.zeros((bsz, 1, cw), F32)

    w = jax.vmap(_layer_weights)(w_in, lru_w_a, lru_w_x, lru_b_a, lru_b_x, lru_lam, mla_w_q_up, mla_w_kv_up,
                                 cmlp_w_s, w_branch, w_out, w_ff1, w_ff3, w_ff2)

    xc = ctx
    for l in range(depth):
        last = l == depth - 1
        mod3 = mod[l].reshape(rows, 1, 6 * d)
        cmlp = (cmlp_ln_g[l], cmlp_ln_b[l], cmlp_b_s[l])
        mla = (mla_q_norm_g[l], mla_kv_norm_g[l], vone)
        lru = (lru_conv_w[l], lru_conv_b[l], w["lru_wg"], l, w["lru_bg"][l], w["lru_lam"][l], TM_MIX)
        gate_w = (w["w_gate"], w["w_branch"], w["w_out"], l, TM_PROJ)
        ffn_w = (w["w_ff1"], w["w_ff3"], w["w_ff2"], l, final_norm_g)

        hc, zc_lru, zc_conv, yc_c, qc, kc, vc = _inproj(xc, mod3, ctx_row, norm1_g[l], w, l, cmlp, mla, ctx_tabs,
                                                        q_scale, TM_PROJ)
        yc_b, hc_f, hc_b = _lru_mixer(zc_lru, zero_state, zero_state, *lru)

        h, z_lru, z_conv, y_c, q, k, v = _inproj(x, mod3, lat_row, norm1_g[l], w, l, cmlp, mla, lat_tabs,
                                                 q_scale, TM_PROJ)
        y_b, _, _ = _lru_mixer(z_lru, hc_f, hc_b, *lru)
        y_d = _attention(q, [(k, v), (kc, vc)], TQ_ATTN, TU_ATTN)
        x = _merge(x, h, mod3, lat_row, z_conv, conv_a_w[l], (y_b, y_c, y_d), *gate_w)
        x = _ffn(x, mod3, lat_row, norm2_g[l], *ffn_w, last, TM_PROJ)

        if not last:
            yc_d = _attention(qc, [(kc, vc)], TQ_ATTN, TU_ATTN)
            xc = _merge(xc, hc, mod3, ctx_row, zc_conv, conv_a_w[l], (yc_b, yc_c, yc_d), *gate_w)
            xc = _ffn(xc, mod3, ctx_row, norm2_g[l], *ffn_w, False, TM_PROJ)
    return x
```

```python
import functools

import jax
import jax.numpy as jnp
from jax import lax
from jax.experimental import pallas as pl
from jax.experimental.pallas import tpu as pltpu

F32 = jnp.float32
BF16 = jnp.bfloat16

EPS = 1e-6
GRID_W = 64
N_BRANCH = 4
LRU_HEADS = 8
LRU_C = 8.0
CMLP_GROUPS = 4
CHUNK = 128
MLA_HEADS = 8
QK_NOPE = 64
QK_ROPE = 32
Q_LORA = 384
KV_LORA = 256
ROPE_THETA = 10000.0
HEAD_PAD = 128
ONES_ROWS = 16
VT_ROWS = QK_NOPE + ONES_ROWS
LOG2E = 1.4426950408889634
SUBLANES = 8
VMEM_LIMIT = 56 * 1024 * 1024


def _params(sem, vmem=VMEM_LIMIT, flags=None):
    return pltpu.CompilerParams(dimension_semantics=sem, vmem_limit_bytes=vmem, flags=flags)


def _const_spec(shape):
    zeros = (0,) * len(shape)
    return pl.BlockSpec(shape, lambda *_: zeros, pipeline_mode=pl.Buffered(1))


def _layer_spec(stacked, layer):
    index = (layer,) + (0,) * (stacked.ndim - 1)
    return pl.BlockSpec((None,) + stacked.shape[1:], lambda *_: index, pipeline_mode=pl.Buffered(1))


def _gelu(x):
    return jax.nn.gelu(x)


def _norm_mod(x, g, shift, scale):
    gain = g * (1.0 + scale)
    return x * lax.rsqrt(jnp.mean(x * x, axis=-1, keepdims=True) + EPS) * gain + shift


def _dot(a, b):
    return jnp.dot(a, b, preferred_element_type=F32)


def _mod_kernel(s_ref, w_ref, b_ref, o_ref):
    s = s_ref[...]
    s = s * jax.nn.sigmoid(s)
    o_ref[0] = _dot(s.astype(BF16), w_ref[0].astype(BF16)) + b_ref[0]


def _modulation(s_rows, w_mod, b_mod):
    n_layer, d, d6 = w_mod.shape
    r = s_rows.shape[0]
    return pl.pallas_call(
        _mod_kernel,
        grid=(n_layer, d6 // d),
        in_specs=[
            pl.BlockSpec((r, d), lambda l, j: (0, 0)),
            pl.BlockSpec((1, d, d), lambda l, j: (l, 0, j)),
            pl.BlockSpec((1, 1, d), lambda l, j: (l, 0, j)),
        ],
        out_specs=pl.BlockSpec((1, r, d), lambda l, j: (l, 0, j)),
        out_shape=jax.ShapeDtypeStruct((n_layer, r, d6), F32),
        compiler_params=_params(("parallel", "parallel")),
        name="modulation",
    )(s_rows, w_mod, b_mod.reshape(n_layer, 1, d6))


def _inproj_kernel(x_ref, sh_ref, sc_ref, g_ref, w_lru, w_conv, w_cmlp, w_mla,
                   lng_ref, lnb_ref, ws_ref, bs_ref,
                   qg_ref, kvg_ref, wq_ref, wk_ref, wvt_ref, vone_ref, qc_ref, qs_ref, kcs_ref,
                   o_h, o_lru, o_conv, o_yc, q_ref, k_ref, vt_ref, *, q_scale):
    h = _norm_mod(x_ref[0], g_ref[...], sh_ref[0], sc_ref[0]).astype(BF16)
    o_h[0] = h
    zl = _dot(h, w_lru[...])
    zc = _dot(h, w_conv[...])
    c = zc.shape[1] // 3
    o_lru[0, :, :c] = zl[:, :c]
    o_lru[0, :, c:] = _gelu(zl[:, c:])
    o_conv[0, :, :c] = zc[:, :c]
    o_conv[0, :, c:] = zc[:, c:2 * c] * zc[:, 2 * c:]
    _cmlp_body(_dot(h, w_cmlp[...]), lng_ref, lnb_ref, ws_ref, bs_ref, o_yc)
    _mla_body(_dot(h, w_mla[...]), qg_ref, kvg_ref, wq_ref, wk_ref, wvt_ref, vone_ref,
              qc_ref, qs_ref, kcs_ref, q_ref, k_ref, vt_ref, q_scale)


def _inproj(x, mod3, row_of, g, w, layer, cmlp, mla, tabs, q_scale, tm):
    b, l, d = x.shape
    tm = min(tm, l)
    c = d // 2
    ln_g, ln_b, b_s = cmlp
    q_g, kv_g, vone = mla
    hp = w["wq"].shape[2]
    vrows = w["wvt"].shape[1]
    whole = lambda a: (a, _const_spec(a.shape))
    stacked = lambda name: (w[name], _layer_spec(w[name], layer))
    consts = [whole(g.reshape(1, d)), stacked("w_lru"), stacked("w_conv"), stacked("w_cmlp"), stacked("w_mla"),
              whole(ln_g.reshape(1, c)), whole(ln_b.reshape(1, c)), stacked("w_s"), whole(b_s.T),
              whole(q_g.reshape(1, Q_LORA)), whole(kv_g.reshape(1, KV_LORA)), stacked("wq"), stacked("wk"),
              stacked("wvt"), whole(vone)]
    tab = pl.BlockSpec((tm, HEAD_PAD), lambda bi, i: (i, 0))
    tok = lambda n: pl.BlockSpec((1, tm, n), lambda bi, i: (bi, i, 0))
    in_specs = ([tok(d),
                 pl.BlockSpec((1, 1, d), lambda bi, i: (row_of(bi), 0, 0)),
                 pl.BlockSpec((1, 1, d), lambda bi, i: (row_of(bi), 0, 1))]
                + [spec for _, spec in consts] + [tab, tab, tab])
    out_specs = [tok(d), tok(2 * c), tok(2 * c), tok(c), tok(hp), tok(hp),
                 pl.BlockSpec((1, vrows, tm), lambda bi, i: (bi, 0, i))]
    out_shape = [jax.ShapeDtypeStruct((b, l, d), BF16),
                 jax.ShapeDtypeStruct((b, l, 2 * c), F32), jax.ShapeDtypeStruct((b, l, 2 * c), F32),
                 jax.ShapeDtypeStruct((b, l, c), BF16), jax.ShapeDtypeStruct((b, l, hp), BF16),
                 jax.ShapeDtypeStruct((b, l, hp), BF16), jax.ShapeDtypeStruct((b, vrows, l), BF16)]
    return pl.pallas_call(
        functools.partial(_inproj_kernel, q_scale=q_scale),
        grid=(b, l // tm),
        in_specs=in_specs,
        out_specs=out_specs,
        out_shape=out_shape,
        compiler_params=_params(("parallel", "parallel")),
        name="inproj",
    )(x, mod3, mod3, *[a for a, _ in consts], *tabs)


def _halo_specs(tm, l, c, lane_blk):
    nb = l // SUBLANES
    per = tm // SUBLANES
    prev = pl.BlockSpec((1, SUBLANES, c), lambda bi, i: (bi, jnp.maximum(i * per - 1, 0), lane_blk))
    nxt = pl.BlockSpec((1, SUBLANES, c), lambda bi, i: (bi, jnp.minimum((i + 1) * per, nb - 1), lane_blk))
    return prev, nxt


def _shift_down(p, first_row, k):
    row = lax.broadcasted_iota(jnp.int32, p.shape, 0)
    out = pltpu.roll(p, k, 0)
    for j in range(k):
        out = jnp.where(row == j, first_row[j:j + 1, :], out)
    return out


def _shift_up(p, last_row):
    tm = p.shape[0]
    row = lax.broadcasted_iota(jnp.int32, p.shape, 0)
    return jnp.where(row == tm - 1, last_row, pltpu.roll(p, tm - 1, 0))


def _lru_conv(x_ref, xp_ref, xn_ref, cw_ref, cb_ref, i, n_tiles):
    x = x_ref[0]
    prev = jnp.where(i > 0, xp_ref[0, SUBLANES - 2:, :], 0.0)
    nxt = jnp.where(i < n_tiles - 1, xn_ref[0, 0:1, :], 0.0)
    cw = cw_ref[...]
    x_m1 = _shift_down(x, prev[1:2], 1)
    x_m2 = _shift_down(x, prev, 2)
    return cw[0:1] * x_m2 + cw[1:2] * x_m1 + cw[2:3] * x + cw[3:4] * _shift_up(x, nxt) + cb_ref[...]


def _lru_coeffs(xl, wg_ref, bg_ref, lam_ref):
    c = xl.shape[1]
    gates = _dot(xl.astype(BF16), wg_ref[0]) + bg_ref[0]
    r = jax.nn.sigmoid(gates[:, :c])
    gi = jax.nn.sigmoid(gates[:, c:])
    lam = lam_ref[0]
    softplus_neg = jnp.maximum(-lam, 0.0) + jnp.log1p(jnp.exp(-jnp.abs(lam)))
    a = jnp.exp2((-LRU_C * LOG2E * softplus_neg) * r)
    y = 1.0 - a * a
    root = jnp.where(y > 0.0, y * lax.rsqrt(y), 0.0)
    return a, root * (gi * xl)


def _scan_tile(a, bb, a_scr, b_scr, h_scr, carry_ref, reverse):
    tm, c = a.shape
    groups = tm // SUBLANES
    acc_a = a.reshape(groups, SUBLANES, c)
    acc_b = bb.reshape(groups, SUBLANES, c)
    r8 = lax.broadcasted_iota(jnp.int32, acc_a.shape, 1)
    for k in (1, 2, 4):
        if reverse:
            keep = r8 < SUBLANES - k
            shift = SUBLANES - k
        else:
            keep = r8 >= k
            shift = k
        a_sh = jnp.where(keep, pltpu.roll(acc_a, shift, 1), 1.0)
        b_sh = jnp.where(keep, pltpu.roll(acc_b, shift, 1), 0.0)
        acc_b = acc_a * b_sh + acc_b
        acc_a = acc_a * a_sh
    a_scr[...] = acc_a.reshape(tm, c)
    b_scr[...] = acc_b.reshape(tm, c)

    def body(gi, h_in):
        g = groups - 1 - gi if reverse else gi
        rows = pl.ds(pl.multiple_of(g * SUBLANES, SUBLANES), SUBLANES)
        h8 = b_scr[rows, :] + a_scr[rows, :] * h_in
        h_scr[rows, :] = h8
        return h8[0:1, :] if reverse else h8[SUBLANES - 1:, :]

    carry_ref[...] = lax.fori_loop(0, groups, body, carry_ref[...], unroll=4)


def _lru_fwd_kernel(x_ref, xp_ref, xn_ref, h0_ref, cw_ref, cb_ref, wg_ref, bg_ref, lam_ref,
                    h_ref, xl_ref, hlast_ref, a_scr, b_scr, carry, *, n_tiles):
    i = pl.program_id(1)

    @pl.when(i == 0)
    def _():
        carry[...] = h0_ref[0]

    xl = _lru_conv(x_ref, xp_ref, xn_ref, cw_ref, cb_ref, i, n_tiles)
    xl_ref[0] = xl
    a, bb = _lru_coeffs(xl, wg_ref, bg_ref, lam_ref)
    _scan_tile(a, bb, a_scr, b_scr, h_ref.at[0], carry, reverse=False)
    hlast_ref[0] = carry[...]


def _lru_bwd_kernel(xl_ref, h0_ref, hf_ref, g_ref, wg_ref, bg_ref, lam_ref,
                    y_ref, hfirst_ref, a_scr, b_scr, h_scr, carry):
    i = pl.program_id(1)

    @pl.when(i == 0)
    def _():
        carry[...] = h0_ref[0]

    a, bb = _lru_coeffs(xl_ref[0], wg_ref, bg_ref, lam_ref)
    _scan_tile(a, bb, a_scr, b_scr, h_scr, carry, reverse=True)
    hfirst_ref[0] = carry[...]
    y_ref[0] = (g_ref[0] * (hf_ref[0] + h_scr[...])).astype(y_ref.dtype)


def _lru_mixer(z_lru, h0_f, h0_b, conv_w, conv_b, wg, layer, bg, lam, tm):
    b, l, c2 = z_lru.shape
    c = c2 // 2
    tm = min(tm, l)
    nt = l // tm
    state = jax.ShapeDtypeStruct((b, 1, c), F32)
    state_spec = pl.BlockSpec((1, 1, c), lambda bi, i: (bi, 0, 0))
    cp = _params(("parallel", "arbitrary"))
    gate = lambda d: [pl.BlockSpec((None, 1, c, c2), lambda bi, i: (layer, d, 0, 0)),
                      pl.BlockSpec((1, 1, c2), lambda bi, i: (d, 0, 0)),
                      pl.BlockSpec((1, 1, c), lambda bi, i: (d, 0, 0))]
    tile_f = pl.BlockSpec((1, tm, c), lambda bi, i: (bi, i, 0))

    prev_f, next_f = _halo_specs(tm, l, c, 0)
    hf, xl, hf_last = pl.pallas_call(
        functools.partial(_lru_fwd_kernel, n_tiles=nt),
        grid=(b, nt),
        in_specs=[tile_f, prev_f, next_f, state_spec, _const_spec(conv_w.shape), _const_spec((1, c))] + gate(0),
        out_specs=[tile_f, tile_f, state_spec],
        out_shape=[jax.ShapeDtypeStruct((b, l, c), F32), jax.ShapeDtypeStruct((b, l, c), F32), state],
        scratch_shapes=[pltpu.VMEM((tm, c), F32), pltpu.VMEM((tm, c), F32), pltpu.VMEM((1, c), F32)],
        compiler_params=cp,
        name="lru_fwd",
    )(z_lru, z_lru, z_lru, h0_f, conv_w, conv_b.reshape(1, c), wg, bg, lam)

    tile_b = lambda blk: pl.BlockSpec((1, tm, c), lambda bi, i: (bi, nt - 1 - i, blk))
    y, hb_first = pl.pallas_call(
        _lru_bwd_kernel,
        grid=(b, nt),
        in_specs=[tile_b(0), state_spec, tile_b(0), tile_b(1)] + gate(1),
        out_specs=[tile_b(0), state_spec],
        out_shape=[jax.ShapeDtypeStruct((b, l, c), BF16), state],
        scratch_shapes=[pltpu.VMEM((tm, c), F32), pltpu.VMEM((tm, c), F32), pltpu.VMEM((tm, c), F32),
                        pltpu.VMEM((1, c), F32)],
        compiler_params=cp,
        name="lru_bwd",
    )(xl, h0_b, hf, z_lru, wg, bg, lam)
    return y, hf_last, hb_first


def _cmlp_body(z, g_ref, b_ref, ws_ref, bs_ref, o_ref):
    tm, c = z.shape[0], z.shape[1] // 2
    gd = c // CMLP_GROUPS
    u = _gelu(z[:, :c])
    v = _gelu(z[:, c:])
    mu = jnp.mean(v, axis=-1, keepdims=True)
    var = jnp.mean(jnp.square(v - mu), axis=-1, keepdims=True)
    v = ((v - mu) * lax.rsqrt(var + EPS) * g_ref[...] + b_ref[...]).astype(BF16)
    bs = bs_ref[...]
    for ck in range(tm // CHUNK):
        rows = slice(ck * CHUNK, (ck + 1) * CHUNK)
        for g in range(CMLP_GROUPS):
            cols = slice(g * gd, (g + 1) * gd)
            mixed = _dot(ws_ref[g], v[rows, cols]) + bs[:, g:g + 1]
            o_ref[0, rows, cols] = (u[rows, cols] * mixed).astype(o_ref.dtype)


def _rms(x, g):
    return x * lax.rsqrt(jnp.mean(x * x, axis=-1, keepdims=True) + EPS) * g


def _mla_body(z, qg_ref, kvg_ref, wq_ref, wk_ref, wvt_ref, vone_ref,
              qc_ref, qs_ref, kcs_ref, q_ref, k_ref, vt_ref, q_scale):
    heads = q_ref.shape[2] // HEAD_PAD
    kvn = _rms(z[:, :KV_LORA], kvg_ref[...]).astype(BF16)
    qn = _rms(z[:, KV_LORA:KV_LORA + Q_LORA], qg_ref[...]).astype(BF16)
    t = z[:, KV_LORA + Q_LORA:] * kcs_ref[...]
    t = t + pltpu.roll(t, HEAD_PAD - QK_ROPE, 1)
    lane = lax.broadcasted_iota(jnp.int32, t.shape, 1)
    rope = jnp.where((lane >= QK_NOPE) & (lane < QK_NOPE + QK_ROPE), pltpu.roll(t, QK_NOPE, 1), 0.0)
    k = _dot(kvn, wk_ref[...]) + jnp.concatenate([rope] * heads, axis=1)
    k_ref[0] = k.astype(k_ref.dtype)
    v_t = lax.dot_general(wvt_ref[...], kvn, (((1,), (1,)), ((), ())), preferred_element_type=F32)
    vt_ref[0] = (v_t + vone_ref[...]).astype(vt_ref.dtype)
    qa = _dot(qn, wq_ref[...])
    qc = jnp.concatenate([qc_ref[...]] * heads, axis=1)
    qs = jnp.concatenate([qs_ref[...]] * heads, axis=1)
    q = qa * qc + pltpu.roll(qa, qa.shape[1] - QK_ROPE, 1) * qs
    q_ref[0] = (q * q_scale).astype(q_ref.dtype)


def _attn_kernel(q_ref, *refs, n_kv, tu, kb):
    kv_refs, o_ref = refs[:2 * n_kv], refs[2 * n_kv]
    s_buf, p_buf, mx_buf, acc_ref, ot_ref, qt_ref = refs[2 * n_kv + 1:]
    hv = o_ref.shape[2] // 2
    hr = kv_refs[1].shape[1] // 2
    tq = q_ref.shape[1]
    blocks = []
    off = 0
    for j in range(n_kv):
        k_ref, vt_ref = kv_refs[2 * j], kv_refs[2 * j + 1]
        lk = k_ref.shape[1]
        for s0 in range(0, lk, kb):
            blocks.append((k_ref, vt_ref, s0, off + s0, min(kb, lk - s0)))
        off += lk
    units = [(qi, hh) for qi in range(tq // tu) for hh in range(2)]
    n = len(units)

    qt_ref[...] = q_ref[0].T

    def stage_qk(u, bi):
        qi, hh = units[u]
        k_ref, _, s0, g0, size = blocks[bi]
        lanes = slice(hh * HEAD_PAD, (hh + 1) * HEAD_PAD)
        s = _dot(k_ref[0, s0:s0 + size, lanes], qt_ref[lanes, qi * tu:(qi + 1) * tu])
        s_buf[g0:g0 + size, :] = s
        m = jnp.max(s, axis=0, keepdims=True)
        mx_buf[u % 2] = m if bi == 0 else jnp.maximum(mx_buf[u % 2], m)

    def stage_exp(u, bi):
        _, _, _, g0, size = blocks[bi]
        p_buf[g0:g0 + size, :] = jnp.exp2(s_buf[g0:g0 + size, :] - mx_buf[u % 2]).astype(BF16)

    def stage_pv(u, bi):
        qi, hh = units[u]
        _, vt_ref, s0, g0, size = blocks[bi]
        pv = _dot(vt_ref[0, hh * hr:(hh + 1) * hr, s0:s0 + size], p_buf[g0:g0 + size, :])
        if bi > 0:
            pv = pv + acc_ref[...]
        if bi < len(blocks) - 1:
            acc_ref[...] = pv
        else:
            ot_ref[hh * hv:(hh + 1) * hv, qi * tu:(qi + 1) * tu] = pv[:hv] * (1.0 / pv[hv:hv + 1])

    for t in range(n + 2):
        for bi in range(len(blocks)):
            if 0 <= t - 2 < n:
                stage_pv(t - 2, bi)
            if 0 <= t - 1 < n:
                stage_exp(t - 1, bi)
            if t < n:
                stage_qk(t, bi)
    o_ref[0] = ot_ref[...].T.astype(o_ref.dtype)


def _attention(q, kvs, tq, tu):
    b, l, hp = q.shape
    heads = hp // HEAD_PAD
    tq = min(tq, l)
    tu = min(tu, tq)
    in_specs = [pl.BlockSpec((1, tq, 2 * HEAD_PAD), lambda bi, h, i: (bi, i, h))]
    args = [q]
    lk_total = 0
    for k, vt in kvs:
        lk = k.shape[1]
        lk_total += lk
        in_specs.append(pl.BlockSpec((1, lk, 2 * HEAD_PAD), lambda bi, h, i: (bi, 0, h)))
        in_specs.append(pl.BlockSpec((1, 2 * VT_ROWS, lk), lambda bi, h, i: (bi, h, 0)))
        args += [k, vt]
    return pl.pallas_call(
        functools.partial(_attn_kernel, n_kv=len(kvs), tu=tu, kb=KB_ATTN),
        grid=(b, heads // 2, l // tq),
        in_specs=in_specs,
        out_specs=pl.BlockSpec((1, tq, 2 * QK_NOPE), lambda bi, h, i: (bi, i, h)),
        out_shape=jax.ShapeDtypeStruct((b, l, heads * QK_NOPE), BF16),
        scratch_shapes=[pltpu.VMEM((lk_total, tu), F32), pltpu.VMEM((lk_total, tu), BF16),
                        pltpu.VMEM((2, 1, tu), F32), pltpu.VMEM((VT_ROWS, tu), F32),
                        pltpu.VMEM((2 * QK_NOPE, tq), F32),
                        pltpu.VMEM((2 * HEAD_PAD, tq), BF16)],
        compiler_params=_params(("parallel", "parallel", "parallel")),
        name="attention",
    )(*args)


def _merge_kernel(x_ref, h_ref, gt_ref, ab_ref, p_ref, pp_ref, pn_ref, cw_ref,
                  yb_ref, yc_ref, yd_ref, wg_ref, wb_ref, wo_ref, o_ref, *, n_tiles):
    i = pl.program_id(1)
    p = p_ref[0]
    p_prev = jnp.where(i > 0, pp_ref[0, SUBLANES - 1:, :], 0.0)
    p_next = jnp.where(i < n_tiles - 1, pn_ref[0, 0:1, :], 0.0)
    cw = cw_ref[...]
    conv = cw[0:1] * _shift_down(p, p_prev, 1) + cw[1:2] * p + cw[2:3] * _shift_up(p, p_next)
    y_a = (ab_ref[0] * conv).astype(BF16)

    x = x_ref[0]
    d = x.shape[1]
    h = h_ref[0]
    merged = None
    for n, y in enumerate((y_a, yb_ref[0], yc_ref[0], yd_ref[0])):
        gate = jax.nn.sigmoid(_dot(h, wg_ref[:, n * d:(n + 1) * d]))
        term = gate * _dot(y, wb_ref[n])
        merged = term if merged is None else merged + term
    o_ref[0] = x + gt_ref[0] * _dot(merged.astype(BF16), wo_ref[...])


def _merge(x, h, mod3, row_of, z_conv, conv_w, ys, wg, wb, wo, layer, tm):
    b, l, d = x.shape
    tm = min(tm, l)
    c = ys[0].shape[2]
    tok = lambda n: pl.BlockSpec((1, tm, n), lambda bi, i: (bi, i, 0))
    modv = lambda j: pl.BlockSpec((1, 1, d), lambda bi, i: (row_of(bi), 0, j))
    prev, nxt = _halo_specs(tm, l, c, 1)
    return pl.pallas_call(
        functools.partial(_merge_kernel, n_tiles=l // tm),
        grid=(b, l // tm),
        in_specs=[tok(d), tok(d), modv(2),
                  tok(c), pl.BlockSpec((1, tm, c), lambda bi, i: (bi, i, 1)), prev, nxt, _const_spec(conv_w.shape),
                  tok(c), tok(c), tok(c),
                  _layer_spec(wg, layer), _layer_spec(wb, layer), _layer_spec(wo, layer)],
        out_specs=tok(d),
        out_shape=jax.ShapeDtypeStruct((b, l, d), F32),
        compiler_params=_params(("parallel", "parallel")),
        name="merge",
    )(x, h, mod3, z_conv, z_conv, z_conv, z_conv, conv_w, *ys, wg, wb, wo)


def _ffn_kernel(x_ref, sh_ref, sc_ref, gt_ref, g_ref, w1_ref, w3_ref, w2_ref, fg_ref, o_ref, *, final_norm):
    x = x_ref[0]
    h = _norm_mod(x, g_ref[...], sh_ref[0], sc_ref[0]).astype(BF16)
    a = _dot(h, w1_ref[...])
    u = (a * jax.nn.sigmoid(a) * _dot(h, w3_ref[...])).astype(BF16)
    y = x + gt_ref[0] * _dot(u, w2_ref[...])
    if final_norm:
        y = _rms(y, fg_ref[...])
    o_ref[0] = y


def _ffn(x, mod3, row_of, g, w1, w3, w2, layer, final_g, final_norm, tm):
    b, l, d = x.shape
    tm = min(tm, l)
    tok = pl.BlockSpec((1, tm, d), lambda bi, i: (bi, i, 0))
    modv = lambda j: pl.BlockSpec((1, 1, d), lambda bi, i: (row_of(bi), 0, j))
    return pl.pallas_call(
        functools.partial(_ffn_kernel, final_norm=final_norm),
        grid=(b, l // tm),
        in_specs=[tok, modv(3), modv(4), modv(5), _const_spec((1, d)),
                  _layer_spec(w1, layer), _layer_spec(w3, layer), _layer_spec(w2, layer), _const_spec((1, d))],
        out_specs=tok,
        out_shape=jax.ShapeDtypeStruct((b, l, d), F32),
        compiler_params=_params(("parallel", "parallel")),
        name="ffn",
    )(x, mod3, mod3, mod3, g.reshape(1, d), w1, w3, w2, final_g.reshape(1, d))


def _rotate_half_cols(w):
    q = QK_ROPE // 4
    return jnp.concatenate([-w[:, q:2 * q], w[:, :q], -w[:, 3 * q:], w[:, 2 * q:3 * q]], axis=1)


def _block_diag(w):
    h, hd, _ = w.shape
    eye = jnp.eye(h, dtype=w.dtype)
    return (eye[:, None, :, None] * w[:, :, None, :]).reshape(h * hd, h * hd)


def _layer_weights(w_in_l, lru_w_a, lru_w_x, lru_b_a, lru_b_x, lru_lam, w_q_up, w_kv_up, cmlp_w_s,
                   w_branch, w_out, w_ff1, w_ff3, w_ff2):
    d = w_in_l.shape[0]
    c = d // 2
    o = {}
    i_lru_x, i_kv, i_kr, i_lru_g, i_q = 0, c, c + KV_LORA, c + KV_LORA + QK_ROPE, 2 * c + KV_LORA + QK_ROPE
    i_ab = i_q + Q_LORA
    i_ac, i_ax, i_cu, i_cv, i_gate = i_ab + c, i_ab + 2 * c, i_ab + 3 * c, i_ab + 4 * c, i_ab + 5 * c
    col = lambda s, n: w_in_l[:, s:s + n]
    k_rope = col(i_kr, QK_ROPE)
    mla_pad = jnp.zeros((d, HEAD_PAD - 2 * QK_ROPE), F32)
    o["w_lru"] = jnp.concatenate([col(i_lru_x, c), col(i_lru_g, c)], axis=1).astype(BF16)
    o["w_mla"] = jnp.concatenate([col(i_kv, KV_LORA), col(i_q, Q_LORA), k_rope, _rotate_half_cols(k_rope),
                                  mla_pad], axis=1).astype(BF16)
    o["w_conv"] = w_in_l[:, i_ab:i_cu].astype(BF16)
    o["w_cmlp"] = w_in_l[:, i_cu:i_gate].astype(BF16)
    o["w_gate"] = w_in_l[:, i_gate:].astype(BF16)
    o["lru_wg"] = jnp.stack([jnp.concatenate([_block_diag(lru_w_a[dd]), _block_diag(lru_w_x[dd])], axis=1)
                             for dd in range(2)]).astype(BF16)
    o["lru_bg"] = jnp.concatenate([lru_b_a, lru_b_x], axis=1).reshape(2, 1, 2 * c)
    o["lru_lam"] = lru_lam.reshape(2, 1, c)
    wq = w_q_up.reshape(Q_LORA, MLA_HEADS, QK_NOPE + QK_ROPE)
    rot = jnp.stack([_rotate_half_cols(wq[:, h, QK_NOPE:]) for h in range(MLA_HEADS)], axis=1)
    o["wq"] = jnp.concatenate([wq, rot], axis=2).reshape(Q_LORA, MLA_HEADS * HEAD_PAD).astype(BF16)
    wkv = w_kv_up.reshape(KV_LORA, MLA_HEADS, 2 * QK_NOPE)
    o["wk"] = jnp.concatenate([wkv[:, :, :QK_NOPE], jnp.zeros((KV_LORA, MLA_HEADS, HEAD_PAD - QK_NOPE), F32)],
                              axis=2).reshape(KV_LORA, MLA_HEADS * HEAD_PAD).astype(BF16)
    wv = jnp.concatenate([wkv[:, :, QK_NOPE:], jnp.zeros((KV_LORA, MLA_HEADS, ONES_ROWS), F32)], axis=2)
    o["wvt"] = wv.reshape(KV_LORA, MLA_HEADS * VT_ROWS).T.astype(BF16)
    o["w_s"] = cmlp_w_s.astype(BF16)
    o["w_branch"] = w_branch.astype(BF16)
    o["w_out"] = w_out.astype(BF16)
    o["w_ff1"] = w_ff1.astype(BF16)
    o["w_ff3"] = w_ff3.astype(BF16)
    o["w_ff2"] = w_ff2.astype(BF16)
    return o


def _rope_tables(n_lat):
    t = jnp.arange(n_lat)
    n_freq = QK_ROPE // 4
    inv = ROPE_THETA ** (-jnp.arange(n_freq, dtype=F32) / n_freq)
    ang_r = (t // GRID_W).astype(F32)[:, None] * inv
    ang_c = (t % GRID_W).astype(F32)[:, None] * inv
    cos = jnp.concatenate([jnp.cos(ang_r)] * 2 + [jnp.cos(ang_c)] * 2, axis=1)
    sin = jnp.concatenate([jnp.sin(ang_r)] * 2 + [jnp.sin(ang_c)] * 2, axis=1)
    ones = jnp.ones((n_lat, QK_NOPE), F32)
    z32 = jnp.zeros((n_lat, HEAD_PAD - QK_NOPE - QK_ROPE), F32)
    q_cos = jnp.concatenate([ones, cos, z32], axis=1)
    q_sin = jnp.concatenate([jnp.zeros((n_lat, QK_NOPE), F32), sin, z32], axis=1)
    k_cs = jnp.concatenate([cos, sin, jnp.zeros((n_lat, HEAD_PAD - 2 * QK_ROPE), F32)], axis=1)
    return q_cos, q_sin, k_cs


def _identity_tables(n):
    q_cos = jnp.concatenate([jnp.ones((n, QK_NOPE + QK_ROPE), F32),
                             jnp.zeros((n, HEAD_PAD - QK_NOPE - QK_ROPE), F32)], axis=1)
    q_sin = jnp.zeros((n, HEAD_PAD), F32)
    k_cs = jnp.concatenate([jnp.ones((n, QK_ROPE), F32), jnp.zeros((n, HEAD_PAD - QK_ROPE), F32)], axis=1)
    return q_cos, q_sin, k_cs


TM_PROJ = 512
TM_MIX = 1024
TQ_ATTN = 2048
TU_ATTN = 512
KB_ATTN = 512


def kernel(x, c, ctx, c_ctx, w_mod, b_mod, norm1_g, norm2_g, w_in, conv_a_w, lru_conv_w, lru_conv_b,
           lru_w_a, lru_b_a, lru_w_x, lru_b_x, lru_lam, cmlp_ln_g, cmlp_ln_b, cmlp_w_s, cmlp_b_s,
           mla_q_norm_g, mla_kv_norm_g, mla_w_q_up, mla_w_kv_up, w_branch, w_out, w_ff1, w_ff3, w_ff2,
           final_norm_g):
    bsz, n_lat, d = x.shape
    n_ctx = ctx.shape[1]
    depth = w_in.shape[0]
    cw = d // 2
    q_scale = (QK_NOPE + QK_ROPE) ** -0.5 * LOG2E
    row_in_head = jnp.arange(MLA_HEADS * VT_ROWS) % VT_ROWS
    vone = (row_in_head >= QK_NOPE).astype(F32)[:, None]

    rows = -(-(bsz + 1) // SUBLANES) * SUBLANES
    s_rows = jnp.concatenate([c, c_ctx[None, :], jnp.zeros((rows - bsz - 1, d), F32)], axis=0)
    mod = _modulation(s_rows, w_mod, b_mod)
    lat_row = lambda bi: bi
    ctx_row = lambda bi: bsz

    lat_tabs = _rope_tables(n_lat)
    ctx_tabs = _identity_tables(n_ctx)
    zero_state = jnp.zeros((bsz, 1, cw), F32)

    w = jax.vmap(_layer_weights)(w_in, lru_w_a, lru_w_x, lru_b_a, lru_b_x, lru_lam, mla_w_q_up, mla_w_kv_up,
                                 cmlp_w_s, w_branch, w_out, w_ff1, w_ff3, w_ff2)

    xc = ctx
    for l in range(depth):
        last = l == depth - 1
        mod3 = mod[l].reshape(rows, 1, 6 * d)
        cmlp = (cmlp_ln_g[l], cmlp_ln_b[l], cmlp_b_s[l])
        mla = (mla_q_norm_g[l], mla_kv_norm_g[l], vone)
        lru = (lru_conv_w[l], lru_conv_b[l], w["lru_wg"], l, w["lru_bg"][l], w["lru_lam"][l], TM_MIX)
        gate_w = (w["w_gate"], w["w_branch"], w["w_out"], l, TM_PROJ)
        ffn_w = (w["w_ff1"], w["w_ff3"], w["w_ff2"], l, final_norm_g)

        hc, zc_lru, zc_conv, yc_c, qc, kc, vc = _inproj(xc, mod3, ctx_row, norm1_g[l], w, l, cmlp, mla, ctx_tabs,
                                                        q_scale, TM_PROJ)
        yc_b, hc_f, hc_b = _lru_mixer(zc_lru, zero_state, zero_state, *lru)

        h, z_lru, z_conv, y_c, q, k, v = _inproj(x, mod3, lat_row, norm1_g[l], w, l, cmlp, mla, lat_tabs,
                                                 q_scale, TM_PROJ)
        y_b, _, _ = _lru_mixer(z_lru, hc_f, hc_b, *lru)
        y_d = _attention(q, [(k, v), (kc, vc)], TQ_ATTN, TU_ATTN)
        x = _merge(x, h, mod3, lat_row, z_conv, conv_a_w[l], (y_b, y_c, y_d), *gate_w)
        x = _ffn(x, mod3, lat_row, norm2_g[l], *ffn_w, last, TM_PROJ)

        if not last:
            yc_d = _attention(qc, [(kc, vc)], TQ_ATTN, TU_ATTN)
            xc = _merge(xc, hc, mod3, ctx_row, zc_conv, conv_a_w[l], (yc_b, yc_c, yc_d), *gate_w)
            xc = _ffn(xc, mod3, ctx_row, norm2_g[l], *ffn_w, False, TM_PROJ)
    return x
```
